```python
import jax, jax.numpy as jnp
from jax import lax
import numpy as np

D_MODEL = 1024
BATCH = 2
SEQ = 8192
DEPTH = 4

N_MEM = 256
D_FF = 2816
N_BRANCH = 3
NORM_EPS = 1e-6
MLA_HEADS = 8
MLA_NOPE = 64
MLA_ROPE = 32
MLA_V = 64
MLA_Q_RANK = 256
MLA_KV_RANK = 128
ROPE_THETA = 10000.0
Q_BLOCK = 128
MLA_W = MLA_HEADS * MLA_V
GLA_HEADS = 4
GLA_DK = 64
GLA_DV = 128
GLA_GATE_RANK = 16
GLA_GATE_NORM = 16.0
GLA_CHUNK = 64
GLA_W = GLA_HEADS * GLA_DV
RWKV_HEADS = 8
RWKV_N = 64
RWKV_DECAY_RANK = 64
RWKV_ICLR_RANK = 64
RWKV_GATE_RANK = 160
RWKV_GN_EPS = 64e-5
RWKV_W = RWKV_HEADS * RWKV_N
RWKV_COLS = 3 * RWKV_W + RWKV_DECAY_RANK + RWKV_ICLR_RANK + RWKV_GATE_RANK
MEM_HEADS = 4
MEM_HD = D_MODEL // MEM_HEADS
IN_SPLITS = (MLA_Q_RANK, MLA_KV_RANK, MLA_ROPE,
             GLA_HEADS * GLA_DK, GLA_HEADS * GLA_DK, GLA_W, GLA_W, GLA_GATE_RANK,
             RWKV_COLS, N_BRANCH * D_MODEL)
D_IN = sum(IN_SPLITS)
BRANCH_W = MLA_W + GLA_W + RWKV_W

kernel_name = "hybrid_mla_gla_rwkv7_gated_trunk"


def _offsets(sizes):
    out, acc = [], 0
    for s in sizes[:-1]:
        acc += s
        out.append(acc)
    return out


def _rmsnorm(x, g, eps=NORM_EPS):
    xf = x.astype(jnp.float32)
    y = xf * lax.rsqrt(jnp.mean(xf * xf, axis=-1, keepdims=True) + eps)
    return y.astype(x.dtype) * g


def _rope(t, cos, sin):
    t1, t2 = jnp.split(t, 2, axis=-1)
    return jnp.concatenate([t1 * cos - t2 * sin, t1 * sin + t2 * cos], axis=-1).astype(t.dtype)


def _swiglu(h, w_in, w_out):
    g, u = jnp.split(h @ w_in, 2, axis=-1)
    return (jax.nn.silu(g) * u) @ w_out


def _mla_branch(c_q, c_kv, k_rope_raw, q_norm, w_uq, kv_norm, w_ukv, cos, sin):
    B, S, _ = c_q.shape
    H = MLA_HEADS
    q = (_rmsnorm(c_q, q_norm) @ w_uq).reshape(B, S, H, MLA_NOPE + MLA_ROPE)
    q_nope = q[..., :MLA_NOPE]
    q_rot = _rope(q[..., MLA_NOPE:], cos[:, :, None, :], sin[:, :, None, :])
    kv = (_rmsnorm(c_kv, kv_norm) @ w_ukv).reshape(B, S, H, MLA_NOPE + MLA_V)
    k_nope, v = kv[..., :MLA_NOPE], kv[..., MLA_NOPE:]
    k_rot = _rope(k_rope_raw, cos, sin)
    scale = (MLA_NOPE + MLA_ROPE) ** -0.5
    nb = S // Q_BLOCK
    qn_b = q_nope.reshape(B, nb, Q_BLOCK, H, MLA_NOPE).swapaxes(0, 1)
    qr_b = q_rot.reshape(B, nb, Q_BLOCK, H, MLA_ROPE).swapaxes(0, 1)
    k_pos = jnp.arange(S)

    def block(args):
        qn, qr, blk = args
        s = (jnp.einsum('bqhd,bkhd->bhqk', qn, k_nope)
             + jnp.einsum('bqhr,bkr->bhqk', qr, k_rot)).astype(jnp.float32) * scale
        q_pos = blk * Q_BLOCK + jnp.arange(Q_BLOCK)
        s = jnp.where(k_pos[None, :] <= q_pos[:, None], s, -1e30)
        p = jax.nn.softmax(s, axis=-1).astype(v.dtype)
        return jnp.einsum('bhqk,bkhd->bqhd', p, v)

    o = lax.map(block, (qn_b, qr_b, jnp.arange(nb)))
    return o.swapaxes(0, 1).reshape(B, S, MLA_W)


def _gla_branch(q, k, v, g, a_lr, w_alpha, b_alpha, norm_g):
    B, S, _ = q.shape
    H, C = GLA_HEADS, GLA_CHUNK
    nc = S // C
    q = q.reshape(B, S, H, GLA_DK) * GLA_DK ** -0.5
    k = k.reshape(B, S, H, GLA_DK)
    v = v.reshape(B, S, H, GLA_DV)
    log_a = (jax.nn.log_sigmoid((a_lr @ w_alpha + b_alpha).astype(jnp.float32))
             / GLA_GATE_NORM).reshape(B, S, H, GLA_DK)

    def to_chunks(t):
        return t.reshape(B, nc, C, H, t.shape[-1]).transpose(1, 0, 3, 2, 4)

    causal = jnp.tril(jnp.ones((C, C), dtype=bool))[:, :, None]

    def step(state, inp):
        qc, kc, vc, lac = inp
        b = jnp.cumsum(lac, axis=2)
        b_last = b[:, :, -1:, :]
        diff = b[:, :, :, None, :] - b[:, :, None, :, :]
        decay = jnp.exp(jnp.where(causal, diff, -jnp.inf))
        attn = jnp.einsum('bhid,bhjd,bhijd->bhij', qc, kc, decay)
        o = (jnp.einsum('bhij,bhjv->bhiv', attn, vc)
             + jnp.einsum('bhid,bhdv->bhiv', qc * jnp.exp(b), state))
        state = (state * jnp.exp(b_last)[:, :, 0, :, None]
                 + jnp.einsum('bhjd,bhjv->bhdv', kc * jnp.exp(b_last - b), vc))
        return state, o

    s0 = jnp.zeros((B, H, GLA_DK, GLA_DV), jnp.float32)
    _, o = lax.scan(step, s0, (to_chunks(q), to_chunks(k), to_chunks(v), to_chunks(log_a)))
    o = o.transpose(1, 0, 3, 2, 4).reshape(B, S, H, GLA_DV).astype(v.dtype)
    o = _rmsnorm(o, norm_g).reshape(B, S, GLA_W)
    return o * jax.nn.silu(g)


def _rwkv_branch(z, mu, w0, w_decay, a0, w_iclr, w_gate, k_k, k_a, r_k, ln_w, ln_b):
    B, S, _ = z.shape
    H, N = RWKV_HEADS, RWKV_N
    z_prev = jnp.pad(z, ((0, 0), (1, 0), (0, 0)))[:, :-1]
    z = z + mu * (z_prev - z)
    r, k, v, xw, xa, xg = jnp.split(
        z, _offsets((RWKV_W, RWKV_W, RWKV_W, RWKV_DECAY_RANK, RWKV_ICLR_RANK, RWKV_GATE_RANK)), axis=-1)
    w_log = -jax.nn.softplus(-(w0 + jnp.tanh(xw) @ w_decay).astype(jnp.float32)) - 0.5
    decay = jnp.exp(-jnp.exp(w_log))
    a = jax.nn.sigmoid(a0 + xa @ w_iclr)
    g = jax.nn.sigmoid(xg) @ w_gate
    heads = lambda t: t.reshape(B, S, H, N)
    kk = heads(k * k_k).astype(jnp.float32)
    kk = kk / jnp.maximum(jnp.linalg.norm(kk, axis=-1, keepdims=True), 1e-12)
    k = k * (1 + (a - 1) * k_a)
    r_h, k_h, v_h, a_h, w_h = heads(r), heads(k), heads(v), heads(a), heads(decay)
    seq_major = lambda t: jnp.moveaxis(t.astype(jnp.float32), 1, 0)

    def step(state, inp):
        r_t, w_t, k_t, v_t, kk_t, a_t = inp
        sa = jnp.einsum('bhij,bhj->bhi', state, kk_t)
        state = (state * w_t[:, :, None, :]
                 - sa[..., None] * (kk_t * a_t)[:, :, None, :]
                 + v_t[..., None] * k_t[:, :, None, :])
        return state, jnp.einsum('bhij,bhj->bhi', state, r_t)

    s0 = jnp.zeros((B, H, N, N), jnp.float32)
    _, y = lax.scan(step, s0, (seq_major(r_h), seq_major(w_h), seq_major(k_h),
                               seq_major(v_h), seq_major(kk), seq_major(a_h)))
    y = jnp.moveaxis(y, 0, 1)
    mean = jnp.mean(y, axis=-1, keepdims=True)
    var = jnp.mean(jnp.square(y - mean), axis=-1, keepdims=True)
    yn = ((y - mean) * lax.rsqrt(var + RWKV_GN_EPS)).reshape(B, S, RWKV_W).astype(z.dtype) * ln_w + ln_b
    bonus = jnp.sum(r_h * k_h * r_k.reshape(H, N), axis=-1, keepdims=True) * v_h
    return (yn + bonus.reshape(B, S, RWKV_W)) * g


def _mixer_block(h, cos, sin, w_in, mla_q_norm, mla_w_uq, mla_kv_norm, mla_w_ukv,
                 gla_w_alpha, gla_b_alpha, gla_norm,
                 rwkv_mu, rwkv_w0, rwkv_w_decay, rwkv_a0, rwkv_w_iclr, rwkv_w_gate,
                 rwkv_k_k, rwkv_k_a, rwkv_r_k, rwkv_ln_w, rwkv_ln_b, w_branch, w_out):
    B, S, _ = h.shape
    (c_q, c_kv, k_rope, gla_q, gla_k, gla_v, gla_g, gla_a, rwkv_z, gate_logits) = jnp.split(
        h @ w_in, _offsets(IN_SPLITS), axis=-1)
    o_a = _mla_branch(c_q, c_kv, k_rope, mla_q_norm, mla_w_uq, mla_kv_norm, mla_w_ukv, cos, sin)
    o_b = _gla_branch(gla_q, gla_k, gla_v, gla_g, gla_a, gla_w_alpha, gla_b_alpha, gla_norm)
    o_c = _rwkv_branch(rwkv_z, rwkv_mu, rwkv_w0, rwkv_w_decay, rwkv_a0, rwkv_w_iclr, rwkv_w_gate,
                       rwkv_k_k, rwkv_k_a, rwkv_r_k, rwkv_ln_w, rwkv_ln_b)
    y_a = o_a @ w_branch[:MLA_W]
    y_b = o_b @ w_branch[MLA_W:MLA_W + GLA_W]
    y_c = o_c @ w_branch[MLA_W + GLA_W:]
    gates = jax.nn.sigmoid(gate_logits).reshape(B, S, N_BRANCH, D_MODEL)
    merged = gates[:, :, 0] * y_a + gates[:, :, 1] * y_b + gates[:, :, 2] * y_c
    return merged @ w_out


def _mem_attn(h, mem_n, wq, wkv, wo):
    B, S, _ = h.shape
    q = (h @ wq).reshape(B, S, MEM_HEADS, MEM_HD)
    kv = (mem_n @ wkv).reshape(B, mem_n.shape[1], 2, MEM_HEADS, MEM_HD)
    k, v = kv[:, :, 0], kv[:, :, 1]
    s = jnp.einsum('bqhd,bkhd->bhqk', q, k).astype(jnp.float32) * MEM_HD ** -0.5
    p = jax.nn.softmax(s, axis=-1).astype(v.dtype)
    return jnp.einsum('bhqk,bkhd->bqhd', p, v).reshape(B, S, D_MODEL) @ wo


def setup_inputs(seed: int = 0) -> dict:
    key = jax.random.key(seed)
    ks = iter(jax.random.split(key, 48))
    f32 = jnp.float32
    L = DEPTH

    def nrm(shape, fan_in, s=1.0):
        return jax.random.normal(next(ks), shape, f32) * (s * fan_in ** -0.5)

    def gain(shape):
        return 1.0 + 0.05 * jax.random.normal(next(ks), shape, f32)

    def small(shape, s):
        return s * jax.random.normal(next(ks), shape, f32)

    def unif(shape, lo, hi):
        return jax.random.uniform(next(ks), shape, f32, lo, hi)

    x = jax.random.normal(next(ks), (BATCH, SEQ, D_MODEL), f32)
    mem = jax.random.normal(next(ks), (BATCH, N_MEM, D_MODEL), f32)
    start = jax.random.randint(next(ks), (BATCH, 1), 0, 4096, dtype=jnp.int32)
    positions = start + jnp.arange(SEQ, dtype=jnp.int32)[None, :]
    return {
        "x": x,
        "mem": mem,
        "positions": positions,
        "norm_g": gain((L, 8, D_MODEL)),
        "w_ffn_in": nrm((L, 2, D_MODEL, 2 * D_FF), D_MODEL),
        "w_ffn_out": nrm((L, 2, D_FF, D_MODEL), D_FF),
        "w_in": nrm((L, D_MODEL, D_IN), D_MODEL),
        "mla_q_norm": gain((L, MLA_Q_RANK)),
        "mla_w_uq": nrm((L, MLA_Q_RANK, MLA_HEADS * (MLA_NOPE + MLA_ROPE)), MLA_Q_RANK),
        "mla_kv_norm": gain((L, MLA_KV_RANK)),
        "mla_w_ukv": nrm((L, MLA_KV_RANK, MLA_HEADS * (MLA_NOPE + MLA_V)), MLA_KV_RANK),
        "gla_w_alpha": nrm((L, GLA_GATE_RANK, GLA_HEADS * GLA_DK), GLA_GATE_RANK),
        "gla_b_alpha": unif((L, GLA_HEADS * GLA_DK), -1.0, 4.0),
        "gla_norm": gain((L, GLA_DV)),
        "rwkv_mu": unif((L, RWKV_COLS), 0.0, 1.0),
        "rwkv_w0": unif((L, RWKV_W), -4.0, 0.0),
        "rwkv_w_decay": nrm((L, RWKV_DECAY_RANK, RWKV_W), RWKV_DECAY_RANK, 0.5),
        "rwkv_a0": small((L, RWKV_W), 0.1),
        "rwkv_w_iclr": nrm((L, RWKV_ICLR_RANK, RWKV_W), RWKV_ICLR_RANK, 0.5),
        "rwkv_w_gate": nrm((L, RWKV_GATE_RANK, RWKV_W), RWKV_GATE_RANK),
        "rwkv_k_k": 0.85 + small((L, RWKV_W), 0.1),
        "rwkv_k_a": gain((L, RWKV_W)),
        "rwkv_r_k": small((L, RWKV_W), 0.1),
        "rwkv_ln_w": gain((L, RWKV_W)),
        "rwkv_ln_b": small((L, RWKV_W), 0.02),
        "w_branch": nrm((L, BRANCH_W, D_MODEL), RWKV_W),
        "w_out": nrm((L, D_MODEL, D_MODEL), D_MODEL),
        "mem_norm": gain((L, D_MODEL)),
        "mem_wq": nrm((L, D_MODEL, D_MODEL), D_MODEL),
        "mem_wkv": nrm((L, D_MODEL, 2 * D_MODEL), D_MODEL),
        "mem_wo": nrm((L, D_MODEL, D_MODEL), D_MODEL),
    }


def reference(x, mem, positions, norm_g, w_ffn_in, w_ffn_out, w_in,
              mla_q_norm, mla_w_uq, mla_kv_norm, mla_w_ukv,
              gla_w_alpha, gla_b_alpha, gla_norm,
              rwkv_mu, rwkv_w0, rwkv_w_decay, rwkv_a0, rwkv_w_iclr, rwkv_w_gate,
              rwkv_k_k, rwkv_k_a, rwkv_r_k, rwkv_ln_w, rwkv_ln_b,
              w_branch, w_out, mem_norm, mem_wq, mem_wkv, mem_wo):
    inv_freq = ROPE_THETA ** (-jnp.arange(0, MLA_ROPE, 2, dtype=jnp.float32) / MLA_ROPE)
    ang = positions.astype(jnp.float32)[..., None] * inv_freq
    cos, sin = jnp.cos(ang), jnp.sin(ang)
    for l in range(DEPTH):
        ng = norm_g[l]
        x = x + 0.5 * _rmsnorm(_swiglu(_rmsnorm(x, ng[0]), w_ffn_in[l, 0], w_ffn_out[l, 0]), ng[1])
        y = _mixer_block(_rmsnorm(x, ng[2]), cos, sin, w_in[l],
                         mla_q_norm[l], mla_w_uq[l], mla_kv_norm[l], mla_w_ukv[l],
                         gla_w_alpha[l], gla_b_alpha[l], gla_norm[l],
                         rwkv_mu[l], rwkv_w0[l], rwkv_w_decay[l], rwkv_a0[l], rwkv_w_iclr[l],
                         rwkv_w_gate[l], rwkv_k_k[l], rwkv_k_a[l], rwkv_r_k[l],
                         rwkv_ln_w[l], rwkv_ln_b[l], w_branch[l], w_out[l])
        x = x + _rmsnorm(y, ng[3])
        y = _mem_attn(_rmsnorm(x, ng[4]), _rmsnorm(mem, mem_norm[l]), mem_wq[l], mem_wkv[l], mem_wo[l])
        x = x + _rmsnorm(y, ng[5])
        x = x + 0.5 * _rmsnorm(_swiglu(_rmsnorm(x, ng[6]), w_ffn_in[l, 1], w_ffn_out[l, 1]), ng[7])
    return x
```

```python
import functools

import jax
import jax.numpy as jnp
from jax import lax
from jax.experimental import pallas as pl
from jax.experimental.pallas import tpu as pltpu

F32 = jnp.float32
BF16 = jnp.bfloat16
HI = lax.Precision.HIGHEST

D_MODEL = 1024
D_FF = 2816
NORM_EPS = 1e-6
MLA_HEADS = 8
MLA_NOPE = 64
MLA_ROPE = 32
MLA_Q_RANK = 256
MLA_KV_RANK = 128
ROPE_THETA = 10000.0
MLA_SLOT = 128
MLA_W = MLA_HEADS * MLA_SLOT
GLA_HEADS = 4
GLA_DK = 64
GLA_DV = 128
GLA_GATE_RANK = 16
GLA_GATE_NORM = 16.0
GLA_QK = GLA_HEADS * GLA_DK
GLA_W = GLA_HEADS * GLA_DV
RWKV_HEADS = 8
RWKV_N = 64
RWKV_DECAY_RANK = 64
RWKV_ICLR_RANK = 64
RWKV_GATE_RANK = 160
RWKV_GN_EPS = 64e-5
RWKV_W = RWKV_HEADS * RWKV_N
RWKV_MISC = 384
RWKV_COLS_PAD = 3 * RWKV_W + RWKV_MISC
MEM_HEADS = 4
MEM_HD = D_MODEL // MEM_HEADS
N_BRANCH = 3
CHUNK = 64

VMEM_LIMIT_BYTES = 56 * 1024 * 1024


def _cparams(n_axes):
    return pltpu.CompilerParams(
        dimension_semantics=("arbitrary",) * n_axes,
        vmem_limit_bytes=VMEM_LIMIT_BYTES,
    )


def _dot(a, b, precision=None):
    return jnp.dot(a, b, preferred_element_type=F32, precision=precision)


def _dot_nt(a, b, precision=None):
    return lax.dot_general(a, b, (((1,), (1,)), ((), ())),
                           preferred_element_type=F32, precision=precision)


def _dot_tn(a, b, precision=None):
    return lax.dot_general(a, b, (((0,), (0,)), ((), ())),
                           preferred_element_type=F32, precision=precision)


def _rms(x, g, eps=NORM_EPS):
    return x * lax.rsqrt(jnp.mean(x * x, axis=-1, keepdims=True) + eps) * g


def _sigmoid(x):
    return 1.0 / (1.0 + jnp.exp(-x))


def _softplus(x):
    return jnp.maximum(x, 0.0) + jnp.log1p(jnp.exp(-jnp.abs(x)))


def _iota(shape, dim):
    return lax.broadcasted_iota(jnp.int32, shape, dim)


def _segsum(x, e_bf16):
    hi = x.astype(BF16)
    lo = (x - hi.astype(F32)).astype(BF16)
    return _dot(hi, e_bf16) + _dot(lo, e_bf16)


def _vec_spec(n, l):
    return pl.BlockSpec((None, 1, n), lambda *_: (l, 0, 0))


def _mat_spec(r, c, l):
    return pl.BlockSpec((None, r, c), lambda *_: (l, 0, 0))


def _ffn_body(x_ref, gpre_ref, wg_ref, wu_ref, wo_ref, gpost_ref, o_ref, h_ref, acc_ref):
    j = pl.program_id(1)

    @pl.when(j == 0)
    def _():
        h_ref[...] = _rms(x_ref[...], gpre_ref[...]).astype(BF16)
        acc_ref[...] = jnp.zeros_like(acc_ref)

    h = h_ref[...]
    g = _dot(h, wg_ref[...])
    u = _dot(h, wu_ref[...])
    act = (g * _sigmoid(g) * u).astype(BF16)
    acc_ref[...] += _dot(act, wo_ref[...])

    @pl.when(j == pl.num_programs(1) - 1)
    def _():
        o_ref[...] = x_ref[...] + 0.5 * _rms(acc_ref[...], gpost_ref[...])


def _ffn(x, ng, w_in, w_out, l, k, tm, tf):
    T, D = x.shape
    ff = w_out.shape[2]
    nf = ff // tf
    g_pre, g_post = l * 8 + 6 * k, l * 8 + 6 * k + 1
    return pl.pallas_call(
        _ffn_body,
        grid=(T // tm, nf),
        in_specs=[
            pl.BlockSpec((tm, D), lambda i, j: (i, 0)),
            pl.BlockSpec((None, 1, D), lambda i, j: (g_pre, 0, 0)),
            pl.BlockSpec((None, None, D, tf), lambda i, j: (l, k, 0, j)),
            pl.BlockSpec((None, None, D, tf), lambda i, j: (l, k, 0, j + nf)),
            pl.BlockSpec((None, None, tf, D), lambda i, j: (l, k, j, 0)),
            pl.BlockSpec((None, 1, D), lambda i, j: (g_post, 0, 0)),
        ],
        out_specs=pl.BlockSpec((tm, D), lambda i, j: (i, 0)),
        out_shape=jax.ShapeDtypeStruct((T, D), F32),
        scratch_shapes=[pltpu.VMEM((tm, D), BF16), pltpu.VMEM((tm, D), F32)],
        compiler_params=_cparams(2),
        name="ffn",
    )(x, ng, w_in, w_in, w_out, ng)


def _normproj_body(x_ref, g_ref, w_ref, o_ref):
    h = _rms(x_ref[...], g_ref[...]).astype(BF16)
    o_ref[...] = _dot(h, w_ref[...]).astype(o_ref.dtype)


def _normproj(x, ng, g_idx, w, l, tm, name):
    T, D = x.shape
    n = w.shape[2]
    return pl.pallas_call(
        _normproj_body,
        grid=(T // tm,),
        in_specs=[
            pl.BlockSpec((tm, D), lambda i: (i, 0)),
            pl.BlockSpec((None, 1, D), lambda i: (g_idx, 0, 0)),
            _mat_spec(D, n, l),
        ],
        out_specs=pl.BlockSpec((tm, n), lambda i: (i, 0)),
        out_shape=jax.ShapeDtypeStruct((T, n), F32),
        compiler_params=_cparams(1),
        name=name,
    )(x, ng, w)


def _rope_table_body(pos_ref, invf_ref, c_ref, s_ref):
    ang = pos_ref[...] * invf_ref[...]
    lane = _iota(ang.shape, 1)
    rot = (lane >= MLA_NOPE) & (lane < MLA_NOPE + MLA_ROPE)
    c_ref[...] = jnp.where(lane < MLA_NOPE, 1.0, jnp.where(rot, jnp.cos(ang), 0.0))
    s_ref[...] = jnp.where(rot, jnp.sin(ang), 0.0)


def _rope_tables(pos_f32, invf_lane, tm):
    T = pos_f32.shape[0]
    return pl.pallas_call(
        _rope_table_body,
        grid=(T // tm,),
        in_specs=[pl.BlockSpec((tm, 1), lambda i: (i, 0)),
                  pl.BlockSpec((1, MLA_SLOT), lambda i: (0, 0))],
        out_specs=[pl.BlockSpec((tm, MLA_SLOT), lambda i: (i, 0))] * 2,
        out_shape=[jax.ShapeDtypeStruct((T, MLA_SLOT), F32)] * 2,
        compiler_params=_cparams(1),
        name="rope_tables",
    )(pos_f32, invf_lane)


def _mla_prep_body(za_ref, c0_ref, s0_ref, qn_ref, kvn_ref, wqa_ref, wqb_ref, wk_ref, wv_ref,
                   q_ref, k_ref, v_ref):
    za = za_ref[...]
    cq = za[:, :MLA_Q_RANK]
    ckv = za[:, MLA_Q_RANK:MLA_Q_RANK + MLA_KV_RANK]
    kra = za[:, 384:512]
    krb = za[:, 512:640]
    cqn = _rms(cq, qn_ref[...]).astype(BF16)
    ckvn = _rms(ckv, kvn_ref[...]).astype(BF16)
    c0 = c0_ref[...]
    s0 = s0_ref[...]
    c8 = jnp.concatenate([c0] * MLA_HEADS, axis=1)
    s8 = jnp.concatenate([s0] * MLA_HEADS, axis=1)
    scale = (MLA_NOPE + MLA_ROPE) ** -0.5
    q = (_dot(cqn, wqa_ref[...]) * c8 + _dot(cqn, wqb_ref[...]) * s8) * scale
    q_ref[...] = q.astype(BF16)
    krot = kra * c0 + krb * s0
    k = _dot(ckvn, wk_ref[...]) + jnp.concatenate([krot] * MLA_HEADS, axis=1)
    k_ref[...] = k.astype(BF16)
    v_ref[...] = _dot(ckvn, wv_ref[...]).astype(BF16)


def _mla_prep(za, c0, s0, qn, kvn, wqa, wqb, wk, wv, l, tm):
    T = za.shape[0]
    tok = lambda n: pl.BlockSpec((tm, n), lambda i: (i, 0))
    return pl.pallas_call(
        _mla_prep_body,
        grid=(T // tm,),
        in_specs=[tok(za.shape[1]), tok(MLA_SLOT), tok(MLA_SLOT),
                  _vec_spec(MLA_Q_RANK, l), _vec_spec(MLA_KV_RANK, l),
                  _mat_spec(MLA_Q_RANK, MLA_W, l), _mat_spec(MLA_Q_RANK, MLA_W, l),
                  _mat_spec(MLA_KV_RANK, MLA_W, l), _mat_spec(MLA_KV_RANK, MLA_W, l)],
        out_specs=[tok(MLA_W)] * 3,
        out_shape=[jax.ShapeDtypeStruct((T, MLA_W), BF16)] * 3,
        compiler_params=_cparams(1),
        name="mla_prep",
    )(za, c0, s0, qn, kvn, wqa, wqb, wk, wv)


def _mla_attn_body(q_ref, k_ref, v_ref, o_ref, m_ref, l_ref, acc_ref, *, tq):
    qi = pl.program_id(2)
    q = q_ref[...]
    m_ref[...] = jnp.full_like(m_ref, -1e30)
    l_ref[...] = jnp.zeros_like(l_ref)
    acc_ref[...] = jnp.zeros_like(acc_ref)

    def step(ki, diagonal):
        r0 = pl.multiple_of(ki * tq, tq)
        k = k_ref[pl.ds(r0, tq), :]
        v = v_ref[pl.ds(r0, tq), :]
        s = _dot_nt(q, k)
        if diagonal:
            s = jnp.where(_iota(s.shape, 1) <= _iota(s.shape, 0), s, -1e30)
        m_old = m_ref[...]
        m_new = jnp.maximum(m_old, jnp.max(s, axis=-1, keepdims=True))
        alpha = jnp.exp(m_old - m_new)
        p = jnp.exp(s - m_new)
        l_ref[...] = alpha * l_ref[...] + jnp.sum(p, axis=-1, keepdims=True)
        acc_ref[...] = alpha * acc_ref[...] + _dot(p.astype(BF16), v)
        m_ref[...] = m_new

    def body(ki, carry):
        step(ki, False)
        return carry

    lax.fori_loop(0, qi, body, 0)
    step(qi, True)
    o_ref[...] = (acc_ref[...] / l_ref[...]).astype(o_ref.dtype)


def _mla_attn(q, k, v, B, S, tq):
    T = q.shape[0]
    nq = S // tq
    return pl.pallas_call(
        functools.partial(_mla_attn_body, tq=tq),
        grid=(B, MLA_HEADS, nq),
        in_specs=[
            pl.BlockSpec((tq, MLA_SLOT), lambda b, h, i: (b * nq + i, h)),
            pl.BlockSpec((S, MLA_SLOT), lambda b, h, i: (b, h)),
            pl.BlockSpec((S, MLA_SLOT), lambda b, h, i: (b, h)),
        ],
        out_specs=pl.BlockSpec((tq, MLA_SLOT), lambda b, h, i: (b * nq + i, h)),
        out_shape=jax.ShapeDtypeStruct((T, MLA_W), BF16),
        scratch_shapes=[pltpu.VMEM((tq, 1), F32), pltpu.VMEM((tq, 1), F32),
                        pltpu.VMEM((tq, MLA_SLOT), F32)],
        compiler_params=_cparams(3),
        name="mla_attn",
    )(q, k, v)


def _gla_body(v_ref, g_ref, q_ref, k_ref, a_ref, walpha_ref, balpha_ref, gn_ref, o_ref, ht_ref,
              *, n_chunks):
    @pl.when(pl.program_id(1) == 0)
    def _():
        ht_ref[...] = jnp.zeros_like(ht_ref)

    C = CHUNK
    walpha = walpha_ref[...]
    balpha = balpha_ref[...]
    gn = gn_ref[...]
    tri = (_iota((C, C), 1) <= _iota((C, C), 0)).astype(F32)
    st_r, st_c = _iota((GLA_HEADS * C, GLA_QK), 0), _iota((GLA_HEADS * C, GLA_QK), 1)
    head_mask = (st_r >> 6) == (st_c >> 6)
    cz_r, cz_c = _iota((GLA_HEADS * C, C), 0), _iota((GLA_HEADS * C, C), 1)
    causal = cz_c <= (cz_r & (C - 1))
    ht_r, ht_c = _iota((GLA_W, GLA_QK), 0), _iota((GLA_W, GLA_QK), 1)
    state_mask = (ht_r >> 7) == (ht_c >> 6)

    def chunk(c, carry):
        r0 = pl.multiple_of(c * C, C)
        rows = pl.ds(r0, C)
        x = _dot(a_ref[rows, :], walpha, HI) + balpha
        la = -_softplus(-x) * (1.0 / GLA_GATE_NORM)
        b = _dot(tri, la, HI)
        b_last = b[C - 1:C, :]
        b_mid = b[C // 2 - 1:C // 2, :]
        q = q_ref[rows, :] * GLA_DK ** -0.5
        k = k_ref[rows, :]
        v = v_ref[rows, :].astype(BF16)
        qe = (q * jnp.exp(b)).astype(BF16)
        qm = q * jnp.exp(b - b_mid)
        km = (k * jnp.exp(b_mid - b)).astype(BF16)
        ke = (k * jnp.exp(b_last - b)).astype(BF16)
        qst = jnp.where(head_mask, jnp.concatenate([qm] * GLA_HEADS, axis=0), 0.0).astype(BF16)
        sc = jnp.where(causal, _dot_nt(qst, km), 0.0).astype(BF16)
        ost = _dot(sc, v)
        o_intra = jnp.concatenate(
            [ost[h * C:(h + 1) * C, h * GLA_DV:(h + 1) * GLA_DV] for h in range(GLA_HEADS)], axis=1)
        ht = ht_ref[...]
        o = o_intra + _dot_nt(qe, ht.astype(BF16))
        ht_ref[...] = ht * jnp.exp(b_last) + jnp.where(state_mask, _dot_tn(v, ke), 0.0)
        outs = []
        for h in range(GLA_HEADS):
            oh = o[:, h * GLA_DV:(h + 1) * GLA_DV]
            outs.append(oh * lax.rsqrt(jnp.mean(oh * oh, axis=-1, keepdims=True) + NORM_EPS))
        on = jnp.concatenate(outs, axis=1) * gn
        g = g_ref[rows, :]
        o_ref[rows, :] = (on * (g * _sigmoid(g))).astype(o_ref.dtype)
        return carry

    lax.fori_loop(0, n_chunks, chunk, 0)


def _gla(zb, walpha, balpha, gn, l, B, S, ts):
    T = zb.shape[0]
    ns = S // ts
    row = lambda b, s: b * ns + s
    return pl.pallas_call(
        functools.partial(_gla_body, n_chunks=ts // CHUNK),
        grid=(B, ns),
        in_specs=[
            pl.BlockSpec((ts, GLA_W), lambda b, s: (row(b, s), 0)),
            pl.BlockSpec((ts, GLA_W), lambda b, s: (row(b, s), 1)),
            pl.BlockSpec((ts, GLA_QK), lambda b, s: (row(b, s), 4)),
            pl.BlockSpec((ts, GLA_QK), lambda b, s: (row(b, s), 5)),
            pl.BlockSpec((ts, 128), lambda b, s: (row(b, s), 12)),
            _mat_spec(128, GLA_QK, l), _vec_spec(GLA_QK, l), _vec_spec(GLA_W, l),
        ],
        out_specs=pl.BlockSpec((ts, GLA_W), lambda b, s: (row(b, s), 0)),
        out_shape=jax.ShapeDtypeStruct((T, GLA_W), BF16),
        scratch_shapes=[pltpu.VMEM((GLA_W, GLA_QK), F32)],
        compiler_params=_cparams(2),
        name="gla",
    )(zb, zb, zb, zb, zb, walpha, balpha, gn)


def _rwkv_body(z_ref, mu_ref, w0_ref, wdec_ref, a0_ref, wiclr_ref, wgate_ref, kk_ref, ka_ref,
               rk_ref, lnw_ref, lnb_ref, o_ref, h_ref, zlast_ref, *, n_chunks):
    @pl.when(pl.program_id(1) == 0)
    def _():
        h_ref[...] = jnp.zeros_like(h_ref)
        zlast_ref[...] = jnp.zeros_like(zlast_ref)

    C, W, H = CHUNK, RWKV_W, RWKV_HEADS
    mu = mu_ref[...]
    w0, a0 = w0_ref[...], a0_ref[...]
    wdec, wiclr = wdec_ref[...], wiclr_ref[...]
    wgate = wgate_ref[...]
    k_k, k_a, r_k = kk_ref[...], ka_ref[...], rk_ref[...]
    ln_w, ln_b = lnw_ref[...], lnb_ref[...]

    tri = (_iota((C, C), 1) <= _iota((C, C), 0)).astype(F32)
    sq_r, sq_c = _iota((W, W), 0), _iota((W, W), 1)
    head_blk = (sq_r >> 6) == (sq_c >> 6)
    eye_w = sq_r == sq_c
    e_seg = head_blk.astype(BF16)
    wd_t, wd_s = _iota((C, W), 0), _iota((C, W), 1) & (C - 1)
    strict = wd_s < wd_t
    incl = wd_s <= wd_t
    eye_wide = (wd_s == wd_t).astype(F32)
    row0 = _iota((C, RWKV_COLS_PAD), 0) == 0

    def bd(x):
        return jnp.where(head_blk, jnp.concatenate([x] * H, axis=0), 0.0).astype(BF16)

    def chunk(c, carry):
        r0 = pl.multiple_of(c * C, C)
        rows = pl.ds(r0, C)
        z = z_ref[rows, :]
        zp = jnp.where(row0, zlast_ref[0:1, :], pltpu.roll(z, 1, 0))
        zlast_ref[0:1, :] = z[C - 1:C, :]
        z = z + mu * (zp - z)
        r = z[:, 0:W]
        k = z[:, W:2 * W]
        v = z[:, 2 * W:3 * W]
        m0 = z[:, 3 * W:3 * W + 128]
        m12 = z[:, 3 * W + 128:3 * W + 384]
        w_log = -_softplus(-(w0 + _dot(jnp.tanh(m0), wdec, HI))) - 0.5
        lw = -jnp.exp(w_log)
        a = _sigmoid(a0 + _dot(m0, wiclr, HI))
        g = _dot(_sigmoid(m12).astype(BF16), wgate)
        kk = k * k_k
        kkn = kk / jnp.maximum(jnp.sqrt(_segsum(kk * kk, e_seg)), 1e-12)
        k2 = k * (1.0 + (a - 1.0) * k_a)
        beta = kkn * a
        cs = _dot(tri, lw, HI)
        c_last = cs[C - 1:C, :]
        dec_in = jnp.exp(-cs)
        dec_out = jnp.exp(c_last - cs)
        kt = kkn * jnp.exp(cs - lw)
        rt = r * jnp.exp(cs)
        bh = beta * dec_in
        kh = k2 * dec_in
        kbar = k2 * dec_out
        bbar = beta * dec_out
        lhs = jnp.concatenate([kt, rt], axis=0).astype(BF16)
        rhs = jnp.concatenate([bd(bh), bd(kh)], axis=0)
        sc = _dot_nt(lhs, rhs)
        a_kb = jnp.where(strict, sc[:C, :W], 0.0)
        a_kk = jnp.where(strict, sc[:C, W:], 0.0)
        a_rb = jnp.where(incl, sc[C:, :W], 0.0)
        a_rk = jnp.where(incl, sc[C:, W:], 0.0)
        n = -a_kb
        t = eye_wide + n
        p = _dot(n.astype(BF16), bd(n))
        for _ in range(4):
            tp = _dot(jnp.concatenate([t, p], axis=0).astype(BF16), bd(p))
            t = t + tp[:C]
            p = tp[C:]
        t = t + _dot(t.astype(BF16), bd(p))
        tb = t.astype(BF16)
        bd_v = bd(v)
        akk_v = _dot(a_kk.astype(BF16), bd_v)
        w12 = _dot(tb, jnp.concatenate([bd(akk_v), bd(kt)], axis=1))
        w1, w2 = w12[:, :W], w12[:, W:]
        y0 = _dot(jnp.concatenate([a_rk, a_rb], axis=1).astype(BF16),
                  jnp.concatenate([bd_v, bd(-w1)], axis=0))
        rp = rt - _dot(a_rb.astype(BF16), bd(w2))
        gam = jnp.exp(c_last)
        p_bd = jnp.where(eye_w, gam, 0.0) - jnp.where(
            head_blk, _dot_tn(bbar.astype(BF16), w2.astype(BF16)), 0.0)
        q_bd = jnp.where(
            head_blk,
            _dot_tn(jnp.concatenate([kbar, -bbar], axis=0).astype(BF16),
                    jnp.concatenate([v, w1], axis=0).astype(BF16)), 0.0)
        hs = h_ref[...]
        hb = hs.astype(BF16)
        y = _dot(rp.astype(BF16), hb) + y0
        h_ref[...] = _dot(p_bd.astype(BF16), hb) + q_bd
        mean = _segsum(y, e_seg) * (1.0 / RWKV_N)
        d = y - mean
        var = _segsum(d * d, e_seg) * (1.0 / RWKV_N)
        yn = d * lax.rsqrt(var + RWKV_GN_EPS) * ln_w + ln_b
        bonus = _segsum(r * k2 * r_k, e_seg) * v
        o_ref[rows, :] = ((yn + bonus) * g).astype(o_ref.dtype)
        return carry

    lax.fori_loop(0, n_chunks, chunk, 0)


def _rwkv(zc, p, l, B, S, ts):
    T = zc.shape[0]
    ns = S // ts
    W = RWKV_W
    return pl.pallas_call(
        functools.partial(_rwkv_body, n_chunks=ts // CHUNK),
        grid=(B, ns),
        in_specs=[
            pl.BlockSpec((ts, RWKV_COLS_PAD), lambda b, s: (b * ns + s, 0)),
            _vec_spec(RWKV_COLS_PAD, l), _vec_spec(W, l), _mat_spec(128, W, l),
            _vec_spec(W, l), _mat_spec(128, W, l), _mat_spec(256, W, l),
            _vec_spec(W, l), _vec_spec(W, l), _vec_spec(W, l), _vec_spec(W, l), _vec_spec(W, l),
        ],
        out_specs=pl.BlockSpec((ts, W), lambda b, s: (b * ns + s, 0)),
        out_shape=jax.ShapeDtypeStruct((T, W), BF16),
        scratch_shapes=[pltpu.VMEM((W, W), F32), pltpu.VMEM((8, RWKV_COLS_PAD), F32)],
        compiler_params=_cparams(2),
        name="rwkv",
    )(zc, p["mu"], p["w0"], p["wdec"], p["a0"], p["wiclr"], p["wgate"], p["k_k"], p["k_a"],
      p["r_k"], p["ln_w"], p["ln_b"])


def _merge_body(x_ref, oa_ref, ob_ref, oc_ref, zd_ref, wa_ref, wb_ref, wc_ref, wo_ref, g_ref, o_ref):
    zd = zd_ref[...]
    D = D_MODEL
    merged = (_sigmoid(zd[:, 0:D]) * _dot(oa_ref[...], wa_ref[...])
              + _sigmoid(zd[:, D:2 * D]) * _dot(ob_ref[...], wb_ref[...])
              + _sigmoid(zd[:, 2 * D:3 * D]) * _dot(oc_ref[...], wc_ref[...]))
    y = _dot(merged.astype(BF16), wo_ref[...])
    o_ref[...] = x_ref[...] + _rms(y, g_ref[...])


def _merge(x, oa, ob, oc, zd, wa, wb, wc, wo, ng, l, tm):
    T, D = x.shape
    tok = lambda n: pl.BlockSpec((tm, n), lambda i: (i, 0))
    return pl.pallas_call(
        _merge_body,
        grid=(T // tm,),
        in_specs=[tok(D), tok(MLA_W), tok(GLA_W), tok(RWKV_W), tok(N_BRANCH * D),
                  _mat_spec(MLA_W, D, l), _mat_spec(GLA_W, D, l), _mat_spec(RWKV_W, D, l),
                  _mat_spec(D, D, l), pl.BlockSpec((None, 1, D), lambda i: (l * 8 + 3, 0, 0))],
        out_specs=tok(D),
        out_shape=jax.ShapeDtypeStruct((T, D), F32),
        compiler_params=_cparams(1),
        name="merge",
    )(x, oa, ob, oc, zd, wa, wb, wc, wo, ng)


def _mem_kv_body(mem_ref, g_ref, w_ref, o_ref):
    o_ref[...] = _dot(_rms(mem_ref[...], g_ref[...]).astype(BF16), w_ref[...]).astype(o_ref.dtype)


def _mem_kv(mem, mem_norm, wkv, l):
    B, M, D = mem.shape
    return pl.pallas_call(
        _mem_kv_body,
        grid=(B,),
        in_specs=[pl.BlockSpec((None, M, D), lambda b: (b, 0, 0)), _vec_spec(D, l),
                  _mat_spec(D, 2 * D, l)],
        out_specs=pl.BlockSpec((None, M, 2 * D), lambda b: (b, 0, 0)),
        out_shape=jax.ShapeDtypeStruct((B, M, 2 * D), BF16),
        compiler_params=_cparams(1),
        name="mem_kv",
    )(mem, mem_norm, wkv)


def _mem_attn_body(x_ref, gpre_ref, wq_ref, kv_ref, wo_ref, gpost_ref, o_ref):
    x = x_ref[...]
    D = D_MODEL
    h = _rms(x, gpre_ref[...]).astype(BF16)
    q = (_dot(h, wq_ref[...]) * MEM_HD ** -0.5).astype(BF16)
    kv = kv_ref[...]
    outs = []
    for hh in range(MEM_HEADS):
        sl = slice(hh * MEM_HD, (hh + 1) * MEM_HD)
        s = _dot_nt(q[:, sl], kv[:, sl])
        p = jnp.exp(s - jnp.max(s, axis=-1, keepdims=True))
        o = _dot(p.astype(BF16), kv[:, D + hh * MEM_HD:D + (hh + 1) * MEM_HD])
        outs.append(o / jnp.sum(p, axis=-1, keepdims=True))
    o = jnp.concatenate(outs, axis=1).astype(BF16)
    o_ref[...] = x + _rms(_dot(o, wo_ref[...]), gpost_ref[...])


def _mem_attn(x, ng, wq, kv, wo, l, S, tm):
    T, D = x.shape
    M = kv.shape[1]
    per_b = S // tm
    return pl.pallas_call(
        _mem_attn_body,
        grid=(T // tm,),
        in_specs=[pl.BlockSpec((tm, D), lambda i: (i, 0)),
                  pl.BlockSpec((None, 1, D), lambda i: (l * 8 + 4, 0, 0)),
                  _mat_spec(D, D, l),
                  pl.BlockSpec((None, M, 2 * D), lambda i: (i // per_b, 0, 0)),
                  _mat_spec(D, D, l),
                  pl.BlockSpec((None, 1, D), lambda i: (l * 8 + 5, 0, 0))],
        out_specs=pl.BlockSpec((tm, D), lambda i: (i, 0)),
        out_shape=jax.ShapeDtypeStruct((T, D), F32),
        compiler_params=_cparams(1),
        name="mem_attn",
    )(x, ng, wq, kv, wo, ng)


def _prepare_params(w_in, mla_w_uq, mla_w_ukv, gla_w_alpha, gla_norm, rwkv_mu, rwkv_w_decay,
                    rwkv_w_iclr, rwkv_w_gate, w_branch):
    L, D = w_in.shape[0], w_in.shape[1]
    zc = lambda n: jnp.zeros((L, D, n), F32)
    o = 0
    cuts = {}
    for name, n in (("c_q", 256), ("c_kv", 128), ("k_rope", 32), ("gla_q", 256), ("gla_k", 256),
                    ("gla_v", 512), ("gla_g", 512), ("gla_a", 16), ("rwkv", 1824), ("gates", 3072)):
        cuts[name] = w_in[:, :, o:o + n]
        o += n
    kr = cuts["k_rope"]
    kr_b = jnp.concatenate([-kr[..., 16:], kr[..., :16]], axis=-1)
    wa = jnp.concatenate([cuts["c_q"], cuts["c_kv"], zc(64), kr, zc(32), zc(64), kr_b, zc(32)], -1)
    wb = jnp.concatenate([cuts["gla_v"], cuts["gla_g"], cuts["gla_q"], cuts["gla_k"],
                          cuts["gla_a"], zc(112)], -1)
    wc = jnp.concatenate([cuts["rwkv"], zc(RWKV_MISC - 288)], -1)
    wd = cuts["gates"]

    wuq = mla_w_uq.reshape(L, MLA_Q_RANK, MLA_HEADS, MLA_NOPE + MLA_ROPE)
    nope, rope = wuq[..., :MLA_NOPE], wuq[..., MLA_NOPE:]
    zq = lambda n: jnp.zeros((L, MLA_Q_RANK, MLA_HEADS, n), F32)
    wqa = jnp.concatenate([nope, rope, zq(32)], -1).reshape(L, MLA_Q_RANK, MLA_W)
    rope_b = jnp.concatenate([-rope[..., 16:], rope[..., :16]], -1)
    wqb = jnp.concatenate([zq(64), rope_b, zq(32)], -1).reshape(L, MLA_Q_RANK, MLA_W)
    wukv = mla_w_ukv.reshape(L, MLA_KV_RANK, MLA_HEADS, 128)
    zk = jnp.zeros((L, MLA_KV_RANK, MLA_HEADS, 64), F32)
    wk = jnp.concatenate([wukv[..., :64], zk], -1).reshape(L, MLA_KV_RANK, MLA_W)
    wv = jnp.concatenate([wukv[..., 64:], zk], -1).reshape(L, MLA_KV_RANK, MLA_W)

    bra = w_branch[:, :512].reshape(L, MLA_HEADS, 64, D)
    bra = jnp.concatenate([bra, jnp.zeros_like(bra)], axis=2).reshape(L, MLA_W, D)
    brb = w_branch[:, 512:512 + GLA_W]
    brc = w_branch[:, 512 + GLA_W:]

    walpha = jnp.concatenate(
        [gla_w_alpha, jnp.zeros((L, 128 - GLA_GATE_RANK, GLA_QK), F32)], axis=1)
    gn = jnp.tile(gla_norm, (1, GLA_HEADS))[:, None, :]
    mu = jnp.concatenate([rwkv_mu, jnp.zeros((L, RWKV_MISC - 288), F32)], -1)[:, None, :]
    zr = lambda n: jnp.zeros((L, n, RWKV_W), F32)
    wdec = jnp.concatenate([rwkv_w_decay, zr(64)], axis=1)
    wiclr = jnp.concatenate([zr(64), rwkv_w_iclr], axis=1)
    wgate = jnp.concatenate([rwkv_w_gate, zr(256 - RWKV_GATE_RANK)], axis=1)
    bf = lambda w: w.astype(BF16)
    return dict(wa=bf(wa), wb=bf(wb), wc=bf(wc), wd=bf(wd), wqa=bf(wqa), wqb=bf(wqb), wk=bf(wk),
                wv=bf(wv), bra=bf(bra), brb=bf(brb), brc=bf(brc), walpha=walpha, gn=gn, mu=mu,
                wdec=wdec, wiclr=wiclr, wgate=bf(wgate))


def kernel(x, mem, positions, norm_g, w_ffn_in, w_ffn_out, w_in, mla_q_norm, mla_w_uq, mla_kv_norm, mla_w_ukv, gla_w_alpha, gla_b_alpha, gla_norm, rwkv_mu, rwkv_w0, rwkv_w_decay, rwkv_a0, rwkv_w_iclr, rwkv_w_gate, rwkv_k_k, rwkv_k_a, rwkv_r_k, rwkv_ln_w, rwkv_ln_b, w_branch, w_out, mem_norm, mem_wq, mem_wkv, mem_wo):
    B, S, D = x.shape
    L = norm_g.shape[0]
    T = B * S
    tm = min(512, S)
    tq = min(512, S)
    ts = min(256, S)
    tf = D_FF // 2

    pp = _prepare_params(w_in, mla_w_uq, mla_w_ukv, gla_w_alpha, gla_norm, rwkv_mu, rwkv_w_decay,
                         rwkv_w_iclr, rwkv_w_gate, w_branch)
    bf = lambda w: w.astype(BF16)
    ffn_in, ffn_out = bf(w_ffn_in), bf(w_ffn_out)
    wout, wq, wkv, wo = bf(w_out), bf(mem_wq), bf(mem_wkv), bf(mem_wo)
    ng = norm_g.reshape(L * 8, 1, D)
    row = lambda p: p[:, None, :]
    rw = dict(mu=pp["mu"], w0=row(rwkv_w0), wdec=pp["wdec"], a0=row(rwkv_a0), wiclr=pp["wiclr"],
              wgate=pp["wgate"], k_k=row(rwkv_k_k), k_a=row(rwkv_k_a), r_k=row(rwkv_r_k),
              ln_w=row(rwkv_ln_w), ln_b=row(rwkv_ln_b))

    inv_freq = ROPE_THETA ** (-jnp.arange(0, MLA_ROPE, 2, dtype=F32) / MLA_ROPE)
    invf_lane = jnp.concatenate(
        [jnp.zeros((MLA_NOPE,), F32), inv_freq, inv_freq, jnp.zeros((32,), F32)])[None, :]
    c0, s0 = _rope_tables(positions.astype(F32).reshape(T, 1), invf_lane, tm)

    x = x.reshape(T, D)
    for l in range(L):
        x = _ffn(x, ng, ffn_in, ffn_out, l, 0, tm, tf)
        za = _normproj(x, ng, l * 8 + 2, pp["wa"], l, tm, "proj_mla")
        zb = _normproj(x, ng, l * 8 + 2, pp["wb"], l, tm, "proj_gla")
        zc = _normproj(x, ng, l * 8 + 2, pp["wc"], l, tm, "proj_rwkv")
        zd = _normproj(x, ng, l * 8 + 2, pp["wd"], l, tm, "proj_gates")
        q, k, v = _mla_prep(za, c0, s0, row(mla_q_norm), row(mla_kv_norm), pp["wqa"], pp["wqb"],
                            pp["wk"], pp["wv"], l, tm)
        o_a = _mla_attn(q, k, v, B, S, tq)
        o_b = _gla(zb, pp["walpha"], row(gla_b_alpha), pp["gn"], l, B, S, ts)
        o_c = _rwkv(zc, rw, l, B, S, ts)
        x = _merge(x, o_a, o_b, o_c, zd, pp["bra"], pp["brb"], pp["brc"], wout, ng, l, tm)
        kv = _mem_kv(mem, row(mem_norm), wkv, l)
        x = _mem_attn(x, ng, wq, kv, wo, l, S, tm)
        x = _ffn(x, ng, ffn_in, ffn_out, l, 1, tm, tf)
    return x.reshape(B, S, D)
```

```python
import functools

import jax
import jax.numpy as jnp
from jax import lax
from jax.experimental import pallas as pl
from jax.experimental.pallas import tpu as pltpu

F32 = jnp.float32
BF16 = jnp.bfloat16
HI = lax.Precision.HIGHEST

D_MODEL = 1024
D_FF = 2816
NORM_EPS = 1e-6
MLA_HEADS = 8
MLA_NOPE = 64
MLA_ROPE = 32
MLA_Q_RANK = 256
MLA_KV_RANK = 128
ROPE_THETA = 10000.0
MLA_SLOT = 128
MLA_W = MLA_HEADS * MLA_SLOT
MLA_HPS = 2
MLA_ONE = MLA_NOPE
GLA_HEADS = 4
GLA_DK = 64
GLA_DV = 128
GLA_GATE_RANK = 16
GLA_GATE_NORM = 16.0
GLA_QK = GLA_HEADS * GLA_DK
GLA_W = GLA_HEADS * GLA_DV
RWKV_HEADS = 8
RWKV_N = 64
RWKV_DECAY_RANK = 64
RWKV_ICLR_RANK = 64
RWKV_GATE_RANK = 160
RWKV_GN_EPS = 64e-5
RWKV_W = RWKV_HEADS * RWKV_N
RWKV_MISC = 384
RWKV_COLS_PAD = 3 * RWKV_W + RWKV_MISC
RWKV_QUAD = 4 * RWKV_N
RWKV_NQ = RWKV_W // RWKV_QUAD
MEM_HEADS = 4
MEM_HD = D_MODEL // MEM_HEADS
N_BRANCH = 3
CHUNK = 64

VMEM_LIMIT_BYTES = 56 * 1024 * 1024


def _cparams(n_axes, flags=None):
    return pltpu.CompilerParams(
        dimension_semantics=("arbitrary",) * n_axes,
        vmem_limit_bytes=VMEM_LIMIT_BYTES,
        flags=flags,
    )


def _dot(a, b, precision=None):
    return jnp.dot(a, b, preferred_element_type=F32, precision=precision)


def _dot_nt(a, b, precision=None):
    return lax.dot_general(a, b, (((1,), (1,)), ((), ())),
                           preferred_element_type=F32, precision=precision)


def _dot_tn(a, b, precision=None):
    return lax.dot_general(a, b, (((0,), (0,)), ((), ())),
                           preferred_element_type=F32, precision=precision)


def _rms(x, g, eps=NORM_EPS):
    return x * lax.rsqrt(jnp.mean(x * x, axis=-1, keepdims=True) + eps) * g


def _sigmoid(x):
    return 1.0 / (1.0 + jnp.exp(-x))


def _softplus(x):
    return jnp.maximum(x, 0.0) + jnp.log(1.0 + jnp.exp(-jnp.abs(x)))


def _split2(x):
    hi = x.astype(BF16)
    return hi, (x - hi.astype(F32)).astype(BF16)


def _dot_split(x, w_hi, w_lo):
    hi, lo = _split2(x)
    m = x.shape[0]
    top = _dot(jnp.concatenate([hi, lo], axis=0), w_hi)
    return top[:m] + top[m:] + _dot(hi, w_lo)


def _dot_01(m01_bf16, x):
    p1 = x.astype(BF16)
    r1 = x - p1.astype(F32)
    p2 = r1.astype(BF16)
    p3 = (r1 - p2.astype(F32)).astype(BF16)
    n = x.shape[1]
    out = _dot(m01_bf16, jnp.concatenate([p1, p2, p3], axis=1))
    return out[:, :n] + out[:, n:2 * n] + out[:, 2 * n:]


def _iota(shape, dim):
    return lax.broadcasted_iota(jnp.int32, shape, dim)


def _segsum(x, e_bf16):
    hi = x.astype(BF16)
    lo = (x - hi.astype(F32)).astype(BF16)
    return _dot(hi, e_bf16) + _dot(lo, e_bf16)


def _vec_spec(n, l):
    return pl.BlockSpec((None, 1, n), lambda *_: (l, 0, 0))


def _mat_spec(r, c, l):
    return pl.BlockSpec((None, r, c), lambda *_: (l, 0, 0))


def _ffn_body(x_ref, gpre_ref, wg_ref, wu_ref, wo_ref, gpost_ref, o_ref, h_ref, acc_ref):
    j = pl.program_id(1)

    @pl.when(j == 0)
    def _():
        h_ref[...] = _rms(x_ref[...], gpre_ref[...]).astype(BF16)
        acc_ref[...] = jnp.zeros_like(acc_ref)

    h = h_ref[...]
    g = _dot(h, wg_ref[...])
    u = _dot(h, wu_ref[...])
    act = (g * _sigmoid(g) * u).astype(BF16)
    acc_ref[...] += _dot(act, wo_ref[...])

    @pl.when(j == pl.num_programs(1) - 1)
    def _():
        o_ref[...] = x_ref[...] + 0.5 * _rms(acc_ref[...], gpost_ref[...])


def _ffn(x, ng, w_in, w_out, l, k, tm, tf):
    T, D = x.shape
    ff = w_out.shape[2]
    nf = ff // tf
    g_pre, g_post = l * 8 + 6 * k, l * 8 + 6 * k + 1
    return pl.pallas_call(
        _ffn_body,
        grid=(T // tm, nf),
        in_specs=[
            pl.BlockSpec((tm, D), lambda i, j: (i, 0)),
            pl.BlockSpec((None, 1, D), lambda i, j: (g_pre, 0, 0)),
            pl.BlockSpec((None, None, D, tf), lambda i, j: (l, k, 0, j)),
            pl.BlockSpec((None, None, D, tf), lambda i, j: (l, k, 0, j + nf)),
            pl.BlockSpec((None, None, tf, D), lambda i, j: (l, k, j, 0)),
            pl.BlockSpec((None, 1, D), lambda i, j: (g_post, 0, 0)),
        ],
        out_specs=pl.BlockSpec((tm, D), lambda i, j: (i, 0)),
        out_shape=jax.ShapeDtypeStruct((T, D), F32),
        scratch_shapes=[pltpu.VMEM((tm, D), BF16), pltpu.VMEM((tm, D), F32)],
        compiler_params=_cparams(2),
        name="ffn",
    )(x, ng, w_in, w_in, w_out, ng)


def _normproj_body(x_ref, g_ref, w_ref, o_ref):
    h = _rms(x_ref[...], g_ref[...]).astype(BF16)
    o_ref[...] = _dot(h, w_ref[...]).astype(o_ref.dtype)


def _normproj(x, ng, g_idx, w, l, tm, name):
    T, D = x.shape
    n = w.shape[2]
    return pl.pallas_call(
        _normproj_body,
        grid=(T // tm,),
        in_specs=[
            pl.BlockSpec((tm, D), lambda i: (i, 0)),
            pl.BlockSpec((None, 1, D), lambda i: (g_idx, 0, 0)),
            _mat_spec(D, n, l),
        ],
        out_specs=pl.BlockSpec((tm, n), lambda i: (i, 0)),
        out_shape=jax.ShapeDtypeStruct((T, n), F32),
        compiler_params=_cparams(1),
        name=name,
    )(x, ng, w)


def _rope_table_body(pos_ref, invf_ref, c_ref, s_ref):
    ang = pos_ref[...] * invf_ref[...]
    lane = _iota(ang.shape, 1)
    rot = (lane >= MLA_NOPE) & (lane < MLA_NOPE + MLA_ROPE)
    c_ref[...] = jnp.where(lane < MLA_NOPE, 1.0, jnp.where(rot, jnp.cos(ang), 0.0))
    s_ref[...] = jnp.where(rot, jnp.sin(ang), 0.0)


def _rope_tables(pos_f32, invf_lane, tm):
    T = pos_f32.shape[0]
    return pl.pallas_call(
        _rope_table_body,
        grid=(T // tm,),
        in_specs=[pl.BlockSpec((tm, 1), lambda i: (i, 0)),
                  pl.BlockSpec((1, MLA_SLOT), lambda i: (0, 0))],
        out_specs=[pl.BlockSpec((tm, MLA_SLOT), lambda i: (i, 0))] * 2,
        out_shape=[jax.ShapeDtypeStruct((T, MLA_SLOT), F32)] * 2,
        compiler_params=_cparams(1),
        name="rope_tables",
    )(pos_f32, invf_lane)


def _mla_prep_body(za_ref, c0_ref, s0_ref, qn_ref, kvn_ref, wqa_ref, wqb_ref, wk_ref, wv_ref,
                   q_ref, k_ref, vt_ref):
    za = za_ref[...]
    cq = za[:, :MLA_Q_RANK]
    ckv = za[:, MLA_Q_RANK:MLA_Q_RANK + MLA_KV_RANK]
    kra = za[:, 384:512]
    krb = za[:, 512:640]
    cqn = _rms(cq, qn_ref[...]).astype(BF16)
    ckvn = _rms(ckv, kvn_ref[...]).astype(BF16)
    c0 = c0_ref[...]
    s0 = s0_ref[...]
    c8 = jnp.concatenate([c0] * MLA_HEADS, axis=1)
    s8 = jnp.concatenate([s0] * MLA_HEADS, axis=1)
    scale = (MLA_NOPE + MLA_ROPE) ** -0.5
    q = (_dot(cqn, wqa_ref[...]) * c8 + _dot(cqn, wqb_ref[...]) * s8) * scale
    q_ref[...] = q.astype(BF16)
    krot = kra * c0 + krb * s0
    k = _dot(ckvn, wk_ref[...]) + jnp.concatenate([krot] * MLA_HEADS, axis=1)
    k_ref[...] = k.astype(BF16)
    v = _dot(ckvn, wv_ref[...])
    one_lane = (_iota(v.shape, 1) & (MLA_SLOT - 1)) == MLA_ONE
    vt_ref[...] = jnp.where(one_lane, 1.0, v).T.astype(BF16)


def _mla_prep(za, c0, s0, qn, kvn, wqa, wqb, wk, wv, l, tk):
    T = za.shape[0]
    tok = lambda n: pl.BlockSpec((tk, n), lambda i: (i, 0))
    return pl.pallas_call(
        _mla_prep_body,
        grid=(T // tk,),
        in_specs=[tok(za.shape[1]), tok(MLA_SLOT), tok(MLA_SLOT),
                  _vec_spec(MLA_Q_RANK, l), _vec_spec(MLA_KV_RANK, l),
                  _mat_spec(MLA_Q_RANK, MLA_W, l), _mat_spec(MLA_Q_RANK, MLA_W, l),
                  _mat_spec(MLA_KV_RANK, MLA_W, l), _mat_spec(MLA_KV_RANK, MLA_W, l)],
        out_specs=[tok(MLA_W), tok(MLA_W),
                   pl.BlockSpec((None, MLA_W, tk), lambda i: (i, 0, 0))],
        out_shape=[jax.ShapeDtypeStruct((T, MLA_W), BF16)] * 2
        + [jax.ShapeDtypeStruct((T // tk, MLA_W, tk), BF16)],
        compiler_params=_cparams(1),
        name="mla_prep",
    )(za, c0, s0, qn, kvn, wqa, wqb, wk, wv)


def _mla_attn_body(q_ref, k_ref, vt_ref, o_ref, s_ref, m_ref, acc_ref, *, tq, tk):
    qi = pl.program_id(2)
    m_ref[...] = jnp.full_like(m_ref, -1e30)
    acc_ref[...] = jnp.zeros_like(acc_ref)
    heads = [slice(hh * MLA_SLOT, (hh + 1) * MLA_SLOT) for hh in range(MLA_HPS)]

    def produce(slot, ki):
        r0 = pl.multiple_of(ki * tk, tk)
        for hh, sl in enumerate(heads):
            s_ref[slot, hh] = _dot_nt(k_ref[pl.ds(r0, tk), sl], q_ref[:, sl])

    def consume(slot, ki, diag):
        for hh, sl in enumerate(heads):
            st = s_ref[slot, hh]
            if diag is not None:
                key = _iota(st.shape, 0) + diag * tk
                st = jnp.where(key <= _iota(st.shape, 1), st, -1e30)
            m_old = m_ref[hh]
            m_new = jnp.maximum(m_old, jnp.max(st, axis=0, keepdims=True))
            p = jnp.exp(st - m_new).astype(BF16)
            acc_ref[hh] = jnp.exp(m_old - m_new) * acc_ref[hh] + _dot(vt_ref[ki, sl, :], p)
            m_ref[hh] = m_new

    def body(j, carry):
        produce(1, 2 * j + 1)
        consume(0, 2 * j, None)
        produce(0, 2 * j + 2)
        consume(1, 2 * j + 1, None)
        return carry

    produce(0, 0)
    lax.fori_loop(0, qi, body, 0)
    produce(1, 2 * qi + 1)
    consume(0, 2 * qi, 0)
    consume(1, 2 * qi + 1, 1)
    for hh in range(MLA_HPS):
        acc = acc_ref[hh]
        o_t = acc / acc[MLA_ONE:MLA_ONE + 1, :]
        o_ref[:, hh * MLA_SLOT:(hh + 1) * MLA_SLOT] = o_t.T.astype(o_ref.dtype)


def _mla_attn(q, k, vt, B, S, tq, tk):
    assert tq == 2 * tk
    T = q.shape[0]
    nq, nk = S // tq, S // tk
    w = MLA_HPS * MLA_SLOT
    return pl.pallas_call(
        functools.partial(_mla_attn_body, tq=tq, tk=tk),
        grid=(B, MLA_HEADS // MLA_HPS, nq),
        in_specs=[
            pl.BlockSpec((tq, w), lambda b, h, i: (b * nq + i, h)),
            pl.BlockSpec((S, w), lambda b, h, i: (b, h)),
            pl.BlockSpec((nk, w, tk), lambda b, h, i: (b, h, 0)),
        ],
        out_specs=pl.BlockSpec((tq, w), lambda b, h, i: (b * nq + i, h)),
        out_shape=jax.ShapeDtypeStruct((T, MLA_W), BF16),
        scratch_shapes=[pltpu.VMEM((2, MLA_HPS, tk, tq), F32),
                        pltpu.VMEM((MLA_HPS, 1, tq), F32),
                        pltpu.VMEM((MLA_HPS, MLA_SLOT, tq), F32)],
        compiler_params=_cparams(3),
        name="mla_attn",
    )(q, k, vt)


def _gla_body(v_ref, g_ref, q_ref, k_ref, a_ref, walpha_ref, balpha_ref, gn_ref, o_ref, ht_ref,
              *, n_chunks, n_batch):
    @pl.when(pl.program_id(0) == 0)
    def _():
        ht_ref[...] = jnp.zeros_like(ht_ref)

    C = CHUNK
    walpha_hi, walpha_lo = walpha_ref[0], walpha_ref[1]
    balpha = balpha_ref[...]
    gn = gn_ref[...]
    tri = (_iota((C, C), 1) <= _iota((C, C), 0)).astype(BF16)
    st_r, st_c = _iota((GLA_HEADS * C, GLA_QK), 0), _iota((GLA_HEADS * C, GLA_QK), 1)
    head_mask = (st_r >> 6) == (st_c >> 6)
    cz_r, cz_c = _iota((GLA_HEADS * C, C), 0), _iota((GLA_HEADS * C, C), 1)
    causal = cz_c <= (cz_r & (C - 1))
    ht_r, ht_c = _iota((GLA_W, GLA_QK), 0), _iota((GLA_W, GLA_QK), 1)
    state_mask = (ht_r >> 7) == (ht_c >> 6)

    def chunk(c, carry):
        rows = pl.ds(pl.multiple_of(c * C, C), C)
        bs = range(n_batch)
        xs = [_dot_split(a_ref[b, rows, :], walpha_hi, walpha_lo) + balpha for b in bs]
        cum = [_dot_01(tri, -_softplus(-x) * (1.0 / GLA_GATE_NORM)) for x in xs]
        st = []
        for b in bs:
            cb = cum[b]
            b_last = cb[C - 1:C, :]
            b_mid = cb[C // 2 - 1:C // 2, :]
            q = q_ref[b, rows, :] * GLA_DK ** -0.5
            k = k_ref[b, rows, :]
            qm = q * jnp.exp(cb - b_mid)
            st.append(dict(
                v=v_ref[b, rows, :].astype(BF16),
                qe=(q * jnp.exp(cb)).astype(BF16),
                km=(k * jnp.exp(b_mid - cb)).astype(BF16),
                ke=(k * jnp.exp(b_last - cb)).astype(BF16),
                gam=jnp.exp(b_last),
                qst=jnp.where(head_mask, jnp.concatenate([qm] * GLA_HEADS, axis=0),
                              0.0).astype(BF16)))
        scs = [jnp.where(causal, _dot_nt(s["qst"], s["km"]), 0.0).astype(BF16) for s in st]
        osts = [_dot(sc, s["v"]) for sc, s in zip(scs, st)]
        for b in bs:
            s, ost = st[b], osts[b]
            o_intra = jnp.concatenate(
                [ost[h * C:(h + 1) * C, h * GLA_DV:(h + 1) * GLA_DV] for h in range(GLA_HEADS)],
                axis=1)
            ht = ht_ref[b]
            o = o_intra + _dot_nt(s["qe"], ht.astype(BF16))
            ht_ref[b] = ht * s["gam"] + jnp.where(state_mask, _dot_tn(s["v"], s["ke"]), 0.0)
            outs = []
            for h in range(GLA_HEADS):
                oh = o[:, h * GLA_DV:(h + 1) * GLA_DV]
                outs.append(oh * lax.rsqrt(jnp.mean(oh * oh, axis=-1, keepdims=True) + NORM_EPS))
            on = jnp.concatenate(outs, axis=1) * gn
            g = g_ref[b, rows, :]
            o_ref[b, rows, :] = (on * (g * _sigmoid(g))).astype(o_ref.dtype)
        return carry

    lax.fori_loop(0, n_chunks, chunk, 0)


def _gla(zb, walpha, balpha, gn, l, B, S, ts):
    ns = S // ts
    blk = lambda n, j: pl.BlockSpec((B, ts, n), lambda s: (0, s, j))
    zb = zb.reshape(B, S, zb.shape[1])
    out = pl.pallas_call(
        functools.partial(_gla_body, n_chunks=ts // CHUNK, n_batch=B),
        grid=(ns,),
        in_specs=[
            blk(GLA_W, 0),
            blk(GLA_W, 1),
            blk(GLA_QK, 4),
            blk(GLA_QK, 5),
            blk(128, 12),
            pl.BlockSpec((None, 2, 128, GLA_QK), lambda s: (l, 0, 0, 0)),
            _vec_spec(GLA_QK, l), _vec_spec(GLA_W, l),
        ],
        out_specs=blk(GLA_W, 0),
        out_shape=jax.ShapeDtypeStruct((B, S, GLA_W), BF16),
        scratch_shapes=[pltpu.VMEM((B, GLA_W, GLA_QK), F32)],
        compiler_params=_cparams(1),
        name="gla",
    )(zb, zb, zb, zb, zb, walpha, balpha, gn)
    return out.reshape(B * S, GLA_W)


def _rwkv_body(z_ref, mu_ref, w0_ref, wdec_ref, a0_ref, wiclr_ref, wgate_ref, kk_ref, ka_ref,
               rk_ref, lnw_ref, lnb_ref, o_ref, h_ref, zlast_ref, *, n_chunks, n_batch):
    @pl.when(pl.program_id(0) == 0)
    def _():
        h_ref[...] = jnp.zeros_like(h_ref)
        zlast_ref[...] = jnp.zeros_like(zlast_ref)

    C, W, Q = CHUNK, RWKV_W, RWKV_QUAD
    mu = mu_ref[...]
    w0, a0 = w0_ref[...], a0_ref[...]
    wdec_hi, wdec_lo = wdec_ref[0], wdec_ref[1]
    wiclr_hi, wiclr_lo = wiclr_ref[0], wiclr_ref[1]
    wgate = wgate_ref[...]
    k_k, k_a, r_k = kk_ref[...], ka_ref[...], rk_ref[...]
    ln_w, ln_b = lnw_ref[...], lnb_ref[...]

    tri = (_iota((C, C), 1) <= _iota((C, C), 0)).astype(BF16)
    sq_r, sq_c = _iota((Q, Q), 0), _iota((Q, Q), 1)
    head_blk = (sq_r >> 6) == (sq_c >> 6)
    eye_q = sq_r == sq_c
    e_seg = head_blk.astype(BF16)
    wd_t, wd_s = _iota((C, Q), 0), _iota((C, Q), 1) & (C - 1)
    strict = wd_s < wd_t
    incl = wd_s <= wd_t
    eye_wide = (wd_s == wd_t).astype(F32)
    row0 = _iota((C, RWKV_COLS_PAD), 0) == 0

    def bd(x):
        return jnp.concatenate([x.astype(BF16)] * 4, axis=0) * e_seg

    def quad(x, i):
        return x[:, i * Q:(i + 1) * Q]

    def segsum(x):
        return jnp.concatenate([_segsum(quad(x, i), e_seg) for i in range(RWKV_NQ)], axis=1)

    def chunk_one(b, rows):
        z = z_ref[b, rows, :]
        zp = jnp.where(row0, zlast_ref[b, 0:1, :], pltpu.roll(z, 1, 0))
        zlast_ref[b, 0:1, :] = z[C - 1:C, :]
        z = z + mu * (zp - z)
        r = z[:, 0:W]
        k = z[:, W:2 * W]
        v = z[:, 2 * W:3 * W]
        m0 = z[:, 3 * W:3 * W + 128]
        m12 = z[:, 3 * W + 128:3 * W + 384]
        w_log = -_softplus(-(w0 + _dot_split(jnp.tanh(m0), wdec_hi, wdec_lo))) - 0.5
        lw = -jnp.exp(w_log)
        a = _sigmoid(a0 + _dot_split(m0, wiclr_hi, wiclr_lo))
        g = _dot(_sigmoid(m12).astype(BF16), wgate)
        kk = k * k_k
        kkn = kk * lax.rsqrt(jnp.maximum(segsum(kk * kk), 1e-24))
        k2 = k * (1.0 + (a - 1.0) * k_a)
        beta = kkn * a
        cs = _dot_01(tri, lw)
        c_last = cs[C - 1:C, :]
        dec_in = jnp.exp(-cs)
        dec_out = jnp.exp(c_last - cs)
        kt = kkn * jnp.exp(cs - lw)
        rt = r * jnp.exp(cs)
        bh = beta * dec_in
        kh = k2 * dec_in
        kbar = k2 * dec_out
        bbar = beta * dec_out
        gam = jnp.exp(c_last)
        chains = [dict(b=b, i=i, kt=quad(kt, i), rt=quad(rt, i), v=quad(v, i), bh=quad(bh, i),
                       kh=quad(kh, i), kbar=quad(kbar, i), bbar=quad(bbar, i), gam=quad(gam, i))
                  for i in range(RWKV_NQ)]
        return dict(r=r, k2=k2, v=v, g=g), chains

    def epilogue(b, rows, tok, y):
        mean = segsum(y) * (1.0 / RWKV_N)
        d = y - mean
        var = segsum(d * d) * (1.0 / RWKV_N)
        yn = d * lax.rsqrt(var + RWKV_GN_EPS) * ln_w + ln_b
        bonus = segsum(tok["r"] * tok["k2"] * r_k) * tok["v"]
        o_ref[b, rows, :] = ((yn + bonus) * tok["g"]).astype(o_ref.dtype)

    def chunk(c, carry):
        rows = pl.ds(pl.multiple_of(c * C, C), C)
        toks, chains = [], []
        for b in range(n_batch):
            tok, ch = chunk_one(b, rows)
            toks.append(tok)
            chains += ch
        for ch in chains:
            lhs = jnp.concatenate([ch["kt"], ch["rt"]], axis=0).astype(BF16)
            rhs = jnp.concatenate([bd(ch["bh"]), bd(ch["kh"])], axis=0)
            sc = _dot_nt(lhs, rhs)
            ch["a_kk"] = jnp.where(strict, sc[:C, Q:], 0.0)
            ch["a_rb"] = jnp.where(incl, sc[C:, :Q], 0.0)
            ch["a_rk"] = jnp.where(incl, sc[C:, Q:], 0.0)
            ch["n"] = -jnp.where(strict, sc[:C, :Q], 0.0)
            ch["t"] = eye_wide + ch["n"]
        for ch in chains:
            ch["p"] = _dot(ch["n"].astype(BF16), bd(ch["n"]))
            ch["bd_v"] = bd(ch["v"])
            ch["akk_v"] = _dot(ch["a_kk"].astype(BF16), ch["bd_v"])
        for _ in range(4):
            for ch in chains:
                tp = _dot(jnp.concatenate([ch["t"], ch["p"]], axis=0).astype(BF16), bd(ch["p"]))
                ch["t"] = ch["t"] + tp[:C]
                ch["p"] = tp[C:]
        for ch in chains:
            ch["t"] = ch["t"] + _dot(ch["t"].astype(BF16), bd(ch["p"]))
        for ch in chains:
            w12 = _dot(ch["t"].astype(BF16),
                       jnp.concatenate([bd(ch["akk_v"]), bd(ch["kt"])], axis=1))
            ch["w1"], ch["w2"] = w12[:, :Q], w12[:, Q:]
        ys = []
        for ch in chains:
            w1, w2, bbar = ch["w1"], ch["w2"], ch["bbar"]
            y0 = _dot(jnp.concatenate([ch["a_rk"], ch["a_rb"]], axis=1).astype(BF16),
                      jnp.concatenate([ch["bd_v"], bd(-w1)], axis=0))
            rp = ch["rt"] - _dot(ch["a_rb"].astype(BF16), bd(w2))
            p_bd = jnp.where(eye_q, ch["gam"], 0.0) - jnp.where(
                head_blk, _dot_tn(bbar.astype(BF16), w2.astype(BF16)), 0.0)
            q_bd = jnp.where(
                head_blk,
                _dot_tn(jnp.concatenate([ch["kbar"], -bbar], axis=0).astype(BF16),
                        jnp.concatenate([ch["v"], w1], axis=0).astype(BF16)), 0.0)
            hb = h_ref[ch["b"], ch["i"]].astype(BF16)
            ys.append(_dot(rp.astype(BF16), hb) + y0)
            h_ref[ch["b"], ch["i"]] = _dot(p_bd.astype(BF16), hb) + q_bd
        for b in range(n_batch):
            y = jnp.concatenate(ys[b * RWKV_NQ:(b + 1) * RWKV_NQ], axis=1)
            epilogue(b, rows, toks[b], y)
        return carry

    lax.fori_loop(0, n_chunks, chunk, 0)


def _rwkv(zc, p, l, B, S, ts):
    ns = S // ts
    W, Q = RWKV_W, RWKV_QUAD
    split_w = pl.BlockSpec((None, 2, 128, W), lambda *_: (l, 0, 0, 0))
    out = pl.pallas_call(
        functools.partial(_rwkv_body, n_chunks=ts // CHUNK, n_batch=B),
        grid=(ns,),
        in_specs=[
            pl.BlockSpec((B, ts, RWKV_COLS_PAD), lambda s: (0, s, 0)),
            _vec_spec(RWKV_COLS_PAD, l), _vec_spec(W, l), split_w,
            _vec_spec(W, l), split_w, _mat_spec(256, W, l),
            _vec_spec(W, l), _vec_spec(W, l), _vec_spec(W, l), _vec_spec(W, l), _vec_spec(W, l),
        ],
        out_specs=pl.BlockSpec((B, ts, W), lambda s: (0, s, 0)),
        out_shape=jax.ShapeDtypeStruct((B, S, W), BF16),
        scratch_shapes=[pltpu.VMEM((B, RWKV_NQ, Q, Q), F32),
                        pltpu.VMEM((B, 8, RWKV_COLS_PAD), F32)],
        compiler_params=_cparams(1),
        name="rwkv",
    )(zc.reshape(B, S, RWKV_COLS_PAD), p["mu"], p["w0"], p["wdec"], p["a0"], p["wiclr"],
      p["wgate"], p["k_k"], p["k_a"], p["r_k"], p["ln_w"], p["ln_b"])
    return out.reshape(B * S, W)


def _merge_body(x_ref, oa_ref, ob_ref, oc_ref, zd_ref, wa_ref, wb_ref, wc_ref, wo_ref, g_ref, o_ref):
    zd = zd_ref[...]
    D = D_MODEL
    merged = (_sigmoid(zd[:, 0:D]) * _dot(oa_ref[...], wa_ref[...])
              + _sigmoid(zd[:, D:2 * D]) * _dot(ob_ref[...], wb_ref[...])
              + _sigmoid(zd[:, 2 * D:3 * D]) * _dot(oc_ref[...], wc_ref[...]))
    y = _dot(merged.astype(BF16), wo_ref[...])
    o_ref[...] = x_ref[...] + _rms(y, g_ref[...])


def _merge(x, oa, ob, oc, zd, wa, wb, wc, wo, ng, l, tm):
    T, D = x.shape
    tok = lambda n: pl.BlockSpec((tm, n), lambda i: (i, 0))
    return pl.pallas_call(
        _merge_body,
        grid=(T // tm,),
        in_specs=[tok(D), tok(MLA_W), tok(GLA_W), tok(RWKV_W), tok(N_BRANCH * D),
                  _mat_spec(MLA_W, D, l), _mat_spec(GLA_W, D, l), _mat_spec(RWKV_W, D, l),
                  _mat_spec(D, D, l), pl.BlockSpec((None, 1, D), lambda i: (l * 8 + 3, 0, 0))],
        out_specs=tok(D),
        out_shape=jax.ShapeDtypeStruct((T, D), F32),
        compiler_params=_cparams(1),
        name="merge",
    )(x, oa, ob, oc, zd, wa, wb, wc, wo, ng)


def _mem_kv_body(mem_ref, g_ref, w_ref, o_ref):
    o_ref[...] = _dot(_rms(mem_ref[...], g_ref[...]).astype(BF16), w_ref[...]).astype(o_ref.dtype)


def _mem_kv(mem, mem_norm, wkv, l):
    B, M, D = mem.shape
    return pl.pallas_call(
        _mem_kv_body,
        grid=(B,),
        in_specs=[pl.BlockSpec((None, M, D), lambda b: (b, 0, 0)), _vec_spec(D, l),
                  _mat_spec(D, 2 * D, l)],
        out_specs=pl.BlockSpec((None, M, 2 * D), lambda b: (b, 0, 0)),
        out_shape=jax.ShapeDtypeStruct((B, M, 2 * D), BF16),
        compiler_params=_cparams(1),
        name="mem_kv",
    )(mem, mem_norm, wkv)


def _mem_attn_body(x_ref, gpre_ref, wq_ref, kv_ref, wo_ref, gpost_ref, o_ref):
    x = x_ref[...]
    D = D_MODEL
    h = _rms(x, gpre_ref[...]).astype(BF16)
    q = (_dot(h, wq_ref[...]) * MEM_HD ** -0.5).astype(BF16)
    kv = kv_ref[...]
    outs = []
    for hh in range(MEM_HEADS):
        sl = slice(hh * MEM_HD, (hh + 1) * MEM_HD)
        s = _dot_nt(q[:, sl], kv[:, sl])
        p = jnp.exp(s - jnp.max(s, axis=-1, keepdims=True))
        o = _dot(p.astype(BF16), kv[:, D + hh * MEM_HD:D + (hh + 1) * MEM_HD])
        outs.append(o / jnp.sum(p, axis=-1, keepdims=True))
    o = jnp.concatenate(outs, axis=1).astype(BF16)
    o_ref[...] = x + _rms(_dot(o, wo_ref[...]), gpost_ref[...])


def _mem_attn(x, ng, wq, kv, wo, l, S, tm):
    T, D = x.shape
    M = kv.shape[1]
    per_b = S // tm
    return pl.pallas_call(
        _mem_attn_body,
        grid=(T // tm,),
        in_specs=[pl.BlockSpec((tm, D), lambda i: (i, 0)),
                  pl.BlockSpec((None, 1, D), lambda i: (l * 8 + 4, 0, 0)),
                  _mat_spec(D, D, l),
                  pl.BlockSpec((None, M, 2 * D), lambda i: (i // per_b, 0, 0)),
                  _mat_spec(D, D, l),
                  pl.BlockSpec((None, 1, D), lambda i: (l * 8 + 5, 0, 0))],
        out_specs=pl.BlockSpec((tm, D), lambda i: (i, 0)),
        out_shape=jax.ShapeDtypeStruct((T, D), F32),
        compiler_params=_cparams(1),
        name="mem_attn",
    )(x, ng, wq, kv, wo, ng)


def _prepare_params(w_in, mla_w_uq, mla_w_ukv, gla_w_alpha, gla_norm, rwkv_mu, rwkv_w_decay,
                    rwkv_w_iclr, rwkv_w_gate, w_branch):
    L, D = w_in.shape[0], w_in.shape[1]
    zc = lambda n: jnp.zeros((L, D, n), F32)
    o = 0
    cuts = {}
    for name, n in (("c_q", 256), ("c_kv", 128), ("k_rope", 32), ("gla_q", 256), ("gla_k", 256),
                    ("gla_v", 512), ("gla_g", 512), ("gla_a", 16), ("rwkv", 1824), ("gates", 3072)):
        cuts[name] = w_in[:, :, o:o + n]
        o += n
    kr = cuts["k_rope"]
    kr_b = jnp.concatenate([-kr[..., 16:], kr[..., :16]], axis=-1)
    wa = jnp.concatenate([cuts["c_q"], cuts["c_kv"], zc(64), kr, zc(32), zc(64), kr_b, zc(32)], -1)
    wb = jnp.concatenate([cuts["gla_v"], cuts["gla_g"], cuts["gla_q"], cuts["gla_k"],
                          cuts["gla_a"], zc(112)], -1)
    wc = jnp.concatenate([cuts["rwkv"], zc(RWKV_MISC - 288)], -1)
    wd = cuts["gates"]

    wuq = mla_w_uq.reshape(L, MLA_Q_RANK, MLA_HEADS, MLA_NOPE + MLA_ROPE)
    nope, rope = wuq[..., :MLA_NOPE], wuq[..., MLA_NOPE:]
    zq = lambda n: jnp.zeros((L, MLA_Q_RANK, MLA_HEADS, n), F32)
    wqa = jnp.concatenate([nope, rope, zq(32)], -1).reshape(L, MLA_Q_RANK, MLA_W)
    rope_b = jnp.concatenate([-rope[..., 16:], rope[..., :16]], -1)
    wqb = jnp.concatenate([zq(64), rope_b, zq(32)], -1).reshape(L, MLA_Q_RANK, MLA_W)
    wukv = mla_w_ukv.reshape(L, MLA_KV_RANK, MLA_HEADS, 128)
    zk = jnp.zeros((L, MLA_KV_RANK, MLA_HEADS, 64), F32)
    wk = jnp.concatenate([wukv[..., :64], zk], -1).reshape(L, MLA_KV_RANK, MLA_W)
    wv = jnp.concatenate([wukv[..., 64:], zk], -1).reshape(L, MLA_KV_RANK, MLA_W)

    bra = w_branch[:, :512].reshape(L, MLA_HEADS, 64, D)
    bra = jnp.concatenate([bra, jnp.zeros_like(bra)], axis=2).reshape(L, MLA_W, D)
    brb = w_branch[:, 512:512 + GLA_W]
    brc = w_branch[:, 512 + GLA_W:]

    walpha = jnp.concatenate(
        [gla_w_alpha, jnp.zeros((L, 128 - GLA_GATE_RANK, GLA_QK), F32)], axis=1)
    gn = jnp.tile(gla_norm, (1, GLA_HEADS))[:, None, :]
    mu = jnp.concatenate([rwkv_mu, jnp.zeros((L, RWKV_MISC - 288), F32)], -1)[:, None, :]
    zr = lambda n: jnp.zeros((L, n, RWKV_W), F32)
    def hi_lo(w):
        hi = w.astype(BF16)
        return jnp.stack([hi, (w - hi.astype(F32)).astype(BF16)], axis=1)

    wdec = hi_lo(jnp.concatenate([rwkv_w_decay, zr(64)], axis=1))
    wiclr = hi_lo(jnp.concatenate([zr(64), rwkv_w_iclr], axis=1))
    wgate = jnp.concatenate([rwkv_w_gate, zr(256 - RWKV_GATE_RANK)], axis=1)
    bf = lambda w: w.astype(BF16)
    return dict(wa=bf(wa), wb=bf(wb), wc=bf(wc), wd=bf(wd), wqa=bf(wqa), wqb=bf(wqb), wk=bf(wk),
                wv=bf(wv), bra=bf(bra), brb=bf(brb), brc=bf(brc), walpha=hi_lo(walpha), gn=gn, mu=mu,
                wdec=wdec, wiclr=wiclr, wgate=bf(wgate))


def kernel(x, mem, positions, norm_g, w_ffn_in, w_ffn_out, w_in, mla_q_norm, mla_w_uq, mla_kv_norm, mla_w_ukv, gla_w_alpha, gla_b_alpha, gla_norm, rwkv_mu, rwkv_w0, rwkv_w_decay, rwkv_a0, rwkv_w_iclr, rwkv_w_gate, rwkv_k_k, rwkv_k_a, rwkv_r_k, rwkv_ln_w, rwkv_ln_b, w_branch, w_out, mem_norm, mem_wq, mem_wkv, mem_wo):
    B, S, D = x.shape
    L = norm_g.shape[0]
    T = B * S
    tm = min(512, S)
    tq = min(1024, S)
    tk = tq // 2
    ts = min(256, S)
    tf = D_FF // 2

    pp = _prepare_params(w_in, mla_w_uq, mla_w_ukv, gla_w_alpha, gla_norm, rwkv_mu, rwkv_w_decay,
                         rwkv_w_iclr, rwkv_w_gate, w_branch)
    bf = lambda w: w.astype(BF16)
    ffn_in, ffn_out = bf(w_ffn_in), bf(w_ffn_out)
    wout, wq, wkv, wo = bf(w_out), bf(mem_wq), bf(mem_wkv), bf(mem_wo)
    ng = norm_g.reshape(L * 8, 1, D)
    row = lambda p: p[:, None, :]
    rw = dict(mu=pp["mu"], w0=row(rwkv_w0), wdec=pp["wdec"], a0=row(rwkv_a0), wiclr=pp["wiclr"],
              wgate=pp["wgate"], k_k=row(rwkv_k_k), k_a=row(rwkv_k_a), r_k=row(rwkv_r_k),
              ln_w=row(rwkv_ln_w), ln_b=row(rwkv_ln_b))

    inv_freq = ROPE_THETA ** (-jnp.arange(0, MLA_ROPE, 2, dtype=F32) / MLA_ROPE)
    invf_lane = jnp.concatenate(
        [jnp.zeros((MLA_NOPE,), F32), inv_freq, inv_freq, jnp.zeros((32,), F32)])[None, :]
    c0, s0 = _rope_tables(positions.astype(F32).reshape(T, 1), invf_lane, tm)

    x = x.reshape(T, D)
    for l in range(L):
        x = _ffn(x, ng, ffn_in, ffn_out, l, 0, tm, tf)
        za = _normproj(x, ng, l * 8 + 2, pp["wa"], l, tm, "proj_mla")
        zb = _normproj(x, ng, l * 8 + 2, pp["wb"], l, tm, "proj_gla")
        zc = _normproj(x, ng, l * 8 + 2, pp["wc"], l, tm, "proj_rwkv")
        zd = _normproj(x, ng, l * 8 + 2, pp["wd"], l, tm, "proj_gates")
        q, k, vt = _mla_prep(za, c0, s0, row(mla_q_norm), row(mla_kv_norm), pp["wqa"], pp["wqb"],
                             pp["wk"], pp["wv"], l, tk)
        o_a = _mla_attn(q, k, vt, B, S, tq, tk)
        o_b = _gla(zb, pp["walpha"], row(gla_b_alpha), pp["gn"], l, B, S, ts)
        o_c = _rwkv(zc, rw, l, B, S, ts)
        x = _merge(x, o_a, o_b, o_c, zd, pp["bra"], pp["brb"], pp["brc"], wout, ng, l, tm)
        kv = _mem_kv(mem, row(mem_norm), wkv, l)
        x = _mem_attn(x, ng, wq, kv, wo, l, S, tm)
        x = _ffn(x, ng, ffn_in, ffn_out, l, 1, tm, tf)
    return x.reshape(B, S, D)
```

```python
import functools

import jax
import jax.numpy as jnp
import numpy as np
from jax import lax
from jax.experimental import pallas as pl
from jax.experimental.pallas import tpu as pltpu

F32 = jnp.float32
BF16 = jnp.bfloat16
HI = lax.Precision.HIGHEST

D_MODEL = 1024
D_FF = 2816
NORM_EPS = 1e-6
MLA_HEADS = 8
MLA_NOPE = 64
MLA_ROPE = 32
MLA_Q_RANK = 256
MLA_KV_RANK = 128
ROPE_THETA = 10000.0
MLA_SLOT = 128
MLA_W = MLA_HEADS * MLA_SLOT
MLA_HPS = 2
MLA_ONE = MLA_NOPE
GLA_HEADS = 4
GLA_DK = 64
GLA_DV = 128
GLA_GATE_RANK = 16
GLA_GATE_NORM = 16.0
GLA_QK = GLA_HEADS * GLA_DK
GLA_W = GLA_HEADS * GLA_DV
GLA_LEVELS = 6
GLA_NEG = -1e30
RWKV_HEADS = 8
RWKV_N = 64
RWKV_DECAY_RANK = 64
RWKV_ICLR_RANK = 64
RWKV_GATE_RANK = 160
RWKV_GN_EPS = 64e-5
RWKV_W = RWKV_HEADS * RWKV_N
RWKV_MISC = 384
RWKV_COLS_PAD = 3 * RWKV_W + RWKV_MISC
RWKV_QUAD = 4 * RWKV_N
RWKV_NQ = RWKV_W // RWKV_QUAD
RWKV_CPI = 2
MEM_HEADS = 4
MEM_HD = D_MODEL // MEM_HEADS
N_BRANCH = 3
CHUNK = 64

VMEM_LIMIT_BYTES = 56 * 1024 * 1024


def _cparams(n_axes, flags=None):
    return pltpu.CompilerParams(
        dimension_semantics=("arbitrary",) * n_axes,
        vmem_limit_bytes=VMEM_LIMIT_BYTES,
        flags=flags,
    )


def _dot(a, b, precision=None):
    return jnp.dot(a, b, preferred_element_type=F32, precision=precision)


def _dot_nt(a, b, precision=None):
    return lax.dot_general(a, b, (((1,), (1,)), ((), ())),
                           preferred_element_type=F32, precision=precision)


def _dot_tn(a, b, precision=None):
    return lax.dot_general(a, b, (((0,), (0,)), ((), ())),
                           preferred_element_type=F32, precision=precision)


def _rms(x, g, eps=NORM_EPS):
    return x * lax.rsqrt(jnp.mean(x * x, axis=-1, keepdims=True) + eps) * g


def _sigmoid(x):
    return 1.0 / (1.0 + jnp.exp(-x))


def _softplus(x):
    return jnp.maximum(x, 0.0) + jnp.log(1.0 + jnp.exp(-jnp.abs(x)))


def _split2(x):
    hi = x.astype(BF16)
    return hi, (x - hi.astype(F32)).astype(BF16)


def _dot_split(x, w_hi, w_lo):
    hi, lo = _split2(x)
    m = x.shape[0]
    top = _dot(jnp.concatenate([hi, lo], axis=0), w_hi)
    return top[:m] + top[m:] + _dot(hi, w_lo)


def _dot_01(m01_bf16, x):
    p1 = x.astype(BF16)
    r1 = x - p1.astype(F32)
    p2 = r1.astype(BF16)
    p3 = (r1 - p2.astype(F32)).astype(BF16)
    n = x.shape[1]
    out = _dot(m01_bf16, jnp.concatenate([p1, p2, p3], axis=1))
    return out[:, :n] + out[:, n:2 * n] + out[:, 2 * n:]


def _iota(shape, dim):
    return lax.broadcasted_iota(jnp.int32, shape, dim)


def _segsum(x, e_bf16):
    hi = x.astype(BF16)
    lo = (x - hi.astype(F32)).astype(BF16)
    return _dot(hi, e_bf16) + _dot(lo, e_bf16)


def _vec_spec(n, l):
    return pl.BlockSpec((None, 1, n), lambda *_: (l, 0, 0))


def _mat_spec(r, c, l):
    return pl.BlockSpec((None, r, c), lambda *_: (l, 0, 0))


def _ffn_body(x_ref, gpre_ref, wg_ref, wu_ref, wo_ref, gpost_ref, o_ref, h_ref, acc_ref):
    j = pl.program_id(1)

    @pl.when(j == 0)
    def _():
        h_ref[...] = _rms(x_ref[...], gpre_ref[...]).astype(BF16)
        acc_ref[...] = jnp.zeros_like(acc_ref)

    h = h_ref[...]
    g = _dot(h, wg_ref[...])
    u = _dot(h, wu_ref[...])
    act = (g * _sigmoid(g) * u).astype(BF16)
    acc_ref[...] += _dot(act, wo_ref[...])

    @pl.when(j == pl.num_programs(1) - 1)
    def _():
        o_ref[...] = x_ref[...] + 0.5 * _rms(acc_ref[...], gpost_ref[...])


def _ffn(x, ng, w_in, w_out, l, k, tm, tf):
    T, D = x.shape
    ff = w_out.shape[2]
    nf = ff // tf
    g_pre, g_post = l * 8 + 6 * k, l * 8 + 6 * k + 1
    return pl.pallas_call(
        _ffn_body,
        grid=(T // tm, nf),
        in_specs=[
            pl.BlockSpec((tm, D), lambda i, j: (i, 0)),
            pl.BlockSpec((None, 1, D), lambda i, j: (g_pre, 0, 0)),
            pl.BlockSpec((None, None, D, tf), lambda i, j: (l, k, 0, j)),
            pl.BlockSpec((None, None, D, tf), lambda i, j: (l, k, 0, j + nf)),
            pl.BlockSpec((None, None, tf, D), lambda i, j: (l, k, j, 0)),
            pl.BlockSpec((None, 1, D), lambda i, j: (g_post, 0, 0)),
        ],
        out_specs=pl.BlockSpec((tm, D), lambda i, j: (i, 0)),
        out_shape=jax.ShapeDtypeStruct((T, D), F32),
        scratch_shapes=[pltpu.VMEM((tm, D), BF16), pltpu.VMEM((tm, D), F32)],
        compiler_params=_cparams(2),
        name="ffn",
    )(x, ng, w_in, w_in, w_out, ng)


def _normproj_body(x_ref, g_ref, *refs):
    n = len(refs) // 2
    h = _rms(x_ref[...], g_ref[...]).astype(BF16)
    for w_ref, o_ref in zip(refs[:n], refs[n:]):
        o_ref[...] = _dot(h, w_ref[...]).astype(o_ref.dtype)


def _normproj(x, ng, g_idx, ws, l, tm):
    T, D = x.shape
    widths = [w.shape[2] for w in ws]
    return pl.pallas_call(
        _normproj_body,
        grid=(T // tm,),
        in_specs=[pl.BlockSpec((tm, D), lambda i: (i, 0)),
                  pl.BlockSpec((None, 1, D), lambda i: (g_idx, 0, 0))]
        + [pl.BlockSpec((None, D, n), lambda i: (l, 0, 0), pipeline_mode=pl.Buffered(1))
           for n in widths],
        out_specs=[pl.BlockSpec((tm, n), lambda i: (i, 0)) for n in widths],
        out_shape=[jax.ShapeDtypeStruct((T, n), F32) for n in widths],
        compiler_params=_cparams(1),
        name="mixer_proj",
    )(x, ng, *ws)


def _rope_table_body(pos_ref, invf_ref, c_ref, s_ref):
    ang = pos_ref[...] * invf_ref[...]
    lane = _iota(ang.shape, 1)
    rot = (lane >= MLA_NOPE) & (lane < MLA_NOPE + MLA_ROPE)
    c_ref[...] = jnp.where(lane < MLA_NOPE, 1.0, jnp.where(rot, jnp.cos(ang), 0.0))
    s_ref[...] = jnp.where(rot, jnp.sin(ang), 0.0)


def _rope_tables(pos_f32, invf_lane, tm):
    T = pos_f32.shape[0]
    return pl.pallas_call(
        _rope_table_body,
        grid=(T // tm,),
        in_specs=[pl.BlockSpec((tm, 1), lambda i: (i, 0)),
                  pl.BlockSpec((1, MLA_SLOT), lambda i: (0, 0))],
        out_specs=[pl.BlockSpec((tm, MLA_SLOT), lambda i: (i, 0))] * 2,
        out_shape=[jax.ShapeDtypeStruct((T, MLA_SLOT), F32)] * 2,
        compiler_params=_cparams(1),
        name="rope_tables",
    )(pos_f32, invf_lane)


def _mla_prep_body(za_ref, c0_ref, s0_ref, qn_ref, kvn_ref, wqa_ref, wqb_ref, wk_ref, wv_ref,
                   q_ref, k_ref, vt_ref):
    za = za_ref[...]
    cq = za[:, :MLA_Q_RANK]
    ckv = za[:, MLA_Q_RANK:MLA_Q_RANK + MLA_KV_RANK]
    kra = za[:, 384:512]
    krb = za[:, 512:640]
    cqn = _rms(cq, qn_ref[...]).astype(BF16)
    ckvn = _rms(ckv, kvn_ref[...]).astype(BF16)
    c0 = c0_ref[...]
    s0 = s0_ref[...]
    c8 = jnp.concatenate([c0] * MLA_HEADS, axis=1)
    s8 = jnp.concatenate([s0] * MLA_HEADS, axis=1)
    scale = (MLA_NOPE + MLA_ROPE) ** -0.5
    q = (_dot(cqn, wqa_ref[...]) * c8 + _dot(cqn, wqb_ref[...]) * s8) * scale
    q_ref[...] = q.astype(BF16)
    krot = kra * c0 + krb * s0
    k = _dot(ckvn, wk_ref[...]) + jnp.concatenate([krot] * MLA_HEADS, axis=1)
    k_ref[...] = k.astype(BF16)
    v = _dot(ckvn, wv_ref[...])
    one_lane = (_iota(v.shape, 1) & (MLA_SLOT - 1)) == MLA_ONE
    vt_ref[...] = jnp.where(one_lane, 1.0, v).T.astype(BF16)


def _mla_prep(za, c0, s0, qn, kvn, wqa, wqb, wk, wv, l, tk):
    T = za.shape[0]
    tok = lambda n: pl.BlockSpec((tk, n), lambda i: (i, 0))
    return pl.pallas_call(
        _mla_prep_body,
        grid=(T // tk,),
        in_specs=[tok(za.shape[1]), tok(MLA_SLOT), tok(MLA_SLOT),
                  _vec_spec(MLA_Q_RANK, l), _vec_spec(MLA_KV_RANK, l),
                  _mat_spec(MLA_Q_RANK, MLA_W, l), _mat_spec(MLA_Q_RANK, MLA_W, l),
                  _mat_spec(MLA_KV_RANK, MLA_W, l), _mat_spec(MLA_KV_RANK, MLA_W, l)],
        out_specs=[tok(MLA_W), tok(MLA_W),
                   pl.BlockSpec((None, MLA_W, tk), lambda i: (i, 0, 0))],
        out_shape=[jax.ShapeDtypeStruct((T, MLA_W), BF16)] * 2
        + [jax.ShapeDtypeStruct((T // tk, MLA_W, tk), BF16)],
        compiler_params=_cparams(1),
        name="mla_prep",
    )(za, c0, s0, qn, kvn, wqa, wqb, wk, wv)


def _mla_attn_body(q_ref, k_ref, vt_ref, o_ref, s_ref, m_ref, acc_ref, *, tq, tk):
    qi = pl.program_id(2)
    m_ref[...] = jnp.full_like(m_ref, -1e30)
    acc_ref[...] = jnp.zeros_like(acc_ref)
    heads = [slice(hh * MLA_SLOT, (hh + 1) * MLA_SLOT) for hh in range(MLA_HPS)]

    def produce(slot, ki):
        r0 = pl.multiple_of(ki * tk, tk)
        for hh, sl in enumerate(heads):
            s_ref[slot, hh] = _dot_nt(k_ref[pl.ds(r0, tk), sl], q_ref[:, sl])

    def consume(slot, ki, diag):
        for hh, sl in enumerate(heads):
            st = s_ref[slot, hh]
            if diag is not None:
                key = _iota(st.shape, 0) + diag * tk
                st = jnp.where(key <= _iota(st.shape, 1), st, -1e30)
            m_old = m_ref[hh]
            m_new = jnp.maximum(m_old, jnp.max(st, axis=0, keepdims=True))
            p = jnp.exp(st - m_new).astype(BF16)
            acc_ref[hh] = jnp.exp(m_old - m_new) * acc_ref[hh] + _dot(vt_ref[ki, sl, :], p)
            m_ref[hh] = m_new

    def body(j, carry):
        produce(1, 2 * j + 1)
        consume(0, 2 * j, None)
        produce(0, 2 * j + 2)
        consume(1, 2 * j + 1, None)
        return carry

    produce(0, 0)
    lax.fori_loop(0, qi, body, 0)
    produce(1, 2 * qi + 1)
    consume(0, 2 * qi, 0)
    consume(1, 2 * qi + 1, 1)
    for hh in range(MLA_HPS):
        acc = acc_ref[hh]
        o_t = acc / acc[MLA_ONE:MLA_ONE + 1, :]
        o_ref[:, hh * MLA_SLOT:(hh + 1) * MLA_SLOT] = o_t.T.astype(o_ref.dtype)


def _mla_attn(q, k, vt, B, S, tq, tk):
    assert tq == 2 * tk
    T = q.shape[0]
    nq, nk = S // tq, S // tk
    w = MLA_HPS * MLA_SLOT
    return pl.pallas_call(
        functools.partial(_mla_attn_body, tq=tq, tk=tk),
        grid=(B, MLA_HEADS // MLA_HPS, nq),
        in_specs=[
            pl.BlockSpec((tq, w), lambda b, h, i: (b * nq + i, h)),
            pl.BlockSpec((S, w), lambda b, h, i: (b, h), pipeline_mode=pl.Buffered(1)),
            pl.BlockSpec((nk, w, tk), lambda b, h, i: (b, h, 0), pipeline_mode=pl.Buffered(1)),
        ],
        out_specs=pl.BlockSpec((tq, w), lambda b, h, i: (b * nq + i, h)),
        out_shape=jax.ShapeDtypeStruct((T, MLA_W), BF16),
        scratch_shapes=[pltpu.VMEM((2, MLA_HPS, tk, tq), F32),
                        pltpu.VMEM((MLA_HPS, 1, tq), F32),
                        pltpu.VMEM((MLA_HPS, MLA_SLOT, tq), F32)],
        compiler_params=_cparams(3),
        name="mla_attn",
    )(q, k, vt)


def _gla_pair_masks():
    i = np.arange(GLA_HEADS * CHUNK)[:, None] % CHUNK
    j = np.arange(CHUNK)[None, :]
    same = [(i >> (6 - lv)) == (j >> (6 - lv)) for lv in range(GLA_LEVELS)]
    return jnp.asarray(np.stack(same + [i == j]).astype(np.float32))


def _gla_body(v_ref, g_ref, q_ref, k_ref, a_ref, walpha_ref, balpha_ref, gn_ref, pair_ref,
              o_ref, ht_ref, *, n_chunks, n_batch):
    @pl.when(pl.program_id(0) == 0)
    def _():
        ht_ref[...] = jnp.zeros_like(ht_ref)

    C = CHUNK
    walpha_hi, walpha_lo = walpha_ref[0], walpha_ref[1]
    balpha = balpha_ref[...]
    gn = gn_ref[...]
    tri = (_iota((C, C), 1) <= _iota((C, C), 0)).astype(BF16)
    st_r, st_c = _iota((GLA_HEADS * C, GLA_QK), 0), _iota((GLA_HEADS * C, GLA_QK), 1)
    head_mask = ((st_r >> 6) == (st_c >> 6)).astype(BF16)
    ht_r, ht_c = _iota((GLA_W, GLA_QK), 0), _iota((GLA_W, GLA_QK), 1)
    state_mask = (ht_r >> 7) == (ht_c >> 6)
    tok = _iota((C, GLA_QK), 0)
    halves = [C >> (lv + 1) for lv in range(GLA_LEVELS)]

    def stack(x):
        return jnp.concatenate([x.astype(BF16)] * GLA_HEADS, axis=0) * head_mask

    def level_operands(q, k, cb):
        last = {1: cb}
        for s in halves[:0:-1]:
            f = last[s]
            last[2 * s] = jnp.where((tok & s) != 0, f, pltpu.roll(f, C - s, 0))
        ops = []
        for hs in halves:
            f = last[hs]
            right = (tok & hs) != 0
            d = cb - jnp.where(right, pltpu.roll(f, hs, 0), f)
            ql = q * jnp.exp(jnp.where(right, d, GLA_NEG))
            kl = k * jnp.exp(jnp.where(right, GLA_NEG, -d))
            ops.append((stack(ql), kl.astype(BF16)))
        ops.append((stack(q), k.astype(BF16)))
        return ops

    def chunk(c, carry):
        rows = pl.ds(pl.multiple_of(c * C, C), C)
        bs = range(n_batch)
        xs = [_dot_split(a_ref[b, rows, :], walpha_hi, walpha_lo) + balpha for b in bs]
        cum = [_dot_01(tri, -_softplus(-x) * (1.0 / GLA_GATE_NORM)) for x in xs]
        st = []
        for b in bs:
            cb = cum[b]
            b_last = cb[C - 1:C, :]
            q = q_ref[b, rows, :] * GLA_DK ** -0.5
            k = k_ref[b, rows, :]
            st.append(dict(
                v=v_ref[b, rows, :].astype(BF16),
                qe=(q * jnp.exp(cb)).astype(BF16),
                ke=(k * jnp.exp(b_last - cb)).astype(BF16),
                gam=jnp.exp(b_last),
                ops=level_operands(q, k, cb)))
        scs = [None] * n_batch
        for lv in range(GLA_LEVELS + 1):
            for b in bs:
                ql, kl = st[b]["ops"][lv]
                part = _dot_nt(ql, kl) * pair_ref[lv]
                scs[b] = part if lv == 0 else scs[b] + part
        osts = [_dot(sc.astype(BF16), s["v"]) for sc, s in zip(scs, st)]
        for b in bs:
            s, ost = st[b], osts[b]
            o_intra = jnp.concatenate(
                [ost[h * C:(h + 1) * C, h * GLA_DV:(h + 1) * GLA_DV] for h in range(GLA_HEADS)],
                axis=1)
            ht = ht_ref[b]
            o = o_intra + _dot_nt(s["qe"], ht.astype(BF16))
            ht_ref[b] = ht * s["gam"] + jnp.where(state_mask, _dot_tn(s["v"], s["ke"]), 0.0)
            outs = []
            for h in range(GLA_HEADS):
                oh = o[:, h * GLA_DV:(h + 1) * GLA_DV]
                outs.append(oh * lax.rsqrt(jnp.mean(oh * oh, axis=-1, keepdims=True) + NORM_EPS))
            on = jnp.concatenate(outs, axis=1) * gn
            g = g_ref[b, rows, :]
            o_ref[b, rows, :] = (on * (g * _sigmoid(g))).astype(o_ref.dtype)
        return carry

    lax.fori_loop(0, n_chunks, chunk, 0)


def _gla(zb, walpha, balpha, gn, l, B, S, ts):
    ns = S // ts
    blk = lambda n, j: pl.BlockSpec((B, ts, n), lambda s: (0, s, j))
    zb = zb.reshape(B, S, zb.shape[1])
    out = pl.pallas_call(
        functools.partial(_gla_body, n_chunks=ts // CHUNK, n_batch=B),
        grid=(ns,),
        in_specs=[
            blk(GLA_W, 0),
            blk(GLA_W, 1),
            blk(GLA_QK, 4),
            blk(GLA_QK, 5),
            blk(128, 12),
            pl.BlockSpec((None, 2, 128, GLA_QK), lambda s: (l, 0, 0, 0)),
            _vec_spec(GLA_QK, l), _vec_spec(GLA_W, l),
            pl.BlockSpec((GLA_LEVELS + 1, GLA_HEADS * CHUNK, CHUNK), lambda s: (0, 0, 0)),
        ],
        out_specs=blk(GLA_W, 0),
        out_shape=jax.ShapeDtypeStruct((B, S, GLA_W), BF16),
        scratch_shapes=[pltpu.VMEM((B, GLA_W, GLA_QK), F32)],
        compiler_params=_cparams(1),
        name="gla",
    )(zb, zb, zb, zb, zb, walpha, balpha, gn, _gla_pair_masks())
    return out.reshape(B * S, GLA_W)


def _rwkv_body(z_ref, mu_ref, w0_ref, wdec_ref, a0_ref, wiclr_ref, wgate_ref, kk_ref, ka_ref,
               rk_ref, lnw_ref, lnb_ref, o_ref, h_ref, zlast_ref, *, n_chunks, n_batch):
    @pl.when(pl.program_id(0) == 0)
    def _():
        h_ref[...] = jnp.zeros_like(h_ref)
        zlast_ref[...] = jnp.zeros_like(zlast_ref)

    C, W, Q = CHUNK, RWKV_W, RWKV_QUAD
    mu = mu_ref[...]
    w0, a0 = w0_ref[...], a0_ref[...]
    wdec_hi, wdec_lo = wdec_ref[0], wdec_ref[1]
    wiclr_hi, wiclr_lo = wiclr_ref[0], wiclr_ref[1]
    wgate = wgate_ref[...]
    k_k, k_a, r_k = kk_ref[...], ka_ref[...], rk_ref[...]
    ln_w, ln_b = lnw_ref[...], lnb_ref[...]

    tri = (_iota((C, C), 1) <= _iota((C, C), 0)).astype(BF16)
    sq_r, sq_c = _iota((Q, Q), 0), _iota((Q, Q), 1)
    head_blk = (sq_r >> 6) == (sq_c >> 6)
    eye_q = sq_r == sq_c
    e_seg = head_blk.astype(BF16)
    wd_t, wd_s = _iota((C, Q), 0), _iota((C, Q), 1) & (C - 1)
    strict = wd_s < wd_t
    incl = wd_s <= wd_t
    eye_wide = (wd_s == wd_t).astype(F32)
    row0 = _iota((C, RWKV_COLS_PAD), 0) == 0

    def bd(x):
        return jnp.concatenate([x.astype(BF16)] * 4, axis=0) * e_seg

    def quad(x, i):
        return x[:, i * Q:(i + 1) * Q]

    def segsum(x):
        return jnp.concatenate([_segsum(quad(x, i), e_seg) for i in range(RWKV_NQ)], axis=1)

    def chunk_one(b, rows):
        z = z_ref[b, rows, :]
        zp = jnp.where(row0, zlast_ref[b, 0:1, :], pltpu.roll(z, 1, 0))
        zlast_ref[b, 0:1, :] = z[C - 1:C, :]
        z = z + mu * (zp - z)
        r = z[:, 0:W]
        k = z[:, W:2 * W]
        v = z[:, 2 * W:3 * W]
        m0 = z[:, 3 * W:3 * W + 128]
        m12 = z[:, 3 * W + 128:3 * W + 384]
        w_log = -_softplus(-(w0 + _dot_split(jnp.tanh(m0), wdec_hi, wdec_lo))) - 0.5
        lw = -jnp.exp(w_log)
        a = _sigmoid(a0 + _dot_split(m0, wiclr_hi, wiclr_lo))
        g = _dot(_sigmoid(m12).astype(BF16), wgate)
        kk = k * k_k
        kkn = kk * lax.rsqrt(jnp.maximum(segsum(kk * kk), 1e-24))
        k2 = k * (1.0 + (a - 1.0) * k_a)
        beta = kkn * a
        cs = _dot_01(tri, lw)
        c_last = cs[C - 1:C, :]
        dec_in = jnp.exp(-cs)
        dec_out = jnp.exp(c_last - cs)
        kt = kkn * jnp.exp(cs - lw)
        rt = r * jnp.exp(cs)
        bh = beta * dec_in
        kh = k2 * dec_in
        kbar = k2 * dec_out
        bbar = beta * dec_out
        gam = jnp.exp(c_last)
        chains = [dict(b=b, i=i, kt=quad(kt, i), rt=quad(rt, i), v=quad(v, i), bh=quad(bh, i),
                       kh=quad(kh, i), kbar=quad(kbar, i), bbar=quad(bbar, i), gam=quad(gam, i))
                  for i in range(RWKV_NQ)]
        return dict(r=r, k2=k2, v=v, g=g), chains

    def epilogue(b, rows, tok, y):
        mean = segsum(y) * (1.0 / RWKV_N)
        d = y - mean
        var = segsum(d * d) * (1.0 / RWKV_N)
        yn = d * lax.rsqrt(var + RWKV_GN_EPS) * ln_w + ln_b
        bonus = segsum(tok["r"] * tok["k2"] * r_k) * tok["v"]
        o_ref[b, rows, :] = ((yn + bonus) * tok["g"]).astype(o_ref.dtype)

    def chunk(c, carry):
        groups, chains = [], []
        for sub in range(RWKV_CPI):
            rows = pl.ds(pl.multiple_of((c * RWKV_CPI + sub) * C, C), C)
            for b in range(n_batch):
                tok, ch = chunk_one(b, rows)
                groups.append((b, rows, tok, ch))
                chains += ch
        for ch in chains:
            lhs = jnp.concatenate([ch["kt"], ch["rt"]], axis=0).astype(BF16)
            rhs = jnp.concatenate([bd(ch["bh"]), bd(ch["kh"])], axis=0)
            sc = _dot_nt(lhs, rhs)
            ch["a_kk"] = jnp.where(strict, sc[:C, Q:], 0.0)
            ch["a_rb"] = jnp.where(incl, sc[C:, :Q], 0.0)
            ch["a_rk"] = jnp.where(incl, sc[C:, Q:], 0.0)
            ch["n"] = -jnp.where(strict, sc[:C, :Q], 0.0)
            ch["t"] = eye_wide + ch["n"]
        for ch in chains:
            ch["p"] = _dot(ch["n"].astype(BF16), bd(ch["n"]))
            ch["bd_v"] = bd(ch["v"])
            ch["akk_v"] = _dot(ch["a_kk"].astype(BF16), ch["bd_v"])
        for _ in range(4):
            for ch in chains:
                tp = _dot(jnp.concatenate([ch["t"], ch["p"]], axis=0).astype(BF16), bd(ch["p"]))
                ch["t"] = ch["t"] + tp[:C]
                ch["p"] = tp[C:]
        for ch in chains:
            ch["t"] = ch["t"] + _dot(ch["t"].astype(BF16), bd(ch["p"]))
        for ch in chains:
            w12 = _dot(ch["t"].astype(BF16),
                       jnp.concatenate([bd(ch["akk_v"]), bd(ch["kt"])], axis=1))
            ch["w1"], ch["w2"] = w12[:, :Q], w12[:, Q:]
        for ch in chains:
            w1, w2, bbar = ch["w1"], ch["w2"], ch["bbar"]
            ch["y0"] = _dot(jnp.concatenate([ch["a_rk"], ch["a_rb"]], axis=1).astype(BF16),
                            jnp.concatenate([ch["bd_v"], bd(-w1)], axis=0))
            ch["rp"] = (ch["rt"] - _dot(ch["a_rb"].astype(BF16), bd(w2))).astype(BF16)
            ch["p_bd"] = (jnp.where(eye_q, ch["gam"], 0.0) - jnp.where(
                head_blk, _dot_tn(bbar.astype(BF16), w2.astype(BF16)), 0.0)).astype(BF16)
            ch["q_bd"] = jnp.where(
                head_blk,
                _dot_tn(jnp.concatenate([ch["kbar"], -bbar], axis=0).astype(BF16),
                        jnp.concatenate([ch["v"], w1], axis=0).astype(BF16)), 0.0)
        for b, rows, tok, chs in groups:
            ys = []
            for ch in chs:
                hb = h_ref[b, ch["i"]].astype(BF16)
                ys.append(_dot(ch["rp"], hb) + ch["y0"])
                h_ref[b, ch["i"]] = _dot(ch["p_bd"], hb) + ch["q_bd"]
            epilogue(b, rows, tok, jnp.concatenate(ys, axis=1))
        return carry

    lax.fori_loop(0, n_chunks // RWKV_CPI, chunk, 0)


def _rwkv(zc, p, l, B, S, ts):
    ns = S // ts
    W, Q = RWKV_W, RWKV_QUAD
    split_w = pl.BlockSpec((None, 2, 128, W), lambda *_: (l, 0, 0, 0))
    out = pl.pallas_call(
        functools.partial(_rwkv_body, n_chunks=ts // CHUNK, n_batch=B),
        grid=(ns,),
        in_specs=[
            pl.BlockSpec((B, ts, RWKV_COLS_PAD), lambda s: (0, s, 0)),
            _vec_spec(RWKV_COLS_PAD, l), _vec_spec(W, l), split_w,
            _vec_spec(W, l), split_w, _mat_spec(256, W, l),
            _vec_spec(W, l), _vec_spec(W, l), _vec_spec(W, l), _vec_spec(W, l), _vec_spec(W, l),
        ],
        out_specs=pl.BlockSpec((B, ts, W), lambda s: (0, s, 0)),
        out_shape=jax.ShapeDtypeStruct((B, S, W), BF16),
        scratch_shapes=[pltpu.VMEM((B, RWKV_NQ, Q, Q), F32),
                        pltpu.VMEM((B, 8, RWKV_COLS_PAD), F32)],
        compiler_params=_cparams(1),
        name="rwkv",
    )(zc.reshape(B, S, RWKV_COLS_PAD), p["mu"], p["w0"], p["wdec"], p["a0"], p["wiclr"],
      p["wgate"], p["k_k"], p["k_a"], p["r_k"], p["ln_w"], p["ln_b"])
    return out.reshape(B * S, W)


def _merge_body(x_ref, oa_ref, ob_ref, oc_ref, zd_ref, wa_ref, wb_ref, wc_ref, wo_ref, g_ref, o_ref):
    zd = zd_ref[...]
    D = D_MODEL
    merged = (_sigmoid(zd[:, 0:D]) * _dot(oa_ref[...], wa_ref[...])
              + _sigmoid(zd[:, D:2 * D]) * _dot(ob_ref[...], wb_ref[...])
              + _sigmoid(zd[:, 2 * D:3 * D]) * _dot(oc_ref[...], wc_ref[...]))
    y = _dot(merged.astype(BF16), wo_ref[...])
    o_ref[...] = x_ref[...] + _rms(y, g_ref[...])


def _merge(x, oa, ob, oc, zd, wa, wb, wc, wo, ng, l, tm):
    T, D = x.shape
    tok = lambda n: pl.BlockSpec((tm, n), lambda i: (i, 0))
    return pl.pallas_call(
        _merge_body,
        grid=(T // tm,),
        in_specs=[tok(D), tok(MLA_W), tok(GLA_W), tok(RWKV_W), tok(N_BRANCH * D),
                  _mat_spec(MLA_W, D, l), _mat_spec(GLA_W, D, l), _mat_spec(RWKV_W, D, l),
                  _mat_spec(D, D, l), pl.BlockSpec((None, 1, D), lambda i: (l * 8 + 3, 0, 0))],
        out_specs=tok(D),
        out_shape=jax.ShapeDtypeStruct((T, D), F32),
        compiler_params=_cparams(1),
        name="merge",
    )(x, oa, ob, oc, zd, wa, wb, wc, wo, ng)


def _mem_kv_body(mem_ref, g_ref, w_ref, o_ref):
    o_ref[...] = _dot(_rms(mem_ref[...], g_ref[...]).astype(BF16), w_ref[...]).astype(o_ref.dtype)


def _mem_kv(mem, mem_norm, wkv, l):
    B, M, D = mem.shape
    return pl.pallas_call(
        _mem_kv_body,
        grid=(B,),
        in_specs=[pl.BlockSpec((None, M, D), lambda b: (b, 0, 0)), _vec_spec(D, l),
                  _mat_spec(D, 2 * D, l)],
        out_specs=pl.BlockSpec((None, M, 2 * D), lambda b: (b, 0, 0)),
        out_shape=jax.ShapeDtypeStruct((B, M, 2 * D), BF16),
        compiler_params=_cparams(1),
        name="mem_kv",
    )(mem, mem_norm, wkv)


def _mem_attn_body(x_ref, gpre_ref, wq_ref, kv_ref, wo_ref, gpost_ref, o_ref):
    x = x_ref[...]
    D = D_MODEL
    h = _rms(x, gpre_ref[...]).astype(BF16)
    q = (_dot(h, wq_ref[...]) * MEM_HD ** -0.5).astype(BF16)
    kv = kv_ref[...]
    outs = []
    for hh in range(MEM_HEADS):
        sl = slice(hh * MEM_HD, (hh + 1) * MEM_HD)
        s = _dot_nt(q[:, sl], kv[:, sl])
        p = jnp.exp(s - jnp.max(s, axis=-1, keepdims=True))
        o = _dot(p.astype(BF16), kv[:, D + hh * MEM_HD:D + (hh + 1) * MEM_HD])
        outs.append(o / jnp.sum(p, axis=-1, keepdims=True))
    o = jnp.concatenate(outs, axis=1).astype(BF16)
    o_ref[...] = x + _rms(_dot(o, wo_ref[...]), gpost_ref[...])


def _mem_attn(x, ng, wq, kv, wo, l, S, tm):
    T, D = x.shape
    M = kv.shape[1]
    per_b = S // tm
    return pl.pallas_call(
        _mem_attn_body,
        grid=(T // tm,),
        in_specs=[pl.BlockSpec((tm, D), lambda i: (i, 0)),
                  pl.BlockSpec((None, 1, D), lambda i: (l * 8 + 4, 0, 0)),
                  _mat_spec(D, D, l),
                  pl.BlockSpec((None, M, 2 * D), lambda i: (i // per_b, 0, 0)),
                  _mat_spec(D, D, l),
                  pl.BlockSpec((None, 1, D), lambda i: (l * 8 + 5, 0, 0))],
        out_specs=pl.BlockSpec((tm, D), lambda i: (i, 0)),
        out_shape=jax.ShapeDtypeStruct((T, D), F32),
        compiler_params=_cparams(1),
        name="mem_attn",
    )(x, ng, wq, kv, wo, ng)


def _prepare_params(w_in, mla_w_uq, mla_w_ukv, gla_w_alpha, gla_norm, rwkv_mu, rwkv_w_decay,
                    rwkv_w_iclr, rwkv_w_gate, w_branch):
    L, D = w_in.shape[0], w_in.shape[1]
    zc = lambda n: jnp.zeros((L, D, n), F32)
    o = 0
    cuts = {}
    for name, n in (("c_q", 256), ("c_kv", 128), ("k_rope", 32), ("gla_q", 256), ("gla_k", 256),
                    ("gla_v", 512), ("gla_g", 512), ("gla_a", 16), ("rwkv", 1824), ("gates", 3072)):
        cuts[name] = w_in[:, :, o:o + n]
        o += n
    kr = cuts["k_rope"]
    kr_b = jnp.concatenate([-kr[..., 16:], kr[..., :16]], axis=-1)
    wa = jnp.concatenate([cuts["c_q"], cuts["c_kv"], zc(64), kr, zc(32), zc(64), kr_b, zc(32)], -1)
    wb = jnp.concatenate([cuts["gla_v"], cuts["gla_g"], cuts["gla_q"], cuts["gla_k"],
                          cuts["gla_a"], zc(112)], -1)
    wc = jnp.concatenate([cuts["rwkv"], zc(RWKV_MISC - 288)], -1)
    wd = cuts["gates"]

    wuq = mla_w_uq.reshape(L, MLA_Q_RANK, MLA_HEADS, MLA_NOPE + MLA_ROPE)
    nope, rope = wuq[..., :MLA_NOPE], wuq[..., MLA_NOPE:]
    zq = lambda n: jnp.zeros((L, MLA_Q_RANK, MLA_HEADS, n), F32)
    wqa = jnp.concatenate([nope, rope, zq(32)], -1).reshape(L, MLA_Q_RANK, MLA_W)
    rope_b = jnp.concatenate([-rope[..., 16:], rope[..., :16]], -1)
    wqb = jnp.concatenate([zq(64), rope_b, zq(32)], -1).reshape(L, MLA_Q_RANK, MLA_W)
    wukv = mla_w_ukv.reshape(L, MLA_KV_RANK, MLA_HEADS, 128)
    zk = jnp.zeros((L, MLA_KV_RANK, MLA_HEADS, 64), F32)
    wk = jnp.concatenate([wukv[..., :64], zk], -1).reshape(L, MLA_KV_RANK, MLA_W)
    wv = jnp.concatenate([wukv[..., 64:], zk], -1).reshape(L, MLA_KV_RANK, MLA_W)

    bra = w_branch[:, :512].reshape(L, MLA_HEADS, 64, D)
    bra = jnp.concatenate([bra, jnp.zeros_like(bra)], axis=2).reshape(L, MLA_W, D)
    brb = w_branch[:, 512:512 + GLA_W]
    brc = w_branch[:, 512 + GLA_W:]

    walpha = jnp.concatenate(
        [gla_w_alpha, jnp.zeros((L, 128 - GLA_GATE_RANK, GLA_QK), F32)], axis=1)
    gn = jnp.tile(gla_norm, (1, GLA_HEADS))[:, None, :]
    mu = jnp.concatenate([rwkv_mu, jnp.zeros((L, RWKV_MISC - 288), F32)], -1)[:, None, :]
    zr = lambda n: jnp.zeros((L, n, RWKV_W), F32)
    def hi_lo(w):
        hi = w.astype(BF16)
        return jnp.stack([hi, (w - hi.astype(F32)).astype(BF16)], axis=1)

    wdec = hi_lo(jnp.concatenate([rwkv_w_decay, zr(64)], axis=1))
    wiclr = hi_lo(jnp.concatenate([zr(64), rwkv_w_iclr], axis=1))
    wgate = jnp.concatenate([rwkv_w_gate, zr(256 - RWKV_GATE_RANK)], axis=1)
    bf = lambda w: w.astype(BF16)
    return dict(wa=bf(wa), wb=bf(wb), wc=bf(wc), wd=bf(wd), wqa=bf(wqa), wqb=bf(wqb), wk=bf(wk),
                wv=bf(wv), bra=bf(bra), brb=bf(brb), brc=bf(brc), walpha=hi_lo(walpha), gn=gn, mu=mu,
                wdec=wdec, wiclr=wiclr, wgate=bf(wgate))


def kernel(x, mem, positions, norm_g, w_ffn_in, w_ffn_out, w_in, mla_q_norm, mla_w_uq, mla_kv_norm, mla_w_ukv, gla_w_alpha, gla_b_alpha, gla_norm, rwkv_mu, rwkv_w0, rwkv_w_decay, rwkv_a0, rwkv_w_iclr, rwkv_w_gate, rwkv_k_k, rwkv_k_a, rwkv_r_k, rwkv_ln_w, rwkv_ln_b, w_branch, w_out, mem_norm, mem_wq, mem_wkv, mem_wo):
    B, S, D = x.shape
    L = norm_g.shape[0]
    T = B * S
    tm = min(512, S)
    tp = min(256, S)
    tq = min(1024, S)
    tk = tq // 2
    ts = min(256, S)
    tf = D_FF // 2

    pp = _prepare_params(w_in, mla_w_uq, mla_w_ukv, gla_w_alpha, gla_norm, rwkv_mu, rwkv_w_decay,
                         rwkv_w_iclr, rwkv_w_gate, w_branch)
    bf = lambda w: w.astype(BF16)
    ffn_in, ffn_out = bf(w_ffn_in), bf(w_ffn_out)
    wout, wq, wkv, wo = bf(w_out), bf(mem_wq), bf(mem_wkv), bf(mem_wo)
    ng = norm_g.reshape(L * 8, 1, D)
    row = lambda p: p[:, None, :]
    rw = dict(mu=pp["mu"], w0=row(rwkv_w0), wdec=pp["wdec"], a0=row(rwkv_a0), wiclr=pp["wiclr"],
              wgate=pp["wgate"], k_k=row(rwkv_k_k), k_a=row(rwkv_k_a), r_k=row(rwkv_r_k),
              ln_w=row(rwkv_ln_w), ln_b=row(rwkv_ln_b))

    inv_freq = ROPE_THETA ** (-jnp.arange(0, MLA_ROPE, 2, dtype=F32) / MLA_ROPE)
    invf_lane = jnp.concatenate(
        [jnp.zeros((MLA_NOPE,), F32), inv_freq, inv_freq, jnp.zeros((32,), F32)])[None, :]
    c0, s0 = _rope_tables(positions.astype(F32).reshape(T, 1), invf_lane, tm)

    x = x.reshape(T, D)
    for l in range(L):
        x = _ffn(x, ng, ffn_in, ffn_out, l, 0, tm, tf)
        za, zb, zc, zd = _normproj(x, ng, l * 8 + 2,
                                   [pp["wa"], pp["wb"], pp["wc"], pp["wd"]], l, tp)
        q, k, vt = _mla_prep(za, c0, s0, row(mla_q_norm), row(mla_kv_norm), pp["wqa"], pp["wqb"],
                             pp["wk"], pp["wv"], l, tk)
        o_a = _mla_attn(q, k, vt, B, S, tq, tk)
        o_b = _gla(zb, pp["walpha"], row(gla_b_alpha), pp["gn"], l, B, S, ts)
        o_c = _rwkv(zc, rw, l, B, S, ts)
        x = _merge(x, o_a, o_b, o_c, zd, pp["bra"], pp["brb"], pp["brc"], wout, ng, l, tm)
        kv = _mem_kv(mem, row(mem_norm), wkv, l)
        x = _mem_attn(x, ng, wq, kv, wo, l, S, tm)
        x = _ffn(x, ng, ffn_in, ffn_out, l, 1, tm, tf)
    return x.reshape(B, S, D)
```

```python
import functools

import jax
import jax.numpy as jnp
import numpy as np
from jax import lax
from jax.experimental import pallas as pl
from jax.experimental.pallas import tpu as pltpu

F32 = jnp.float32
BF16 = jnp.bfloat16
HI = lax.Precision.HIGHEST

D_MODEL = 1024
D_FF = 2816
NORM_EPS = 1e-6
MLA_HEADS = 8
MLA_NOPE = 64
MLA_ROPE = 32
MLA_Q_RANK = 256
MLA_KV_RANK = 128
ROPE_THETA = 10000.0
MLA_SLOT = 128
MLA_W = MLA_HEADS * MLA_SLOT
MLA_HPS = 2
MLA_VSLOT = 80
MLA_ONE = 64
GLA_HEADS = 4
GLA_DK = 64
GLA_DV = 128
GLA_GATE_RANK = 16
GLA_GATE_NORM = 16.0
GLA_QK = GLA_HEADS * GLA_DK
GLA_W = GLA_HEADS * GLA_DV
GLA_LEVELS = 6
GLA_NEG = -1e30
RWKV_HEADS = 8
RWKV_N = 64
RWKV_DECAY_RANK = 64
RWKV_ICLR_RANK = 64
RWKV_GATE_RANK = 160
RWKV_GN_EPS = 64e-5
RWKV_W = RWKV_HEADS * RWKV_N
RWKV_MISC = 384
RWKV_COLS_PAD = 3 * RWKV_W + RWKV_MISC
RWKV_QUAD = 4 * RWKV_N
RWKV_NQ = RWKV_W // RWKV_QUAD
RWKV_CPI = 2
MEM_HEADS = 4
MEM_HD = D_MODEL // MEM_HEADS
N_BRANCH = 3
CHUNK = 64

VMEM_LIMIT_BYTES = 56 * 1024 * 1024


def _cparams(n_axes, flags=None):
    return pltpu.CompilerParams(
        dimension_semantics=("arbitrary",) * n_axes,
        vmem_limit_bytes=VMEM_LIMIT_BYTES,
        flags=flags,
    )


def _dot(a, b, precision=None):
    return jnp.dot(a, b, preferred_element_type=F32, precision=precision)


def _dot_nt(a, b, precision=None):
    return lax.dot_general(a, b, (((1,), (1,)), ((), ())),
                           preferred_element_type=F32, precision=precision)


def _dot_tn(a, b, precision=None):
    return lax.dot_general(a, b, (((0,), (0,)), ((), ())),
                           preferred_element_type=F32, precision=precision)


def _rms(x, g, eps=NORM_EPS):
    return x * lax.rsqrt(jnp.mean(x * x, axis=-1, keepdims=True) + eps) * g


def _sigmoid(x):
    return 1.0 / (1.0 + jnp.exp(-x))


def _softplus(x):
    return jnp.maximum(x, 0.0) + jnp.log(1.0 + jnp.exp(-jnp.abs(x)))


def _split2(x):
    hi = x.astype(BF16)
    return hi, (x - hi.astype(F32)).astype(BF16)


def _dot_split(x, w_hi, w_lo):
    hi, lo = _split2(x)
    m = x.shape[0]
    top = _dot(jnp.concatenate([hi, lo], axis=0), w_hi)
    return top[:m] + top[m:] + _dot(hi, w_lo)


def _dot_01(m01_bf16, x):
    p1 = x.astype(BF16)
    r1 = x - p1.astype(F32)
    p2 = r1.astype(BF16)
    p3 = (r1 - p2.astype(F32)).astype(BF16)
    n = x.shape[1]
    out = _dot(m01_bf16, jnp.concatenate([p1, p2, p3], axis=1))
    return out[:, :n] + out[:, n:2 * n] + out[:, 2 * n:]


def _iota(shape, dim):
    return lax.broadcasted_iota(jnp.int32, shape, dim)


def _segsum(x, e_bf16):
    hi = x.astype(BF16)
    lo = (x - hi.astype(F32)).astype(BF16)
    return _dot(hi, e_bf16) + _dot(lo, e_bf16)


def _vec_spec(n, l):
    return pl.BlockSpec((None, 1, n), lambda *_: (l, 0, 0))


def _mat_spec(r, c, l):
    return pl.BlockSpec((None, r, c), lambda *_: (l, 0, 0))


def _ffn_body(x_ref, gpre_ref, wg_ref, wu_ref, wo_ref, gpost_ref, o_ref, h_ref, acc_ref):
    j = pl.program_id(1)

    @pl.when(j == 0)
    def _():
        h_ref[...] = _rms(x_ref[...], gpre_ref[...]).astype(BF16)
        acc_ref[...] = jnp.zeros_like(acc_ref)

    h = h_ref[...]
    g = _dot(h, wg_ref[...])
    u = _dot(h, wu_ref[...])
    act = (g * _sigmoid(g) * u).astype(BF16)
    acc_ref[...] += _dot(act, wo_ref[...])

    @pl.when(j == pl.num_programs(1) - 1)
    def _():
        o_ref[...] = x_ref[...] + 0.5 * _rms(acc_ref[...], gpost_ref[...])


def _ffn(x, ng, w_in, w_out, l, k, tm, tf):
    T, D = x.shape
    ff = w_out.shape[2]
    nf = ff // tf
    g_pre, g_post = l * 8 + 6 * k, l * 8 + 6 * k + 1
    return pl.pallas_call(
        _ffn_body,
        grid=(T // tm, nf),
        in_specs=[
            pl.BlockSpec((tm, D), lambda i, j: (i, 0)),
            pl.BlockSpec((None, 1, D), lambda i, j: (g_pre, 0, 0)),
            pl.BlockSpec((None, None, D, tf), lambda i, j: (l, k, 0, j)),
            pl.BlockSpec((None, None, D, tf), lambda i, j: (l, k, 0, j + nf)),
            pl.BlockSpec((None, None, tf, D), lambda i, j: (l, k, j, 0)),
            pl.BlockSpec((None, 1, D), lambda i, j: (g_post, 0, 0)),
        ],
        out_specs=pl.BlockSpec((tm, D), lambda i, j: (i, 0)),
        out_shape=jax.ShapeDtypeStruct((T, D), F32),
        scratch_shapes=[pltpu.VMEM((tm, D), BF16), pltpu.VMEM((tm, D), F32)],
        compiler_params=_cparams(2),
        name="ffn",
    )(x, ng, w_in, w_in, w_out, ng)


def _normproj_body(x_ref, g_ref, *refs):
    n = len(refs) // 2
    h = _rms(x_ref[...], g_ref[...]).astype(BF16)
    for w_ref, o_ref in zip(refs[:n], refs[n:]):
        o_ref[...] = _dot(h, w_ref[...]).astype(o_ref.dtype)


def _normproj(x, ng, g_idx, ws, l, tm):
    T, D = x.shape
    widths = [w.shape[2] for w in ws]
    return pl.pallas_call(
        _normproj_body,
        grid=(T // tm,),
        in_specs=[pl.BlockSpec((tm, D), lambda i: (i, 0)),
                  pl.BlockSpec((None, 1, D), lambda i: (g_idx, 0, 0))]
        + [pl.BlockSpec((None, D, n), lambda i: (l, 0, 0), pipeline_mode=pl.Buffered(1))
           for n in widths],
        out_specs=[pl.BlockSpec((tm, n), lambda i: (i, 0)) for n in widths],
        out_shape=[jax.ShapeDtypeStruct((T, n), F32) for n in widths],
        compiler_params=_cparams(1),
        name="mixer_proj",
    )(x, ng, *ws)


def _rope_table_body(pos_ref, invf_ref, c_ref, s_ref):
    ang = pos_ref[...] * invf_ref[...]
    lane = _iota(ang.shape, 1)
    rot = (lane >= MLA_NOPE) & (lane < MLA_NOPE + MLA_ROPE)
    c_ref[...] = jnp.where(lane < MLA_NOPE, 1.0, jnp.where(rot, jnp.cos(ang), 0.0))
    s_ref[...] = jnp.where(rot, jnp.sin(ang), 0.0)


def _rope_tables(pos_f32, invf_lane, tm):
    T = pos_f32.shape[0]
    return pl.pallas_call(
        _rope_table_body,
        grid=(T // tm,),
        in_specs=[pl.BlockSpec((tm, 1), lambda i: (i, 0)),
                  pl.BlockSpec((1, MLA_SLOT), lambda i: (0, 0))],
        out_specs=[pl.BlockSpec((tm, MLA_SLOT), lambda i: (i, 0))] * 2,
        out_shape=[jax.ShapeDtypeStruct((T, MLA_SLOT), F32)] * 2,
        compiler_params=_cparams(1),
        name="rope_tables",
    )(pos_f32, invf_lane)


def _mla_prep_body(za_ref, c0_ref, s0_ref, qn_ref, kvn_ref, wqa_ref, wqb_ref, wk_ref, wv_ref,
                   one_ref, q_ref, k_ref, vt_ref):
    za = za_ref[...]
    cq = za[:, :MLA_Q_RANK]
    ckv = za[:, MLA_Q_RANK:MLA_Q_RANK + MLA_KV_RANK]
    kra = za[:, 384:512]
    krb = za[:, 512:640]
    cqn = _rms(cq, qn_ref[...]).astype(BF16)
    ckvn = _rms(ckv, kvn_ref[...]).astype(BF16)
    c0 = c0_ref[...]
    s0 = s0_ref[...]
    c8 = jnp.concatenate([c0] * MLA_HEADS, axis=1)
    s8 = jnp.concatenate([s0] * MLA_HEADS, axis=1)
    scale = (MLA_NOPE + MLA_ROPE) ** -0.5
    q = (_dot(cqn, wqa_ref[...]) * c8 + _dot(cqn, wqb_ref[...]) * s8) * scale
    q_ref[...] = q.astype(BF16)
    krot = kra * c0 + krb * s0
    k = _dot(ckvn, wk_ref[...]) + jnp.concatenate([krot] * MLA_HEADS, axis=1)
    k_ref[...] = k.astype(BF16)
    v = _dot(ckvn, wv_ref[...]) + one_ref[...]
    vt_ref[...] = v.T.astype(BF16)


def _mla_prep(za, c0, s0, qn, kvn, wqa, wqb, wk, wv, l, tk):
    T = za.shape[0]
    vw = MLA_HEADS * MLA_VSLOT
    tok = lambda n: pl.BlockSpec((tk, n), lambda i: (i, 0))
    one_lane = jnp.asarray((np.arange(vw) % MLA_VSLOT == MLA_ONE).astype(np.float32))[None, :]
    return pl.pallas_call(
        _mla_prep_body,
        grid=(T // tk,),
        in_specs=[tok(za.shape[1]), tok(MLA_SLOT), tok(MLA_SLOT),
                  _vec_spec(MLA_Q_RANK, l), _vec_spec(MLA_KV_RANK, l),
                  _mat_spec(MLA_Q_RANK, MLA_W, l), _mat_spec(MLA_Q_RANK, MLA_W, l),
                  _mat_spec(MLA_KV_RANK, MLA_W, l), _mat_spec(MLA_KV_RANK, vw, l),
                  pl.BlockSpec((1, vw), lambda i: (0, 0))],
        out_specs=[tok(MLA_W), tok(MLA_W),
                   pl.BlockSpec((None, vw, tk), lambda i: (i, 0, 0))],
        out_shape=[jax.ShapeDtypeStruct((T, MLA_W), BF16)] * 2
        + [jax.ShapeDtypeStruct((T // tk, vw, tk), BF16)],
        compiler_params=_cparams(1),
        name="mla_prep",
    )(za, c0, s0, qn, kvn, wqa, wqb, wk, wv, one_lane)


def _mla_attn_body(q_ref, k_ref, vt_ref, o_ref, s_ref, m_ref, acc_ref, *, tq, tk):
    qi = pl.program_id(2)
    m_ref[...] = jnp.full_like(m_ref, -1e30)
    acc_ref[...] = jnp.zeros_like(acc_ref)
    heads = [slice(hh * MLA_SLOT, (hh + 1) * MLA_SLOT) for hh in range(MLA_HPS)]

    def produce(slot, ki):
        r0 = pl.multiple_of(ki * tk, tk)
        for hh, sl in enumerate(heads):
            s_ref[slot, hh] = _dot_nt(k_ref[pl.ds(r0, tk), sl], q_ref[:, sl])

    def consume(slot, ki, diag):
        for hh, sl in enumerate(heads):
            st = s_ref[slot, hh]
            if diag is not None:
                key = _iota(st.shape, 0) + diag * tk
                st = jnp.where(key <= _iota(st.shape, 1), st, -1e30)
            m_old = m_ref[hh]
            m_new = jnp.maximum(m_old, jnp.max(st, axis=0, keepdims=True))
            p = jnp.exp(st - m_new).astype(BF16)
            vt = vt_ref[ki, hh * MLA_VSLOT:(hh + 1) * MLA_VSLOT, :]
            acc_ref[hh] = jnp.exp(m_old - m_new) * acc_ref[hh] + _dot(vt, p)
            m_ref[hh] = m_new

    def body(j, carry):
        produce(1, 2 * j + 1)
        consume(0, 2 * j, None)
        produce(0, 2 * j + 2)
        consume(1, 2 * j + 1, None)
        return carry

    produce(0, 0)
    lax.fori_loop(0, qi, body, 0)
    produce(1, 2 * qi + 1)
    consume(0, 2 * qi, 0)
    consume(1, 2 * qi + 1, 1)
    for hh in range(MLA_HPS):
        acc = acc_ref[hh]
        o_t = acc / acc[MLA_ONE:MLA_ONE + 1, :]
        o_t = jnp.concatenate([o_t, jnp.zeros((MLA_SLOT - MLA_VSLOT, tq), F32)], axis=0)
        o_ref[:, hh * MLA_SLOT:(hh + 1) * MLA_SLOT] = o_t.T.astype(o_ref.dtype)


def _mla_attn(q, k, vt, B, S, tq, tk):
    assert tq == 2 * tk
    T = q.shape[0]
    nq, nk = S // tq, S // tk
    w = MLA_HPS * MLA_SLOT
    return pl.pallas_call(
        functools.partial(_mla_attn_body, tq=tq, tk=tk),
        grid=(B, MLA_HEADS // MLA_HPS, nq),
        in_specs=[
            pl.BlockSpec((tq, w), lambda b, h, i: (b * nq + i, h)),
            pl.BlockSpec((S, w), lambda b, h, i: (b, h)),
            pl.BlockSpec((nk, MLA_HPS * MLA_VSLOT, tk), lambda b, h, i: (b, h, 0)),
        ],
        out_specs=pl.BlockSpec((tq, w), lambda b, h, i: (b * nq + i, h)),
        out_shape=jax.ShapeDtypeStruct((T, MLA_W), BF16),
        scratch_shapes=[pltpu.VMEM((2, MLA_HPS, tk, tq), F32),
                        pltpu.VMEM((MLA_HPS, 1, tq), F32),
                        pltpu.VMEM((MLA_HPS, MLA_VSLOT, tq), F32)],
        compiler_params=_cparams(3),
        name="mla_attn",
    )(q, k, vt)


def _gla_pair_masks():
    i = np.arange(GLA_HEADS * CHUNK)[:, None] % CHUNK
    j = np.arange(CHUNK)[None, :]
    same = [(i >> (6 - lv)) == (j >> (6 - lv)) for lv in range(GLA_LEVELS)]
    return jnp.asarray(np.stack(same + [i == j]).astype(np.float32))


def _gla_body(v_ref, g_ref, q_ref, k_ref, a_ref, walpha_ref, balpha_ref, gn_ref, pair_ref,
              o_ref, ht_ref, *, n_chunks, n_batch):
    @pl.when(pl.program_id(0) == 0)
    def _():
        ht_ref[...] = jnp.zeros_like(ht_ref)

    C = CHUNK
    walpha_hi, walpha_lo = walpha_ref[0], walpha_ref[1]
    balpha = balpha_ref[...]
    gn = gn_ref[...]
    tri = (_iota((C, C), 1) <= _iota((C, C), 0)).astype(BF16)
    st_r, st_c = _iota((GLA_HEADS * C, GLA_QK), 0), _iota((GLA_HEADS * C, GLA_QK), 1)
    head_mask = ((st_r >> 6) == (st_c >> 6)).astype(BF16)
    ht_r, ht_c = _iota((GLA_W, GLA_QK), 0), _iota((GLA_W, GLA_QK), 1)
    state_mask = (ht_r >> 7) == (ht_c >> 6)
    tok = _iota((C, GLA_QK), 0)
    halves = [C >> (lv + 1) for lv in range(GLA_LEVELS)]

    def stack(x):
        return jnp.concatenate([x.astype(BF16)] * GLA_HEADS, axis=0) * head_mask

    def level_operands(q, k, cb):
        last = {1: cb}
        for s in halves[:0:-1]:
            f = last[s]
            last[2 * s] = jnp.where((tok & s) != 0, f, pltpu.roll(f, C - s, 0))
        ops = []
        for hs in halves:
            f = last[hs]
            right = (tok & hs) != 0
            d = cb - jnp.where(right, pltpu.roll(f, hs, 0), f)
            ql = q * jnp.exp(jnp.where(right, d, GLA_NEG))
            kl = k * jnp.exp(jnp.where(right, GLA_NEG, -d))
            ops.append((stack(ql), kl.astype(BF16)))
        ops.append((stack(q), k.astype(BF16)))
        return ops

    def chunk(c, carry):
        rows = pl.ds(pl.multiple_of(c * C, C), C)
        bs = range(n_batch)
        xs = [_dot_split(a_ref[b, rows, :], walpha_hi, walpha_lo) + balpha for b in bs]
        cum = [_dot_01(tri, -_softplus(-x) * (1.0 / GLA_GATE_NORM)) for x in xs]
        st = []
        for b in bs:
            cb = cum[b]
            b_last = cb[C - 1:C, :]
            q = q_ref[b, rows, :] * GLA_DK ** -0.5
            k = k_ref[b, rows, :]
            st.append(dict(
                v=v_ref[b, rows, :].astype(BF16),
                qe=(q * jnp.exp(cb)).astype(BF16),
                ke=(k * jnp.exp(b_last - cb)).astype(BF16),
                gam=jnp.exp(b_last),
                ops=level_operands(q, k, cb)))
        scs = [None] * n_batch
        for lv in range(GLA_LEVELS + 1):
            for b in bs:
                ql, kl = st[b]["ops"][lv]
                part = _dot_nt(ql, kl) * pair_ref[lv]
                scs[b] = part if lv == 0 else scs[b] + part
        osts = [_dot(sc.astype(BF16), s["v"]) for sc, s in zip(scs, st)]
        for b in bs:
            s, ost = st[b], osts[b]
            o_intra = jnp.concatenate(
                [ost[h * C:(h + 1) * C, h * GLA_DV:(h + 1) * GLA_DV] for h in range(GLA_HEADS)],
                axis=1)
            ht = ht_ref[b]
            o = o_intra + _dot_nt(s["qe"], ht.astype(BF16))
            ht_ref[b] = ht * s["gam"] + jnp.where(state_mask, _dot_tn(s["v"], s["ke"]), 0.0)
            outs = []
            for h in range(GLA_HEADS):
                oh = o[:, h * GLA_DV:(h + 1) * GLA_DV]
                outs.append(oh * lax.rsqrt(jnp.mean(oh * oh, axis=-1, keepdims=True) + NORM_EPS))
            on = jnp.concatenate(outs, axis=1) * gn
            g = g_ref[b, rows, :]
            o_ref[b, rows, :] = (on * (g * _sigmoid(g))).astype(o_ref.dtype)
        return carry

    lax.fori_loop(0, n_chunks, chunk, 0)


def _gla(zb, walpha, balpha, gn, l, B, S, ts):
    ns = S // ts
    blk = lambda n, j: pl.BlockSpec((B, ts, n), lambda s: (0, s, j))
    zb = zb.reshape(B, S, zb.shape[1])
    out = pl.pallas_call(
        functools.partial(_gla_body, n_chunks=ts // CHUNK, n_batch=B),
        grid=(ns,),
        in_specs=[
            blk(GLA_W, 0),
            blk(GLA_W, 1),
            blk(GLA_QK, 4),
            blk(GLA_QK, 5),
            blk(128, 12),
            pl.BlockSpec((None, 2, 128, GLA_QK), lambda s: (l, 0, 0, 0)),
            _vec_spec(GLA_QK, l), _vec_spec(GLA_W, l),
            pl.BlockSpec((GLA_LEVELS + 1, GLA_HEADS * CHUNK, CHUNK), lambda s: (0, 0, 0)),
        ],
        out_specs=blk(GLA_W, 0),
        out_shape=jax.ShapeDtypeStruct((B, S, GLA_W), BF16),
        scratch_shapes=[pltpu.VMEM((B, GLA_W, GLA_QK), F32)],
        compiler_params=_cparams(1),
        name="gla",
    )(zb, zb, zb, zb, zb, walpha, balpha, gn, _gla_pair_masks())
    return out.reshape(B * S, GLA_W)


def _rwkv_body(z_ref, mu_ref, w0_ref, wdec_ref, a0_ref, wiclr_ref, wgate_ref, kk_ref, ka_ref,
               rk_ref, lnw_ref, lnb_ref, o_ref, h_ref, zlast_ref, *, n_chunks, n_batch):
    @pl.when(pl.program_id(0) == 0)
    def _():
        h_ref[...] = jnp.zeros_like(h_ref)
        zlast_ref[...] = jnp.zeros_like(zlast_ref)

    C, W, Q = CHUNK, RWKV_W, RWKV_QUAD
    mu = mu_ref[...]
    w0, a0 = w0_ref[...], a0_ref[...]
    wdec_hi, wdec_lo = wdec_ref[0], wdec_ref[1]
    wiclr_hi, wiclr_lo = wiclr_ref[0], wiclr_ref[1]
    wgate = wgate_ref[...]
    k_k, k_a, r_k = kk_ref[...], ka_ref[...], rk_ref[...]
    ln_w, ln_b = lnw_ref[...], lnb_ref[...]

    tri = (_iota((C, C), 1) <= _iota((C, C), 0)).astype(BF16)
    sq_r, sq_c = _iota((Q, Q), 0), _iota((Q, Q), 1)
    head_blk = (sq_r >> 6) == (sq_c >> 6)
    eye_q = sq_r == sq_c
    e_seg = head_blk.astype(BF16)
    wd_t, wd_s = _iota((C, Q), 0), _iota((C, Q), 1) & (C - 1)
    strict = wd_s < wd_t
    incl = wd_s <= wd_t
    eye_wide = (wd_s == wd_t).astype(F32)
    row0 = _iota((C, RWKV_COLS_PAD), 0) == 0

    def bd(x):
        return jnp.concatenate([x.astype(BF16)] * 4, axis=0) * e_seg

    def quad(x, i):
        return x[:, i * Q:(i + 1) * Q]

    def segsum(x):
        return jnp.concatenate([_segsum(quad(x, i), e_seg) for i in range(RWKV_NQ)], axis=1)

    def chunk_one(b, rows):
        z = z_ref[b, rows, :]
        zp = jnp.where(row0, zlast_ref[b, 0:1, :], pltpu.roll(z, 1, 0))
        zlast_ref[b, 0:1, :] = z[C - 1:C, :]
        z = z + mu * (zp - z)
        r = z[:, 0:W]
        k = z[:, W:2 * W]
        v = z[:, 2 * W:3 * W]
        m0 = z[:, 3 * W:3 * W + 128]
        m12 = z[:, 3 * W + 128:3 * W + 384]
        w_log = -_softplus(-(w0 + _dot_split(jnp.tanh(m0), wdec_hi, wdec_lo))) - 0.5
        lw = -jnp.exp(w_log)
        a = _sigmoid(a0 + _dot_split(m0, wiclr_hi, wiclr_lo))
        g = _dot(_sigmoid(m12).astype(BF16), wgate)
        kk = k * k_k
        kkn = kk * lax.rsqrt(jnp.maximum(segsum(kk * kk), 1e-24))
        k2 = k * (1.0 + (a - 1.0) * k_a)
        beta = kkn * a
        cs = _dot_01(tri, lw)
        c_last = cs[C - 1:C, :]
        dec_in = jnp.exp(-cs)
        dec_out = jnp.exp(c_last - cs)
        kt = kkn * jnp.exp(cs - lw)
        rt = r * jnp.exp(cs)
        bh = beta * dec_in
        kh = k2 * dec_in
        kbar = k2 * dec_out
        bbar = beta * dec_out
        gam = jnp.exp(c_last)
        chains = [dict(b=b, i=i, kt=quad(kt, i), rt=quad(rt, i), v=quad(v, i), bh=quad(bh, i),
                       kh=quad(kh, i), kbar=quad(kbar, i), bbar=quad(bbar, i), gam=quad(gam, i))
                  for i in range(RWKV_NQ)]
        return dict(r=r, k2=k2, v=v, g=g), chains

    def epilogue(b, rows, tok, y):
        mean = segsum(y) * (1.0 / RWKV_N)
        d = y - mean
        var = segsum(d * d) * (1.0 / RWKV_N)
        yn = d * lax.rsqrt(var + RWKV_GN_EPS) * ln_w + ln_b
        bonus = segsum(tok["r"] * tok["k2"] * r_k) * tok["v"]
        o_ref[b, rows, :] = ((yn + bonus) * tok["g"]).astype(o_ref.dtype)

    def chunk(c, carry):
        groups, chains = [], []
        for sub in range(RWKV_CPI):
            rows = pl.ds(pl.multiple_of((c * RWKV_CPI + sub) * C, C), C)
            for b in range(n_batch):
                tok, ch = chunk_one(b, rows)
                groups.append((b, rows, tok, ch))
                chains += ch
        for ch in chains:
            lhs = jnp.concatenate([ch["kt"], ch["rt"]], axis=0).astype(BF16)
            rhs = jnp.concatenate([bd(ch["bh"]), bd(ch["kh"])], axis=0)
            sc = _dot_nt(lhs, rhs)
            ch["a_kk"] = jnp.where(strict, sc[:C, Q:], 0.0)
            ch["a_rb"] = jnp.where(incl, sc[C:, :Q], 0.0)
            ch["a_rk"] = jnp.where(incl, sc[C:, Q:], 0.0)
            ch["n"] = -jnp.where(strict, sc[:C, :Q], 0.0)
            ch["t"] = eye_wide + ch["n"]
        for ch in chains:
            ch["p"] = _dot(ch["n"].astype(BF16), bd(ch["n"]))
            ch["bd_v"] = bd(ch["v"])
            ch["akk_v"] = _dot(ch["a_kk"].astype(BF16), ch["bd_v"])
        for _ in range(4):
            for ch in chains:
                tp = _dot(jnp.concatenate([ch["t"], ch["p"]], axis=0).astype(BF16), bd(ch["p"]))
                ch["t"] = ch["t"] + tp[:C]
                ch["p"] = tp[C:]
        for ch in chains:
            ch["t"] = ch["t"] + _dot(ch["t"].astype(BF16), bd(ch["p"]))
        for ch in chains:
            w12 = _dot(ch["t"].astype(BF16),
                       jnp.concatenate([bd(ch["akk_v"]), bd(ch["kt"])], axis=1))
            ch["w1"], ch["w2"] = w12[:, :Q], w12[:, Q:]
        for ch in chains:
            w1, w2, bbar = ch["w1"], ch["w2"], ch["bbar"]
            ch["y0"] = _dot(jnp.concatenate([ch["a_rk"], ch["a_rb"]], axis=1).astype(BF16),
                            jnp.concatenate([ch["bd_v"], bd(-w1)], axis=0))
            ch["rp"] = (ch["rt"] - _dot(ch["a_rb"].astype(BF16), bd(w2))).astype(BF16)
            ch["p_bd"] = (jnp.where(eye_q, ch["gam"], 0.0) - jnp.where(
                head_blk, _dot_tn(bbar.astype(BF16), w2.astype(BF16)), 0.0)).astype(BF16)
            ch["q_bd"] = jnp.where(
                head_blk,
                _dot_tn(jnp.concatenate([ch["kbar"], -bbar], axis=0).astype(BF16),
                        jnp.concatenate([ch["v"], w1], axis=0).astype(BF16)), 0.0)
        for b, rows, tok, chs in groups:
            ys = []
            for ch in chs:
                hb = h_ref[b, ch["i"]].astype(BF16)
                ys.append(_dot(ch["rp"], hb) + ch["y0"])
                h_ref[b, ch["i"]] = _dot(ch["p_bd"], hb) + ch["q_bd"]
            epilogue(b, rows, tok, jnp.concatenate(ys, axis=1))
        return carry

    lax.fori_loop(0, n_chunks // RWKV_CPI, chunk, 0)


def _rwkv(zc, p, l, B, S, ts):
    ns = S // ts
    W, Q = RWKV_W, RWKV_QUAD
    split_w = pl.BlockSpec((None, 2, 128, W), lambda *_: (l, 0, 0, 0))
    out = pl.pallas_call(
        functools.partial(_rwkv_body, n_chunks=ts // CHUNK, n_batch=B),
        grid=(ns,),
        in_specs=[
            pl.BlockSpec((B, ts, RWKV_COLS_PAD), lambda s: (0, s, 0)),
            _vec_spec(RWKV_COLS_PAD, l), _vec_spec(W, l), split_w,
            _vec_spec(W, l), split_w, _mat_spec(256, W, l),
            _vec_spec(W, l), _vec_spec(W, l), _vec_spec(W, l), _vec_spec(W, l), _vec_spec(W, l),
        ],
        out_specs=pl.BlockSpec((B, ts, W), lambda s: (0, s, 0)),
        out_shape=jax.ShapeDtypeStruct((B, S, W), BF16),
        scratch_shapes=[pltpu.VMEM((B, RWKV_NQ, Q, Q), F32),
                        pltpu.VMEM((B, 8, RWKV_COLS_PAD), F32)],
        compiler_params=_cparams(1),
        name="rwkv",
    )(zc.reshape(B, S, RWKV_COLS_PAD), p["mu"], p["w0"], p["wdec"], p["a0"], p["wiclr"],
      p["wgate"], p["k_k"], p["k_a"], p["r_k"], p["ln_w"], p["ln_b"])
    return out.reshape(B * S, W)


def _merge_body(x_ref, oa_ref, ob_ref, oc_ref, gpre_ref, wd_ref, wa_ref, wb_ref, wc_ref, wo_ref,
                gpost_ref, o_ref):
    x = x_ref[...]
    D = D_MODEL
    h = _rms(x, gpre_ref[...]).astype(BF16)
    merged = None
    for j, (o_r, w_r) in enumerate(((oa_ref, wa_ref), (ob_ref, wb_ref), (oc_ref, wc_ref))):
        gate = _sigmoid(_dot(h, wd_ref[:, j * D:(j + 1) * D]))
        term = gate * _dot(o_r[...], w_r[...])
        merged = term if merged is None else merged + term
    y = _dot(merged.astype(BF16), wo_ref[...])
    o_ref[...] = x + _rms(y, gpost_ref[...])


def _merge(x, oa, ob, oc, wd, wa, wb, wc, wo, ng, l, tm):
    T, D = x.shape
    tok = lambda n: pl.BlockSpec((tm, n), lambda i: (i, 0))
    res = lambda r, c: pl.BlockSpec((None, r, c), lambda i: (l, 0, 0), pipeline_mode=pl.Buffered(1))
    gain = lambda j: pl.BlockSpec((None, 1, D), lambda i: (l * 8 + j, 0, 0))
    return pl.pallas_call(
        _merge_body,
        grid=(T // tm,),
        in_specs=[tok(D), tok(MLA_W), tok(GLA_W), tok(RWKV_W), gain(2), res(D, N_BRANCH * D),
                  res(MLA_W, D), res(GLA_W, D), res(RWKV_W, D), res(D, D), gain(3)],
        out_specs=tok(D),
        out_shape=jax.ShapeDtypeStruct((T, D), F32),
        compiler_params=_cparams(1),
        name="merge",
    )(x, oa, ob, oc, ng, wd, wa, wb, wc, wo, ng)


def _mem_kv_body(mem_ref, g_ref, w_ref, o_ref):
    o_ref[...] = _dot(_rms(mem_ref[...], g_ref[...]).astype(BF16), w_ref[...]).astype(o_ref.dtype)


def _mem_kv(mem, mem_norm, wkv, l):
    B, M, D = mem.shape
    return pl.pallas_call(
        _mem_kv_body,
        grid=(B,),
        in_specs=[pl.BlockSpec((None, M, D), lambda b: (b, 0, 0)), _vec_spec(D, l),
                  _mat_spec(D, 2 * D, l)],
        out_specs=pl.BlockSpec((None, M, 2 * D), lambda b: (b, 0, 0)),
        out_shape=jax.ShapeDtypeStruct((B, M, 2 * D), BF16),
        compiler_params=_cparams(1),
        name="mem_kv",
    )(mem, mem_norm, wkv)


def _mem_attn_body(x_ref, gpre_ref, wq_ref, kv_ref, wo_ref, gpost_ref, o_ref):
    x = x_ref[...]
    D = D_MODEL
    h = _rms(x, gpre_ref[...]).astype(BF16)
    q = (_dot(h, wq_ref[...]) * MEM_HD ** -0.5).astype(BF16)
    kv = kv_ref[...]
    outs = []
    for hh in range(MEM_HEADS):
        sl = slice(hh * MEM_HD, (hh + 1) * MEM_HD)
        s = _dot_nt(q[:, sl], kv[:, sl])
        p = jnp.exp(s - jnp.max(s, axis=-1, keepdims=True))
        o = _dot(p.astype(BF16), kv[:, D + hh * MEM_HD:D + (hh + 1) * MEM_HD])
        outs.append(o / jnp.sum(p, axis=-1, keepdims=True))
    o = jnp.concatenate(outs, axis=1).astype(BF16)
    o_ref[...] = x + _rms(_dot(o, wo_ref[...]), gpost_ref[...])


def _mem_attn(x, ng, wq, kv, wo, l, S, tm):
    T, D = x.shape
    M = kv.shape[1]
    per_b = S // tm
    return pl.pallas_call(
        _mem_attn_body,
        grid=(T // tm,),
        in_specs=[pl.BlockSpec((tm, D), lambda i: (i, 0)),
                  pl.BlockSpec((None, 1, D), lambda i: (l * 8 + 4, 0, 0)),
                  _mat_spec(D, D, l),
                  pl.BlockSpec((None, M, 2 * D), lambda i: (i // per_b, 0, 0)),
                  _mat_spec(D, D, l),
                  pl.BlockSpec((None, 1, D), lambda i: (l * 8 + 5, 0, 0))],
        out_specs=pl.BlockSpec((tm, D), lambda i: (i, 0)),
        out_shape=jax.ShapeDtypeStruct((T, D), F32),
        compiler_params=_cparams(1),
        name="mem_attn",
    )(x, ng, wq, kv, wo, ng)


def _prepare_params(w_in, mla_w_uq, mla_w_ukv, gla_w_alpha, gla_norm, rwkv_mu, rwkv_w_decay,
                    rwkv_w_iclr, rwkv_w_gate, w_branch):
    L, D = w_in.shape[0], w_in.shape[1]
    zc = lambda n: jnp.zeros((L, D, n), F32)
    o = 0
    cuts = {}
    for name, n in (("c_q", 256), ("c_kv", 128), ("k_rope", 32), ("gla_q", 256), ("gla_k", 256),
                    ("gla_v", 512), ("gla_g", 512), ("gla_a", 16), ("rwkv", 1824), ("gates", 3072)):
        cuts[name] = w_in[:, :, o:o + n]
        o += n
    kr = cuts["k_rope"]
    kr_b = jnp.concatenate([-kr[..., 16:], kr[..., :16]], axis=-1)
    wa = jnp.concatenate([cuts["c_q"], cuts["c_kv"], zc(64), kr, zc(32), zc(64), kr_b, zc(32)], -1)
    wb = jnp.concatenate([cuts["gla_v"], cuts["gla_g"], cuts["gla_q"], cuts["gla_k"],
                          cuts["gla_a"], zc(112)], -1)
    wc = jnp.concatenate([cuts["rwkv"], zc(RWKV_MISC - 288)], -1)
    wd = cuts["gates"]

    wuq = mla_w_uq.reshape(L, MLA_Q_RANK, MLA_HEADS, MLA_NOPE + MLA_ROPE)
    nope, rope = wuq[..., :MLA_NOPE], wuq[..., MLA_NOPE:]
    zq = lambda n: jnp.zeros((L, MLA_Q_RANK, MLA_HEADS, n), F32)
    wqa = jnp.concatenate([nope, rope, zq(32)], -1).reshape(L, MLA_Q_RANK, MLA_W)
    rope_b = jnp.concatenate([-rope[..., 16:], rope[..., :16]], -1)
    wqb = jnp.concatenate([zq(64), rope_b, zq(32)], -1).reshape(L, MLA_Q_RANK, MLA_W)
    wukv = mla_w_ukv.reshape(L, MLA_KV_RANK, MLA_HEADS, 128)
    zk = jnp.zeros((L, MLA_KV_RANK, MLA_HEADS, 64), F32)
    wk = jnp.concatenate([wukv[..., :64], zk], -1).reshape(L, MLA_KV_RANK, MLA_W)
    wv = jnp.concatenate([wukv[..., 64:], zk[..., :MLA_VSLOT - 64]], -1).reshape(
        L, MLA_KV_RANK, MLA_HEADS * MLA_VSLOT)

    bra = w_branch[:, :512].reshape(L, MLA_HEADS, 64, D)
    bra = jnp.concatenate([bra, jnp.zeros_like(bra)], axis=2).reshape(L, MLA_W, D)
    brb = w_branch[:, 512:512 + GLA_W]
    brc = w_branch[:, 512 + GLA_W:]

    walpha = jnp.concatenate(
        [gla_w_alpha, jnp.zeros((L, 128 - GLA_GATE_RANK, GLA_QK), F32)], axis=1)
    gn = jnp.tile(gla_norm, (1, GLA_HEADS))[:, None, :]
    mu = jnp.concatenate([rwkv_mu, jnp.zeros((L, RWKV_MISC - 288), F32)], -1)[:, None, :]
    zr = lambda n: jnp.zeros((L, n, RWKV_W), F32)
    def hi_lo(w):
        hi = w.astype(BF16)
        return jnp.stack([hi, (w - hi.astype(F32)).astype(BF16)], axis=1)

    wdec = hi_lo(jnp.concatenate([rwkv_w_decay, zr(64)], axis=1))
    wiclr = hi_lo(jnp.concatenate([zr(64), rwkv_w_iclr], axis=1))
    wgate = jnp.concatenate([rwkv_w_gate, zr(256 - RWKV_GATE_RANK)], axis=1)
    bf = lambda w: w.astype(BF16)
    return dict(wa=bf(wa), wb=bf(wb), wc=bf(wc), wd=bf(wd), wqa=bf(wqa), wqb=bf(wqb), wk=bf(wk),
                wv=bf(wv), bra=bf(bra), brb=bf(brb), brc=bf(brc), walpha=hi_lo(walpha), gn=gn, mu=mu,
                wdec=wdec, wiclr=wiclr, wgate=bf(wgate))


def kernel(x, mem, positions, norm_g, w_ffn_in, w_ffn_out, w_in, mla_q_norm, mla_w_uq, mla_kv_norm, mla_w_ukv, gla_w_alpha, gla_b_alpha, gla_norm, rwkv_mu, rwkv_w0, rwkv_w_decay, rwkv_a0, rwkv_w_iclr, rwkv_w_gate, rwkv_k_k, rwkv_k_a, rwkv_r_k, rwkv_ln_w, rwkv_ln_b, w_branch, w_out, mem_norm, mem_wq, mem_wkv, mem_wo):
    B, S, D = x.shape
    L = norm_g.shape[0]
    T = B * S
    tm = min(512, S)
    tq = min(1024, S)
    tk = tq // 2
    ts = min(256, S)
    tf = D_FF // 2

    pp = _prepare_params(w_in, mla_w_uq, mla_w_ukv, gla_w_alpha, gla_norm, rwkv_mu, rwkv_w_decay,
                         rwkv_w_iclr, rwkv_w_gate, w_branch)
    bf = lambda w: w.astype(BF16)
    ffn_in, ffn_out = bf(w_ffn_in), bf(w_ffn_out)
    wout, wq, wkv, wo = bf(w_out), bf(mem_wq), bf(mem_wkv), bf(mem_wo)
    ng = norm_g.reshape(L * 8, 1, D)
    row = lambda p: p[:, None, :]
    rw = dict(mu=pp["mu"], w0=row(rwkv_w0), wdec=pp["wdec"], a0=row(rwkv_a0), wiclr=pp["wiclr"],
              wgate=pp["wgate"], k_k=row(rwkv_k_k), k_a=row(rwkv_k_a), r_k=row(rwkv_r_k),
              ln_w=row(rwkv_ln_w), ln_b=row(rwkv_ln_b))

    inv_freq = ROPE_THETA ** (-jnp.arange(0, MLA_ROPE, 2, dtype=F32) / MLA_ROPE)
    invf_lane = jnp.concatenate(
        [jnp.zeros((MLA_NOPE,), F32), inv_freq, inv_freq, jnp.zeros((32,), F32)])[None, :]
    c0, s0 = _rope_tables(positions.astype(F32).reshape(T, 1), invf_lane, tm)

    x = x.reshape(T, D)
    for l in range(L):
        x = _ffn(x, ng, ffn_in, ffn_out, l, 0, tm, tf)
        za, zb, zc = _normproj(x, ng, l * 8 + 2, [pp["wa"], pp["wb"], pp["wc"]], l, tm)
        q, k, vt = _mla_prep(za, c0, s0, row(mla_q_norm), row(mla_kv_norm), pp["wqa"], pp["wqb"],
                             pp["wk"], pp["wv"], l, tk)
        o_a = _mla_attn(q, k, vt, B, S, tq, tk)
        o_b = _gla(zb, pp["walpha"], row(gla_b_alpha), pp["gn"], l, B, S, ts)
        o_c = _rwkv(zc, rw, l, B, S, ts)
        x = _merge(x, o_a, o_b, o_c, pp["wd"], pp["bra"], pp["brb"], pp["brc"], wout, ng, l, tm)
        kv = _mem_kv(mem, row(mem_norm), wkv, l)
        x = _mem_attn(x, ng, wq, kv, wo, l, S, tm)
        x = _ffn(x, ng, ffn_in, ffn_out, l, 1, tm, tf)
    return x.reshape(B, S, D)
```

```python
import functools

import jax
import jax.numpy as jnp
import numpy as np
from jax import lax
from jax.experimental import pallas as pl
from jax.experimental.pallas import tpu as pltpu

F32 = jnp.float32
BF16 = jnp.bfloat16
HI = lax.Precision.HIGHEST

D_MODEL = 1024
D_FF = 2816
NORM_EPS = 1e-6
MLA_HEADS = 8
MLA_NOPE = 64
MLA_ROPE = 32
MLA_Q_RANK = 256
MLA_KV_RANK = 128
ROPE_THETA = 10000.0
MLA_SLOT = 128
MLA_W = MLA_HEADS * MLA_SLOT
MLA_HPS = 2
MLA_VSLOT = 80
MLA_ONE = 64
GLA_HEADS = 4
GLA_DK = 64
GLA_DV = 128
GLA_GATE_RANK = 16
GLA_GATE_NORM = 16.0
GLA_QK = GLA_HEADS * GLA_DK
GLA_W = GLA_HEADS * GLA_DV
GLA_LEVELS = 6
GLA_NEG = -1e30
RWKV_HEADS = 8
RWKV_N = 64
RWKV_DECAY_RANK = 64
RWKV_ICLR_RANK = 64
RWKV_GATE_RANK = 160
RWKV_GN_EPS = 64e-5
RWKV_W = RWKV_HEADS * RWKV_N
RWKV_MISC = 384
RWKV_COLS_PAD = 3 * RWKV_W + RWKV_MISC
RWKV_QUAD = 4 * RWKV_N
RWKV_NQ = RWKV_W // RWKV_QUAD
REC_CPI = 2
MEM_HEADS = 4
MEM_HD = D_MODEL // MEM_HEADS
N_BRANCH = 3
CHUNK = 64

VMEM_LIMIT_BYTES = 56 * 1024 * 1024


def _cparams(n_axes, flags=None):
    return pltpu.CompilerParams(
        dimension_semantics=("arbitrary",) * n_axes,
        vmem_limit_bytes=VMEM_LIMIT_BYTES,
        flags=flags,
    )


def _dot(a, b, precision=None):
    return jnp.dot(a, b, preferred_element_type=F32, precision=precision)


def _dot_nt(a, b, precision=None):
    return lax.dot_general(a, b, (((1,), (1,)), ((), ())),
                           preferred_element_type=F32, precision=precision)


def _dot_tn(a, b, precision=None):
    return lax.dot_general(a, b, (((0,), (0,)), ((), ())),
                           preferred_element_type=F32, precision=precision)


def _rms(x, g, eps=NORM_EPS):
    return x * lax.rsqrt(jnp.mean(x * x, axis=-1, keepdims=True) + eps) * g


def _sigmoid(x):
    return 1.0 / (1.0 + jnp.exp(-x))


def _softplus(x):
    return jnp.maximum(x, 0.0) + jnp.log(1.0 + jnp.exp(-jnp.abs(x)))


def _split2(x):
    hi = x.astype(BF16)
    return hi, (x - hi.astype(F32)).astype(BF16)


def _dot_split(x, w_hi, w_lo):
    hi, lo = _split2(x)
    m = x.shape[0]
    top = _dot(jnp.concatenate([hi, lo], axis=0), w_hi)
    return top[:m] + top[m:] + _dot(hi, w_lo)


def _dot_01(m01_bf16, x):
    p1 = x.astype(BF16)
    r1 = x - p1.astype(F32)
    p2 = r1.astype(BF16)
    p3 = (r1 - p2.astype(F32)).astype(BF16)
    n = x.shape[1]
    out = _dot(m01_bf16, jnp.concatenate([p1, p2, p3], axis=1))
    return out[:, :n] + out[:, n:2 * n] + out[:, 2 * n:]


def _iota(shape, dim):
    return lax.broadcasted_iota(jnp.int32, shape, dim)


def _segsum(x, e_bf16):
    hi = x.astype(BF16)
    lo = (x - hi.astype(F32)).astype(BF16)
    return _dot(hi, e_bf16) + _dot(lo, e_bf16)


def _vec_spec(n, l):
    return pl.BlockSpec((None, 1, n), lambda *_: (l, 0, 0))


def _mat_spec(r, c, l):
    return pl.BlockSpec((None, r, c), lambda *_: (l, 0, 0))


def _ffn_body(x_ref, gpre_ref, wg_ref, wu_ref, wo_ref, gpost_ref, o_ref, h_ref, acc_ref):
    j = pl.program_id(1)

    @pl.when(j == 0)
    def _():
        h_ref[...] = _rms(x_ref[...], gpre_ref[...]).astype(BF16)
        acc_ref[...] = jnp.zeros_like(acc_ref)

    h = h_ref[...]
    g = _dot(h, wg_ref[...])
    u = _dot(h, wu_ref[...])
    act = (g * _sigmoid(g) * u).astype(BF16)
    acc_ref[...] += _dot(act, wo_ref[...])

    @pl.when(j == pl.num_programs(1) - 1)
    def _():
        o_ref[...] = x_ref[...] + 0.5 * _rms(acc_ref[...], gpost_ref[...])


def _ffn(x, ng, w_in, w_out, l, k, tm, tf):
    T, D = x.shape
    ff = w_out.shape[2]
    nf = ff // tf
    g_pre, g_post = l * 8 + 6 * k, l * 8 + 6 * k + 1
    return pl.pallas_call(
        _ffn_body,
        grid=(T // tm, nf),
        in_specs=[
            pl.BlockSpec((tm, D), lambda i, j: (i, 0)),
            pl.BlockSpec((None, 1, D), lambda i, j: (g_pre, 0, 0)),
            pl.BlockSpec((None, None, D, tf), lambda i, j: (l, k, 0, j)),
            pl.BlockSpec((None, None, D, tf), lambda i, j: (l, k, 0, j + nf)),
            pl.BlockSpec((None, None, tf, D), lambda i, j: (l, k, j, 0)),
            pl.BlockSpec((None, 1, D), lambda i, j: (g_post, 0, 0)),
        ],
        out_specs=pl.BlockSpec((tm, D), lambda i, j: (i, 0)),
        out_shape=jax.ShapeDtypeStruct((T, D), F32),
        scratch_shapes=[pltpu.VMEM((tm, D), BF16), pltpu.VMEM((tm, D), F32)],
        compiler_params=_cparams(2),
        name="ffn",
    )(x, ng, w_in, w_in, w_out, ng)


def _normproj_body(x_ref, g_ref, *refs):
    n = len(refs) // 2
    h = _rms(x_ref[...], g_ref[...]).astype(BF16)
    for w_ref, o_ref in zip(refs[:n], refs[n:]):
        o_ref[...] = _dot(h, w_ref[...]).astype(o_ref.dtype)


def _normproj(x, ng, g_idx, ws, l, tm):
    T, D = x.shape
    widths = [w.shape[2] for w in ws]
    return pl.pallas_call(
        _normproj_body,
        grid=(T // tm,),
        in_specs=[pl.BlockSpec((tm, D), lambda i: (i, 0)),
                  pl.BlockSpec((None, 1, D), lambda i: (g_idx, 0, 0))]
        + [pl.BlockSpec((None, D, n), lambda i: (l, 0, 0), pipeline_mode=pl.Buffered(1))
           for n in widths],
        out_specs=[pl.BlockSpec((tm, n), lambda i: (i, 0)) for n in widths],
        out_shape=[jax.ShapeDtypeStruct((T, n), F32) for n in widths],
        compiler_params=_cparams(1),
        name="mixer_proj",
    )(x, ng, *ws)


def _rope_table_body(pos_ref, invf_ref, c_ref, s_ref):
    ang = pos_ref[...] * invf_ref[...]
    lane = _iota(ang.shape, 1)
    rot = (lane >= MLA_NOPE) & (lane < MLA_NOPE + MLA_ROPE)
    c_ref[...] = jnp.where(lane < MLA_NOPE, 1.0, jnp.where(rot, jnp.cos(ang), 0.0))
    s_ref[...] = jnp.where(rot, jnp.sin(ang), 0.0)


def _rope_tables(pos_f32, invf_lane, tm):
    T = pos_f32.shape[0]
    return pl.pallas_call(
        _rope_table_body,
        grid=(T // tm,),
        in_specs=[pl.BlockSpec((tm, 1), lambda i: (i, 0)),
                  pl.BlockSpec((1, MLA_SLOT), lambda i: (0, 0))],
        out_specs=[pl.BlockSpec((tm, MLA_SLOT), lambda i: (i, 0))] * 2,
        out_shape=[jax.ShapeDtypeStruct((T, MLA_SLOT), F32)] * 2,
        compiler_params=_cparams(1),
        name="rope_tables",
    )(pos_f32, invf_lane)


def _mla_prep_body(za_ref, c0_ref, s0_ref, qn_ref, kvn_ref, wqa_ref, wqb_ref, wk_ref, wv_ref,
                   one_ref, q_ref, k_ref, vt_ref):
    za = za_ref[...]
    cq = za[:, :MLA_Q_RANK]
    ckv = za[:, MLA_Q_RANK:MLA_Q_RANK + MLA_KV_RANK]
    kra = za[:, 384:512]
    krb = za[:, 512:640]
    cqn = _rms(cq, qn_ref[...]).astype(BF16)
    ckvn = _rms(ckv, kvn_ref[...]).astype(BF16)
    c0 = c0_ref[...]
    s0 = s0_ref[...]
    c8 = jnp.concatenate([c0] * MLA_HEADS, axis=1)
    s8 = jnp.concatenate([s0] * MLA_HEADS, axis=1)
    scale = (MLA_NOPE + MLA_ROPE) ** -0.5
    q = (_dot(cqn, wqa_ref[...]) * c8 + _dot(cqn, wqb_ref[...]) * s8) * scale
    q_ref[...] = q.astype(BF16)
    krot = kra * c0 + krb * s0
    k = _dot(ckvn, wk_ref[...]) + jnp.concatenate([krot] * MLA_HEADS, axis=1)
    k_ref[...] = k.astype(BF16)
    v = _dot(ckvn, wv_ref[...]) + one_ref[...]
    vt_ref[...] = v.T.astype(BF16)


def _mla_prep(za, c0, s0, qn, kvn, wqa, wqb, wk, wv, l, tk):
    T = za.shape[0]
    vw = MLA_HEADS * MLA_VSLOT
    tok = lambda n: pl.BlockSpec((tk, n), lambda i: (i, 0))
    one_lane = jnp.asarray((np.arange(vw) % MLA_VSLOT == MLA_ONE).astype(np.float32))[None, :]
    return pl.pallas_call(
        _mla_prep_body,
        grid=(T // tk,),
        in_specs=[tok(za.shape[1]), tok(MLA_SLOT), tok(MLA_SLOT),
                  _vec_spec(MLA_Q_RANK, l), _vec_spec(MLA_KV_RANK, l),
                  _mat_spec(MLA_Q_RANK, MLA_W, l), _mat_spec(MLA_Q_RANK, MLA_W, l),
                  _mat_spec(MLA_KV_RANK, MLA_W, l), _mat_spec(MLA_KV_RANK, vw, l),
                  pl.BlockSpec((1, vw), lambda i: (0, 0))],
        out_specs=[tok(MLA_W), tok(MLA_W),
                   pl.BlockSpec((None, vw, tk), lambda i: (i, 0, 0))],
        out_shape=[jax.ShapeDtypeStruct((T, MLA_W), BF16)] * 2
        + [jax.ShapeDtypeStruct((T // tk, vw, tk), BF16)],
        compiler_params=_cparams(1),
        name="mla_prep",
    )(za, c0, s0, qn, kvn, wqa, wqb, wk, wv, one_lane)


def _mla_attn_body(q_ref, k_ref, vt_ref, o_ref, s_ref, m_ref, acc_ref, *, tq, tk):
    qi = pl.program_id(2)
    m_ref[...] = jnp.full_like(m_ref, -1e30)
    acc_ref[...] = jnp.zeros_like(acc_ref)
    heads = [slice(hh * MLA_SLOT, (hh + 1) * MLA_SLOT) for hh in range(MLA_HPS)]

    def produce(slot, ki):
        r0 = pl.multiple_of(ki * tk, tk)
        for hh, sl in enumerate(heads):
            s_ref[slot, hh] = _dot_nt(k_ref[pl.ds(r0, tk), sl], q_ref[:, sl])

    def consume(slot, ki, diag):
        for hh, sl in enumerate(heads):
            st = s_ref[slot, hh]
            if diag is not None:
                key = _iota(st.shape, 0) + diag * tk
                st = jnp.where(key <= _iota(st.shape, 1), st, -1e30)
            m_old = m_ref[hh]
            m_new = jnp.maximum(m_old, jnp.max(st, axis=0, keepdims=True))
            p = jnp.exp(st - m_new).astype(BF16)
            vt = vt_ref[ki, hh * MLA_VSLOT:(hh + 1) * MLA_VSLOT, :]
            acc_ref[hh] = jnp.exp(m_old - m_new) * acc_ref[hh] + _dot(vt, p)
            m_ref[hh] = m_new

    def body(j, carry):
        produce(1, 2 * j + 1)
        consume(0, 2 * j, None)
        produce(0, 2 * j + 2)
        consume(1, 2 * j + 1, None)
        return carry

    produce(0, 0)
    lax.fori_loop(0, qi, body, 0)
    produce(1, 2 * qi + 1)
    consume(0, 2 * qi, 0)
    consume(1, 2 * qi + 1, 1)
    for hh in range(MLA_HPS):
        acc = acc_ref[hh]
        o_t = acc / acc[MLA_ONE:MLA_ONE + 1, :]
        o_t = jnp.concatenate([o_t, jnp.zeros((MLA_SLOT - MLA_VSLOT, tq), F32)], axis=0)
        o_ref[:, hh * MLA_SLOT:(hh + 1) * MLA_SLOT] = o_t.T.astype(o_ref.dtype)


def _mla_attn(q, k, vt, B, S, tq, tk):
    assert tq == 2 * tk
    T = q.shape[0]
    nq, nk = S // tq, S // tk
    w = MLA_HPS * MLA_SLOT
    return pl.pallas_call(
        functools.partial(_mla_attn_body, tq=tq, tk=tk),
        grid=(B, MLA_HEADS // MLA_HPS, nq),
        in_specs=[
            pl.BlockSpec((tq, w), lambda b, h, i: (b * nq + i, h)),
            pl.BlockSpec((S, w), lambda b, h, i: (b, h)),
            pl.BlockSpec((nk, MLA_HPS * MLA_VSLOT, tk), lambda b, h, i: (b, h, 0)),
        ],
        out_specs=pl.BlockSpec((tq, w), lambda b, h, i: (b * nq + i, h)),
        out_shape=jax.ShapeDtypeStruct((T, MLA_W), BF16),
        scratch_shapes=[pltpu.VMEM((2, MLA_HPS, tk, tq), F32),
                        pltpu.VMEM((MLA_HPS, 1, tq), F32),
                        pltpu.VMEM((MLA_HPS, MLA_VSLOT, tq), F32)],
        compiler_params=_cparams(3),
        name="mla_attn",
    )(q, k, vt)


def _gla_pair_masks():
    i = np.arange(GLA_HEADS * CHUNK)[:, None] % CHUNK
    j = np.arange(CHUNK)[None, :]
    same = [(i >> (6 - lv)) == (j >> (6 - lv)) for lv in range(GLA_LEVELS)]
    return jnp.asarray(np.stack(same + [i == j]).astype(np.float32))


def _gla_stages(v_ref, g_ref, q_ref, k_ref, a_ref, walpha_ref, balpha_ref, gn_ref, pair_ref,
                o_ref, ht_ref, n_batch):
    C = CHUNK
    walpha_hi, walpha_lo = walpha_ref[0], walpha_ref[1]
    balpha = balpha_ref[...]
    gn = gn_ref[...]
    tri = (_iota((C, C), 1) <= _iota((C, C), 0)).astype(BF16)
    st_r, st_c = _iota((GLA_HEADS * C, GLA_QK), 0), _iota((GLA_HEADS * C, GLA_QK), 1)
    head_mask = ((st_r >> 6) == (st_c >> 6)).astype(BF16)
    ht_r, ht_c = _iota((GLA_W, GLA_QK), 0), _iota((GLA_W, GLA_QK), 1)
    state_mask = (ht_r >> 7) == (ht_c >> 6)
    tok = _iota((C, GLA_QK), 0)
    halves = [C >> (lv + 1) for lv in range(GLA_LEVELS)]

    def stack(x):
        return jnp.concatenate([x.astype(BF16)] * GLA_HEADS, axis=0) * head_mask

    def level_operands(q, k, cb):
        last = {1: cb}
        for s in halves[:0:-1]:
            f = last[s]
            last[2 * s] = jnp.where((tok & s) != 0, f, pltpu.roll(f, C - s, 0))
        ops = []
        for hs in halves:
            f = last[hs]
            right = (tok & hs) != 0
            d = cb - jnp.where(right, pltpu.roll(f, hs, 0), f)
            ql = q * jnp.exp(jnp.where(right, d, GLA_NEG))
            kl = k * jnp.exp(jnp.where(right, GLA_NEG, -d))
            ops.append((stack(ql), kl.astype(BF16)))
        ops.append((stack(q), k.astype(BF16)))
        return ops

    def stages(c):
        cbs = [(b, pl.ds(pl.multiple_of((c * REC_CPI + sub) * C, C), C))
               for sub in range(REC_CPI) for b in range(n_batch)]
        xs = [_dot_split(a_ref[b, rows, :], walpha_hi, walpha_lo) + balpha for b, rows in cbs]
        yield
        cum = [_dot_01(tri, -_softplus(-x) * (1.0 / GLA_GATE_NORM)) for x in xs]
        yield
        st = []
        for (b, rows), cb in zip(cbs, cum):
            b_last = cb[C - 1:C, :]
            q = q_ref[b, rows, :] * GLA_DK ** -0.5
            k = k_ref[b, rows, :]
            st.append(dict(
                v=v_ref[b, rows, :].astype(BF16),
                qe=(q * jnp.exp(cb)).astype(BF16),
                ke=(k * jnp.exp(b_last - cb)).astype(BF16),
                gam=jnp.exp(b_last),
                ops=level_operands(q, k, cb)))
        scs = [None] * len(cbs)
        for lv in range(GLA_LEVELS + 1):
            for i, s in enumerate(st):
                ql, kl = s["ops"][lv]
                part = _dot_nt(ql, kl) * pair_ref[lv]
                scs[i] = part if lv == 0 else scs[i] + part
            yield
        osts = [_dot(sc.astype(BF16), s["v"]) for sc, s in zip(scs, st)]
        yield
        for (b, rows), s, ost in zip(cbs, st, osts):
            o_intra = jnp.concatenate(
                [ost[h * C:(h + 1) * C, h * GLA_DV:(h + 1) * GLA_DV] for h in range(GLA_HEADS)],
                axis=1)
            ht = ht_ref[b]
            o = o_intra + _dot_nt(s["qe"], ht.astype(BF16))
            ht_ref[b] = ht * s["gam"] + jnp.where(state_mask, _dot_tn(s["v"], s["ke"]), 0.0)
            outs = []
            for h in range(GLA_HEADS):
                oh = o[:, h * GLA_DV:(h + 1) * GLA_DV]
                outs.append(oh * lax.rsqrt(jnp.mean(oh * oh, axis=-1, keepdims=True) + NORM_EPS))
            on = jnp.concatenate(outs, axis=1) * gn
            g = g_ref[b, rows, :]
            o_ref[b, rows, :] = (on * (g * _sigmoid(g))).astype(o_ref.dtype)
            yield

    return stages


def _rwkv_stages(z_ref, mu_ref, w0_ref, wdec_ref, a0_ref, wiclr_ref, wgate_ref, kk_ref, ka_ref,
                 rk_ref, lnw_ref, lnb_ref, o_ref, h_ref, zlast_ref, n_batch):
    C, W, Q = CHUNK, RWKV_W, RWKV_QUAD
    mu = mu_ref[...]
    w0, a0 = w0_ref[...], a0_ref[...]
    wdec_hi, wdec_lo = wdec_ref[0], wdec_ref[1]
    wiclr_hi, wiclr_lo = wiclr_ref[0], wiclr_ref[1]
    wgate = wgate_ref[...]
    k_k, k_a, r_k = kk_ref[...], ka_ref[...], rk_ref[...]
    ln_w, ln_b = lnw_ref[...], lnb_ref[...]

    tri = (_iota((C, C), 1) <= _iota((C, C), 0)).astype(BF16)
    sq_r, sq_c = _iota((Q, Q), 0), _iota((Q, Q), 1)
    head_blk = (sq_r >> 6) == (sq_c >> 6)
    eye_q = sq_r == sq_c
    e_seg = head_blk.astype(BF16)
    wd_t, wd_s = _iota((C, Q), 0), _iota((C, Q), 1) & (C - 1)
    strict = wd_s < wd_t
    incl = wd_s <= wd_t
    eye_wide = (wd_s == wd_t).astype(F32)
    row0 = _iota((C, RWKV_COLS_PAD), 0) == 0

    def bd(x):
        return jnp.concatenate([x.astype(BF16)] * 4, axis=0) * e_seg

    def quad(x, i):
        return x[:, i * Q:(i + 1) * Q]

    def segsum(x):
        return jnp.concatenate([_segsum(quad(x, i), e_seg) for i in range(RWKV_NQ)], axis=1)

    def chunk_one(b, rows):
        z = z_ref[b, rows, :]
        zp = jnp.where(row0, zlast_ref[b, 0:1, :], pltpu.roll(z, 1, 0))
        zlast_ref[b, 0:1, :] = z[C - 1:C, :]
        z = z + mu * (zp - z)
        r = z[:, 0:W]
        k = z[:, W:2 * W]
        v = z[:, 2 * W:3 * W]
        m0 = z[:, 3 * W:3 * W + 128]
        m12 = z[:, 3 * W + 128:3 * W + 384]
        w_log = -_softplus(-(w0 + _dot_split(jnp.tanh(m0), wdec_hi, wdec_lo))) - 0.5
        lw = -jnp.exp(w_log)
        a = _sigmoid(a0 + _dot_split(m0, wiclr_hi, wiclr_lo))
        g = _dot(_sigmoid(m12).astype(BF16), wgate)
        kk = k * k_k
        kkn = kk * lax.rsqrt(jnp.maximum(segsum(kk * kk), 1e-24))
        k2 = k * (1.0 + (a - 1.0) * k_a)
        beta = kkn * a
        cs = _dot_01(tri, lw)
        c_last = cs[C - 1:C, :]
        dec_in = jnp.exp(-cs)
        dec_out = jnp.exp(c_last - cs)
        kt = kkn * jnp.exp(cs - lw)
        rt = r * jnp.exp(cs)
        bh = beta * dec_in
        kh = k2 * dec_in
        kbar = k2 * dec_out
        bbar = beta * dec_out
        gam = jnp.exp(c_last)
        chains = [dict(b=b, i=i, kt=quad(kt, i), rt=quad(rt, i), v=quad(v, i), bh=quad(bh, i),
                       kh=quad(kh, i), kbar=quad(kbar, i), bbar=quad(bbar, i), gam=quad(gam, i))
                  for i in range(RWKV_NQ)]
        return dict(r=r, k2=k2, v=v, g=g), chains

    def epilogue(b, rows, tok, y):
        mean = segsum(y) * (1.0 / RWKV_N)
        d = y - mean
        var = segsum(d * d) * (1.0 / RWKV_N)
        yn = d * lax.rsqrt(var + RWKV_GN_EPS) * ln_w + ln_b
        bonus = segsum(tok["r"] * tok["k2"] * r_k) * tok["v"]
        o_ref[b, rows, :] = ((yn + bonus) * tok["g"]).astype(o_ref.dtype)

    def stages(c):
        groups, chains = [], []
        for sub in range(REC_CPI):
            rows = pl.ds(pl.multiple_of((c * REC_CPI + sub) * C, C), C)
            for b in range(n_batch):
                tok, ch = chunk_one(b, rows)
                groups.append((b, rows, tok, ch))
                chains += ch
                yield
        for ch in chains:
            lhs = jnp.concatenate([ch["kt"], ch["rt"]], axis=0).astype(BF16)
            rhs = jnp.concatenate([bd(ch["bh"]), bd(ch["kh"])], axis=0)
            sc = _dot_nt(lhs, rhs)
            ch["a_kk"] = jnp.where(strict, sc[:C, Q:], 0.0)
            ch["a_rb"] = jnp.where(incl, sc[C:, :Q], 0.0)
            ch["a_rk"] = jnp.where(incl, sc[C:, Q:], 0.0)
            ch["n"] = -jnp.where(strict, sc[:C, :Q], 0.0)
            ch["t"] = eye_wide + ch["n"]
        yield
        for ch in chains:
            ch["p"] = _dot(ch["n"].astype(BF16), bd(ch["n"]))
            ch["bd_v"] = bd(ch["v"])
            ch["akk_v"] = _dot(ch["a_kk"].astype(BF16), ch["bd_v"])
        yield
        for _ in range(4):
            for ch in chains:
                tp = _dot(jnp.concatenate([ch["t"], ch["p"]], axis=0).astype(BF16), bd(ch["p"]))
                ch["t"] = ch["t"] + tp[:C]
                ch["p"] = tp[C:]
            yield
        for ch in chains:
            ch["t"] = ch["t"] + _dot(ch["t"].astype(BF16), bd(ch["p"]))
        yield
        for ch in chains:
            w12 = _dot(ch["t"].astype(BF16),
                       jnp.concatenate([bd(ch["akk_v"]), bd(ch["kt"])], axis=1))
            ch["w1"], ch["w2"] = w12[:, :Q], w12[:, Q:]
        yield
        for ch in chains:
            w1, w2, bbar = ch["w1"], ch["w2"], ch["bbar"]
            ch["y0"] = _dot(jnp.concatenate([ch["a_rk"], ch["a_rb"]], axis=1).astype(BF16),
                            jnp.concatenate([ch["bd_v"], bd(-w1)], axis=0))
            ch["rp"] = (ch["rt"] - _dot(ch["a_rb"].astype(BF16), bd(w2))).astype(BF16)
            ch["p_bd"] = (jnp.where(eye_q, ch["gam"], 0.0) - jnp.where(
                head_blk, _dot_tn(bbar.astype(BF16), w2.astype(BF16)), 0.0)).astype(BF16)
            ch["q_bd"] = jnp.where(
                head_blk,
                _dot_tn(jnp.concatenate([ch["kbar"], -bbar], axis=0).astype(BF16),
                        jnp.concatenate([ch["v"], w1], axis=0).astype(BF16)), 0.0)
        yield
        for b, rows, tok, chs in groups:
            ys = []
            for ch in chs:
                hb = h_ref[b, ch["i"]].astype(BF16)
                ys.append(_dot(ch["rp"], hb) + ch["y0"])
                h_ref[b, ch["i"]] = _dot(ch["p_bd"], hb) + ch["q_bd"]
            epilogue(b, rows, tok, jnp.concatenate(ys, axis=1))
            yield

    return stages


N_GLA_IN, N_RWKV_IN = 9, 12


def _recur_body(*refs, n_chunks, n_batch):
    gla_in = refs[:N_GLA_IN]
    rwkv_in = refs[N_GLA_IN:N_GLA_IN + N_RWKV_IN]
    ob_ref, oc_ref, ht_ref, h_ref, zlast_ref = refs[N_GLA_IN + N_RWKV_IN:]

    @pl.when(pl.program_id(0) == 0)
    def _():
        ht_ref[...] = jnp.zeros_like(ht_ref)
        h_ref[...] = jnp.zeros_like(h_ref)
        zlast_ref[...] = jnp.zeros_like(zlast_ref)

    gla = _gla_stages(*gla_in, ob_ref, ht_ref, n_batch)
    rwkv = _rwkv_stages(*rwkv_in, oc_ref, h_ref, zlast_ref, n_batch)

    def chunk(c, carry):
        live = [rwkv(c), gla(c)]
        while live:
            live = [g for g in live if next(g, True) is None]
        return carry

    lax.fori_loop(0, n_chunks // REC_CPI, chunk, 0)


def _recur(zb, zc, gp, rp, l, B, S, ts):
    ns = S // ts
    W, Q = RWKV_W, RWKV_QUAD
    blk = lambda n, j: pl.BlockSpec((B, ts, n), lambda s: (0, s, j))
    split = lambda c: pl.BlockSpec((None, 2, 128, c), lambda s: (l, 0, 0, 0))
    zb = zb.reshape(B, S, zb.shape[1])
    o_b, o_c = pl.pallas_call(
        functools.partial(_recur_body, n_chunks=ts // CHUNK, n_batch=B),
        grid=(ns,),
        in_specs=[
            blk(GLA_W, 0),
            blk(GLA_W, 1),
            blk(GLA_QK, 4),
            blk(GLA_QK, 5),
            blk(128, 12),
            split(GLA_QK), _vec_spec(GLA_QK, l), _vec_spec(GLA_W, l),
            pl.BlockSpec((GLA_LEVELS + 1, GLA_HEADS * CHUNK, CHUNK), lambda s: (0, 0, 0)),
            blk(RWKV_COLS_PAD, 0),
            _vec_spec(RWKV_COLS_PAD, l), _vec_spec(W, l), split(W),
            _vec_spec(W, l), split(W), _mat_spec(256, W, l),
            _vec_spec(W, l), _vec_spec(W, l), _vec_spec(W, l), _vec_spec(W, l), _vec_spec(W, l),
        ],
        out_specs=[blk(GLA_W, 0), blk(W, 0)],
        out_shape=[jax.ShapeDtypeStruct((B, S, GLA_W), BF16), jax.ShapeDtypeStruct((B, S, W), BF16)],
        scratch_shapes=[pltpu.VMEM((B, GLA_W, GLA_QK), F32),
                        pltpu.VMEM((B, RWKV_NQ, Q, Q), F32),
                        pltpu.VMEM((B, 8, RWKV_COLS_PAD), F32)],
        compiler_params=_cparams(1),
        name="recurrences",
    )(zb, zb, zb, zb, zb, gp["walpha"], gp["balpha"], gp["gn"], _gla_pair_masks(),
      zc.reshape(B, S, RWKV_COLS_PAD), rp["mu"], rp["w0"], rp["wdec"], rp["a0"], rp["wiclr"],
      rp["wgate"], rp["k_k"], rp["k_a"], rp["r_k"], rp["ln_w"], rp["ln_b"])
    return o_b.reshape(B * S, GLA_W), o_c.reshape(B * S, W)


def _merge_body(x_ref, oa_ref, ob_ref, oc_ref, gpre_ref, wd_ref, wa_ref, wb_ref, wc_ref, wo_ref,
                gpost_ref, o_ref):
    x = x_ref[...]
    D = D_MODEL
    h = _rms(x, gpre_ref[...]).astype(BF16)
    merged = None
    for j, (o_r, w_r) in enumerate(((oa_ref, wa_ref), (ob_ref, wb_ref), (oc_ref, wc_ref))):
        gate = _sigmoid(_dot(h, wd_ref[:, j * D:(j + 1) * D]))
        term = gate * _dot(o_r[...], w_r[...])
        merged = term if merged is None else merged + term
    y = _dot(merged.astype(BF16), wo_ref[...])
    o_ref[...] = x + _rms(y, gpost_ref[...])


def _merge(x, oa, ob, oc, wd, wa, wb, wc, wo, ng, l, tm):
    T, D = x.shape
    tok = lambda n: pl.BlockSpec((tm, n), lambda i: (i, 0))
    res = lambda r, c: pl.BlockSpec((None, r, c), lambda i: (l, 0, 0), pipeline_mode=pl.Buffered(1))
    gain = lambda j: pl.BlockSpec((None, 1, D), lambda i: (l * 8 + j, 0, 0))
    return pl.pallas_call(
        _merge_body,
        grid=(T // tm,),
        in_specs=[tok(D), tok(MLA_W), tok(GLA_W), tok(RWKV_W), gain(2), res(D, N_BRANCH * D),
                  res(MLA_W, D), res(GLA_W, D), res(RWKV_W, D), res(D, D), gain(3)],
        out_specs=tok(D),
        out_shape=jax.ShapeDtypeStruct((T, D), F32),
        compiler_params=_cparams(1),
        name="merge",
    )(x, oa, ob, oc, ng, wd, wa, wb, wc, wo, ng)


def _mem_kv_body(mem_ref, g_ref, w_ref, o_ref):
    o_ref[...] = _dot(_rms(mem_ref[...], g_ref[...]).astype(BF16), w_ref[...]).astype(o_ref.dtype)


def _mem_kv(mem, mem_norm, wkv, l):
    B, M, D = mem.shape
    return pl.pallas_call(
        _mem_kv_body,
        grid=(B,),
        in_specs=[pl.BlockSpec((None, M, D), lambda b: (b, 0, 0)), _vec_spec(D, l),
                  _mat_spec(D, 2 * D, l)],
        out_specs=pl.BlockSpec((None, M, 2 * D), lambda b: (b, 0, 0)),
        out_shape=jax.ShapeDtypeStruct((B, M, 2 * D), BF16),
        compiler_params=_cparams(1),
        name="mem_kv",
    )(mem, mem_norm, wkv)


def _mem_attn_body(x_ref, gpre_ref, wq_ref, kv_ref, wo_ref, gpost_ref, o_ref):
    x = x_ref[...]
    D = D_MODEL
    h = _rms(x, gpre_ref[...]).astype(BF16)
    q = (_dot(h, wq_ref[...]) * MEM_HD ** -0.5).astype(BF16)
    kv = kv_ref[...]
    outs = []
    for hh in range(MEM_HEADS):
        sl = slice(hh * MEM_HD, (hh + 1) * MEM_HD)
        s = _dot_nt(q[:, sl], kv[:, sl])
        p = jnp.exp(s - jnp.max(s, axis=-1, keepdims=True))
        o = _dot(p.astype(BF16), kv[:, D + hh * MEM_HD:D + (hh + 1) * MEM_HD])
        outs.append(o / jnp.sum(p, axis=-1, keepdims=True))
    o = jnp.concatenate(outs, axis=1).astype(BF16)
    o_ref[...] = x + _rms(_dot(o, wo_ref[...]), gpost_ref[...])


def _mem_attn(x, ng, wq, kv, wo, l, S, tm):
    T, D = x.shape
    M = kv.shape[1]
    per_b = S // tm
    return pl.pallas_call(
        _mem_attn_body,
        grid=(T // tm,),
        in_specs=[pl.BlockSpec((tm, D), lambda i: (i, 0)),
                  pl.BlockSpec((None, 1, D), lambda i: (l * 8 + 4, 0, 0)),
                  _mat_spec(D, D, l),
                  pl.BlockSpec((None, M, 2 * D), lambda i: (i // per_b, 0, 0)),
                  _mat_spec(D, D, l),
                  pl.BlockSpec((None, 1, D), lambda i: (l * 8 + 5, 0, 0))],
        out_specs=pl.BlockSpec((tm, D), lambda i: (i, 0)),
        out_shape=jax.ShapeDtypeStruct((T, D), F32),
        compiler_params=_cparams(1),
        name="mem_attn",
    )(x, ng, wq, kv, wo, ng)


def _prepare_params(w_in, mla_w_uq, mla_w_ukv, gla_w_alpha, gla_norm, rwkv_mu, rwkv_w_decay,
                    rwkv_w_iclr, rwkv_w_gate, w_branch):
    L, D = w_in.shape[0], w_in.shape[1]
    zc = lambda n: jnp.zeros((L, D, n), F32)
    o = 0
    cuts = {}
    for name, n in (("c_q", 256), ("c_kv", 128), ("k_rope", 32), ("gla_q", 256), ("gla_k", 256),
                    ("gla_v", 512), ("gla_g", 512), ("gla_a", 16), ("rwkv", 1824), ("gates", 3072)):
        cuts[name] = w_in[:, :, o:o + n]
        o += n
    kr = cuts["k_rope"]
    kr_b = jnp.concatenate([-kr[..., 16:], kr[..., :16]], axis=-1)
    wa = jnp.concatenate([cuts["c_q"], cuts["c_kv"], zc(64), kr, zc(32), zc(64), kr_b, zc(32)], -1)
    wb = jnp.concatenate([cuts["gla_v"], cuts["gla_g"], cuts["gla_q"], cuts["gla_k"],
                          cuts["gla_a"], zc(112)], -1)
    wc = jnp.concatenate([cuts["rwkv"], zc(RWKV_MISC - 288)], -1)
    wd = cuts["gates"]

    wuq = mla_w_uq.reshape(L, MLA_Q_RANK, MLA_HEADS, MLA_NOPE + MLA_ROPE)
    nope, rope = wuq[..., :MLA_NOPE], wuq[..., MLA_NOPE:]
    zq = lambda n: jnp.zeros((L, MLA_Q_RANK, MLA_HEADS, n), F32)
    wqa = jnp.concatenate([nope, rope, zq(32)], -1).reshape(L, MLA_Q_RANK, MLA_W)
    rope_b = jnp.concatenate([-rope[..., 16:], rope[..., :16]], -1)
    wqb = jnp.concatenate([zq(64), rope_b, zq(32)], -1).reshape(L, MLA_Q_RANK, MLA_W)
    wukv = mla_w_ukv.reshape(L, MLA_KV_RANK, MLA_HEADS, 128)
    zk = jnp.zeros((L, MLA_KV_RANK, MLA_HEADS, 64), F32)
    wk = jnp.concatenate([wukv[..., :64], zk], -1).reshape(L, MLA_KV_RANK, MLA_W)
    wv = jnp.concatenate([wukv[..., 64:], zk[..., :MLA_VSLOT - 64]], -1).reshape(
        L, MLA_KV_RANK, MLA_HEADS * MLA_VSLOT)

    bra = w_branch[:, :512].reshape(L, MLA_HEADS, 64, D)
    bra = jnp.concatenate([bra, jnp.zeros_like(bra)], axis=2).reshape(L, MLA_W, D)
    brb = w_branch[:, 512:512 + GLA_W]
    brc = w_branch[:, 512 + GLA_W:]

    walpha = jnp.concatenate(
        [gla_w_alpha, jnp.zeros((L, 128 - GLA_GATE_RANK, GLA_QK), F32)], axis=1)
    gn = jnp.tile(gla_norm, (1, GLA_HEADS))[:, None, :]
    mu = jnp.concatenate([rwkv_mu, jnp.zeros((L, RWKV_MISC - 288), F32)], -1)[:, None, :]
    zr = lambda n: jnp.zeros((L, n, RWKV_W), F32)
    def hi_lo(w):
        hi = w.astype(BF16)
        return jnp.stack([hi, (w - hi.astype(F32)).astype(BF16)], axis=1)

    wdec = hi_lo(jnp.concatenate([rwkv_w_decay, zr(64)], axis=1))
    wiclr = hi_lo(jnp.concatenate([zr(64), rwkv_w_iclr], axis=1))
    wgate = jnp.concatenate([rwkv_w_gate, zr(256 - RWKV_GATE_RANK)], axis=1)
    bf = lambda w: w.astype(BF16)
    return dict(wa=bf(wa), wb=bf(wb), wc=bf(wc), wd=bf(wd), wqa=bf(wqa), wqb=bf(wqb), wk=bf(wk),
                wv=bf(wv), bra=bf(bra), brb=bf(brb), brc=bf(brc), walpha=hi_lo(walpha), gn=gn, mu=mu,
                wdec=wdec, wiclr=wiclr, wgate=bf(wgate))


def kernel(x, mem, positions, norm_g, w_ffn_in, w_ffn_out, w_in, mla_q_norm, mla_w_uq, mla_kv_norm, mla_w_ukv, gla_w_alpha, gla_b_alpha, gla_norm, rwkv_mu, rwkv_w0, rwkv_w_decay, rwkv_a0, rwkv_w_iclr, rwkv_w_gate, rwkv_k_k, rwkv_k_a, rwkv_r_k, rwkv_ln_w, rwkv_ln_b, w_branch, w_out, mem_norm, mem_wq, mem_wkv, mem_wo):
    B, S, D = x.shape
    L = norm_g.shape[0]
    T = B * S
    tm = min(512, S)
    tq = min(1024, S)
    tk = tq // 2
    ts = min(256, S)
    tf = D_FF // 2

    pp = _prepare_params(w_in, mla_w_uq, mla_w_ukv, gla_w_alpha, gla_norm, rwkv_mu, rwkv_w_decay,
                         rwkv_w_iclr, rwkv_w_gate, w_branch)
    bf = lambda w: w.astype(BF16)
    ffn_in, ffn_out = bf(w_ffn_in), bf(w_ffn_out)
    wout, wq, wkv, wo = bf(w_out), bf(mem_wq), bf(mem_wkv), bf(mem_wo)
    ng = norm_g.reshape(L * 8, 1, D)
    row = lambda p: p[:, None, :]
    gl = dict(walpha=pp["walpha"], balpha=row(gla_b_alpha), gn=pp["gn"])
    rw = dict(mu=pp["mu"], w0=row(rwkv_w0), wdec=pp["wdec"], a0=row(rwkv_a0), wiclr=pp["wiclr"],
              wgate=pp["wgate"], k_k=row(rwkv_k_k), k_a=row(rwkv_k_a), r_k=row(rwkv_r_k),
              ln_w=row(rwkv_ln_w), ln_b=row(rwkv_ln_b))

    inv_freq = ROPE_THETA ** (-jnp.arange(0, MLA_ROPE, 2, dtype=F32) / MLA_ROPE)
    invf_lane = jnp.concatenate(
        [jnp.zeros((MLA_NOPE,), F32), inv_freq, inv_freq, jnp.zeros((32,), F32)])[None, :]
    c0, s0 = _rope_tables(positions.astype(F32).reshape(T, 1), invf_lane, tm)

    x = x.reshape(T, D)
    for l in range(L):
        x = _ffn(x, ng, ffn_in, ffn_out, l, 0, tm, tf)
        za, zb, zc = _normproj(x, ng, l * 8 + 2, [pp["wa"], pp["wb"], pp["wc"]], l, tm)
        q, k, vt = _mla_prep(za, c0, s0, row(mla_q_norm), row(mla_kv_norm), pp["wqa"], pp["wqb"],
                             pp["wk"], pp["wv"], l, tk)
        o_a = _mla_attn(q, k, vt, B, S, tq, tk)
        o_b, o_c = _recur(zb, zc, gl, rw, l, B, S, ts)
        x = _merge(x, o_a, o_b, o_c, pp["wd"], pp["bra"], pp["brb"], pp["brc"], wout, ng, l, tm)
        kv = _mem_kv(mem, row(mem_norm), wkv, l)
        x = _mem_attn(x, ng, wq, kv, wo, l, S, tm)
        x = _ffn(x, ng, ffn_in, ffn_out, l, 1, tm, tf)
    return x.reshape(B, S, D)
```

```python
import functools

import jax
import jax.numpy as jnp
import numpy as np
from jax import lax
from jax.experimental import pallas as pl
from jax.experimental.pallas import tpu as pltpu

F32 = jnp.float32
BF16 = jnp.bfloat16
HI = lax.Precision.HIGHEST

D_MODEL = 1024
D_FF = 2816
NORM_EPS = 1e-6
MLA_HEADS = 8
MLA_NOPE = 64
MLA_ROPE = 32
MLA_Q_RANK = 256
MLA_KV_RANK = 128
ROPE_THETA = 10000.0
MLA_SLOT = 128
MLA_W = MLA_HEADS * MLA_SLOT
MLA_HPS = 2
MLA_VSLOT = 80
MLA_ONE = 64
GLA_HEADS = 4
GLA_DK = 64
GLA_DV = 128
GLA_GATE_RANK = 16
GLA_GATE_NORM = 16.0
GLA_QK = GLA_HEADS * GLA_DK
GLA_W = GLA_HEADS * GLA_DV
GLA_LEVELS = 6
GLA_NEG = -1e30
RWKV_HEADS = 8
RWKV_N = 64
RWKV_DECAY_RANK = 64
RWKV_ICLR_RANK = 64
RWKV_GATE_RANK = 160
RWKV_GN_EPS = 64e-5
RWKV_W = RWKV_HEADS * RWKV_N
RWKV_MISC = 384
RWKV_COLS_PAD = 3 * RWKV_W + RWKV_MISC
RWKV_QUAD = 4 * RWKV_N
RWKV_NQ = RWKV_W // RWKV_QUAD
REC_CPI = 2
MEM_HEADS = 4
MEM_HD = D_MODEL // MEM_HEADS
N_BRANCH = 3
MERGE_COLS = 256
CHUNK = 64

VMEM_LIMIT_BYTES = 56 * 1024 * 1024


def _cparams(n_axes, flags=None):
    return pltpu.CompilerParams(
        dimension_semantics=("arbitrary",) * n_axes,
        vmem_limit_bytes=VMEM_LIMIT_BYTES,
        flags=flags,
    )


def _dot(a, b, precision=None):
    return jnp.dot(a, b, preferred_element_type=F32, precision=precision)


def _dot_nt(a, b, precision=None):
    return lax.dot_general(a, b, (((1,), (1,)), ((), ())),
                           preferred_element_type=F32, precision=precision)


def _dot_tn(a, b, precision=None):
    return lax.dot_general(a, b, (((0,), (0,)), ((), ())),
                           preferred_element_type=F32, precision=precision)


def _rms(x, g, eps=NORM_EPS):
    return x * lax.rsqrt(jnp.mean(x * x, axis=-1, keepdims=True) + eps) * g


def _sigmoid(x):
    return 1.0 / (1.0 + jnp.exp(-x))


def _softplus(x):
    return jnp.maximum(x, 0.0) + jnp.log(1.0 + jnp.exp(-jnp.abs(x)))


def _split2(x):
    hi = x.astype(BF16)
    return hi, (x - hi.astype(F32)).astype(BF16)


def _dot_split(x, w_hi, w_lo):
    hi, lo = _split2(x)
    m = x.shape[0]
    top = _dot(jnp.concatenate([hi, lo], axis=0), w_hi)
    return top[:m] + top[m:] + _dot(hi, w_lo)


def _dot_01(m01_bf16, x):
    p1 = x.astype(BF16)
    r1 = x - p1.astype(F32)
    p2 = r1.astype(BF16)
    p3 = (r1 - p2.astype(F32)).astype(BF16)
    n = x.shape[1]
    out = _dot(m01_bf16, jnp.concatenate([p1, p2, p3], axis=1))
    return out[:, :n] + out[:, n:2 * n] + out[:, 2 * n:]


def _iota(shape, dim):
    return lax.broadcasted_iota(jnp.int32, shape, dim)


def _segsum(x, e_bf16):
    hi = x.astype(BF16)
    lo = (x - hi.astype(F32)).astype(BF16)
    return _dot(hi, e_bf16) + _dot(lo, e_bf16)


def _vec_spec(n, l):
    return pl.BlockSpec((None, 1, n), lambda *_: (l, 0, 0))


def _mat_spec(r, c, l):
    return pl.BlockSpec((None, r, c), lambda *_: (l, 0, 0))


def _ffn_body(x_ref, gpre_ref, wi_ref, wo_ref, gpost_ref, o_ref, *, n_split):
    x = x_ref[...]
    h = _rms(x, gpre_ref[...]).astype(BF16)
    ff = wo_ref.shape[0]
    tf = ff // n_split
    y = None
    for j in range(n_split):
        g = _dot(h, wi_ref[:, j * tf:(j + 1) * tf])
        u = _dot(h, wi_ref[:, ff + j * tf:ff + (j + 1) * tf])
        act = (g * _sigmoid(g) * u).astype(BF16)
        part = _dot(act, wo_ref[j * tf:(j + 1) * tf, :])
        y = part if y is None else y + part
    o_ref[...] = x + 0.5 * _rms(y, gpost_ref[...])


def _ffn(x, ng, w_in, w_out, l, k, tm, n_split):
    T, D = x.shape
    ff = w_out.shape[2]
    g_pre, g_post = l * 8 + 6 * k, l * 8 + 6 * k + 1
    return pl.pallas_call(
        functools.partial(_ffn_body, n_split=n_split),
        grid=(T // tm,),
        in_specs=[
            pl.BlockSpec((tm, D), lambda i: (i, 0)),
            pl.BlockSpec((None, 1, D), lambda i: (g_pre, 0, 0)),
            pl.BlockSpec((None, None, D, 2 * ff), lambda i: (l, k, 0, 0),
                         pipeline_mode=pl.Buffered(1)),
            pl.BlockSpec((None, None, ff, D), lambda i: (l, k, 0, 0), pipeline_mode=pl.Buffered(1)),
            pl.BlockSpec((None, 1, D), lambda i: (g_post, 0, 0)),
        ],
        out_specs=pl.BlockSpec((tm, D), lambda i: (i, 0)),
        out_shape=jax.ShapeDtypeStruct((T, D), F32),
        compiler_params=_cparams(1),
        name="ffn",
    )(x, ng, w_in, w_out, ng)


def _normproj_body(x_ref, g_ref, *refs):
    n = len(refs) // 2
    h = _rms(x_ref[...], g_ref[...]).astype(BF16)
    for w_ref, o_ref in zip(refs[:n], refs[n:]):
        o_ref[...] = _dot(h, w_ref[...]).astype(o_ref.dtype)


def _normproj(x, ng, g_idx, ws, l, tm):
    T, D = x.shape
    widths = [w.shape[2] for w in ws]
    return pl.pallas_call(
        _normproj_body,
        grid=(T // tm,),
        in_specs=[pl.BlockSpec((tm, D), lambda i: (i, 0)),
                  pl.BlockSpec((None, 1, D), lambda i: (g_idx, 0, 0))]
        + [pl.BlockSpec((None, D, n), lambda i: (l, 0, 0), pipeline_mode=pl.Buffered(1))
           for n in widths],
        out_specs=[pl.BlockSpec((tm, n), lambda i: (i, 0)) for n in widths],
        out_shape=[jax.ShapeDtypeStruct((T, n), F32) for n in widths],
        compiler_params=_cparams(1),
        name="mixer_proj",
    )(x, ng, *ws)


def _rope_table_body(pos_ref, invf_ref, c_ref, s_ref):
    ang = pos_ref[...] * invf_ref[...]
    lane = _iota(ang.shape, 1)
    rot = (lane >= MLA_NOPE) & (lane < MLA_NOPE + MLA_ROPE)
    c_ref[...] = jnp.where(lane < MLA_NOPE, 1.0, jnp.where(rot, jnp.cos(ang), 0.0))
    s_ref[...] = jnp.where(rot, jnp.sin(ang), 0.0)


def _rope_tables(pos_f32, invf_lane, tm):
    T = pos_f32.shape[0]
    return pl.pallas_call(
        _rope_table_body,
        grid=(T // tm,),
        in_specs=[pl.BlockSpec((tm, 1), lambda i: (i, 0)),
                  pl.BlockSpec((1, MLA_SLOT), lambda i: (0, 0))],
        out_specs=[pl.BlockSpec((tm, MLA_SLOT), lambda i: (i, 0))] * 2,
        out_shape=[jax.ShapeDtypeStruct((T, MLA_SLOT), F32)] * 2,
        compiler_params=_cparams(1),
        name="rope_tables",
    )(pos_f32, invf_lane)


def _mla_prep_body(za_ref, c0_ref, s0_ref, qn_ref, kvn_ref, wqa_ref, wqb_ref, wk_ref, wv_ref,
                   one_ref, q_ref, k_ref, vt_ref):
    za = za_ref[...]
    cq = za[:, :MLA_Q_RANK]
    ckv = za[:, MLA_Q_RANK:MLA_Q_RANK + MLA_KV_RANK]
    kra = za[:, 384:512]
    krb = za[:, 512:640]
    cqn = _rms(cq, qn_ref[...]).astype(BF16)
    ckvn = _rms(ckv, kvn_ref[...]).astype(BF16)
    c0 = c0_ref[...]
    s0 = s0_ref[...]
    c8 = jnp.concatenate([c0] * MLA_HEADS, axis=1)
    s8 = jnp.concatenate([s0] * MLA_HEADS, axis=1)
    scale = (MLA_NOPE + MLA_ROPE) ** -0.5
    q = (_dot(cqn, wqa_ref[...]) * c8 + _dot(cqn, wqb_ref[...]) * s8) * scale
    q_ref[...] = q.astype(BF16)
    krot = kra * c0 + krb * s0
    k = _dot(ckvn, wk_ref[...]) + jnp.concatenate([krot] * MLA_HEADS, axis=1)
    k_ref[...] = k.astype(BF16)
    v = _dot(ckvn, wv_ref[...]) + one_ref[...]
    vt_ref[...] = v.T.astype(BF16)


def _mla_prep(za, c0, s0, qn, kvn, wqa, wqb, wk, wv, l, tk):
    T = za.shape[0]
    vw = MLA_HEADS * MLA_VSLOT
    tok = lambda n: pl.BlockSpec((tk, n), lambda i: (i, 0))
    one_lane = jnp.asarray((np.arange(vw) % MLA_VSLOT == MLA_ONE).astype(np.float32))[None, :]
    return pl.pallas_call(
        _mla_prep_body,
        grid=(T // tk,),
        in_specs=[tok(za.shape[1]), tok(MLA_SLOT), tok(MLA_SLOT),
                  _vec_spec(MLA_Q_RANK, l), _vec_spec(MLA_KV_RANK, l),
                  _mat_spec(MLA_Q_RANK, MLA_W, l), _mat_spec(MLA_Q_RANK, MLA_W, l),
                  _mat_spec(MLA_KV_RANK, MLA_W, l), _mat_spec(MLA_KV_RANK, vw, l),
                  pl.BlockSpec((1, vw), lambda i: (0, 0))],
        out_specs=[tok(MLA_W), tok(MLA_W),
                   pl.BlockSpec((None, vw, tk), lambda i: (i, 0, 0))],
        out_shape=[jax.ShapeDtypeStruct((T, MLA_W), BF16)] * 2
        + [jax.ShapeDtypeStruct((T // tk, vw, tk), BF16)],
        compiler_params=_cparams(1),
        name="mla_prep",
    )(za, c0, s0, qn, kvn, wqa, wqb, wk, wv, one_lane)


def _mla_attn_body(q_ref, k_ref, vt_ref, o_ref, s_ref, m_ref, acc_ref, *, tq, tk):
    qi = pl.program_id(2)
    m_ref[...] = jnp.full_like(m_ref, -1e30)
    acc_ref[...] = jnp.zeros_like(acc_ref)
    heads = [slice(hh * MLA_SLOT, (hh + 1) * MLA_SLOT) for hh in range(MLA_HPS)]

    def produce(slot, ki):
        r0 = pl.multiple_of(ki * tk, tk)
        for hh, sl in enumerate(heads):
            s_ref[slot, hh] = _dot_nt(k_ref[pl.ds(r0, tk), sl], q_ref[:, sl])

    def consume(slot, ki, diag):
        for hh, sl in enumerate(heads):
            st = s_ref[slot, hh]
            if diag is not None:
                key = _iota(st.shape, 0) + diag * tk
                st = jnp.where(key <= _iota(st.shape, 1), st, -1e30)
            m_old = m_ref[hh]
            m_new = jnp.maximum(m_old, jnp.max(st, axis=0, keepdims=True))
            p = jnp.exp(st - m_new).astype(BF16)
            vt = vt_ref[ki, hh * MLA_VSLOT:(hh + 1) * MLA_VSLOT, :]
            acc_ref[hh] = jnp.exp(m_old - m_new) * acc_ref[hh] + _dot(vt, p)
            m_ref[hh] = m_new

    def body(j, carry):
        produce(1, 2 * j + 1)
        consume(0, 2 * j, None)
        produce(0, 2 * j + 2)
        consume(1, 2 * j + 1, None)
        return carry

    produce(0, 0)
    lax.fori_loop(0, qi, body, 0)
    produce(1, 2 * qi + 1)
    consume(0, 2 * qi, 0)
    consume(1, 2 * qi + 1, 1)
    for hh in range(MLA_HPS):
        acc = acc_ref[hh]
        o_t = acc / acc[MLA_ONE:MLA_ONE + 1, :]
        o_t = jnp.concatenate([o_t, jnp.zeros((MLA_SLOT - MLA_VSLOT, tq), F32)], axis=0)
        o_ref[:, hh * MLA_SLOT:(hh + 1) * MLA_SLOT] = o_t.T.astype(o_ref.dtype)


def _mla_attn(q, k, vt, B, S, tq, tk):
    assert tq == 2 * tk
    T = q.shape[0]
    nq, nk = S // tq, S // tk
    w = MLA_HPS * MLA_SLOT
    return pl.pallas_call(
        functools.partial(_mla_attn_body, tq=tq, tk=tk),
        grid=(B, MLA_HEADS // MLA_HPS, nq),
        in_specs=[
            pl.BlockSpec((tq, w), lambda b, h, i: (b * nq + i, h)),
            pl.BlockSpec((S, w), lambda b, h, i: (b, h)),
            pl.BlockSpec((nk, MLA_HPS * MLA_VSLOT, tk), lambda b, h, i: (b, h, 0)),
        ],
        out_specs=pl.BlockSpec((tq, w), lambda b, h, i: (b * nq + i, h)),
        out_shape=jax.ShapeDtypeStruct((T, MLA_W), BF16),
        scratch_shapes=[pltpu.VMEM((2, MLA_HPS, tk, tq), F32),
                        pltpu.VMEM((MLA_HPS, 1, tq), F32),
                        pltpu.VMEM((MLA_HPS, MLA_VSLOT, tq), F32)],
        compiler_params=_cparams(3),
        name="mla_attn",
    )(q, k, vt)


def _gla_pair_masks():
    i = np.arange(GLA_HEADS * CHUNK)[:, None] % CHUNK
    j = np.arange(CHUNK)[None, :]
    same = [(i >> (6 - lv)) == (j >> (6 - lv)) for lv in range(GLA_LEVELS)]
    return jnp.asarray(np.stack(same + [i == j]).astype(np.float32))


def _gla_stages(v_ref, g_ref, q_ref, k_ref, a_ref, walpha_ref, balpha_ref, gn_ref, pair_ref,
                o_ref, ht_ref, n_batch):
    C = CHUNK
    walpha_hi, walpha_lo = walpha_ref[0], walpha_ref[1]
    balpha = balpha_ref[...]
    gn = gn_ref[...]
    tri = (_iota((C, C), 1) <= _iota((C, C), 0)).astype(BF16)
    st_r, st_c = _iota((GLA_HEADS * C, GLA_QK), 0), _iota((GLA_HEADS * C, GLA_QK), 1)
    head_mask = ((st_r >> 6) == (st_c >> 6)).astype(BF16)
    ht_r, ht_c = _iota((GLA_W, GLA_QK), 0), _iota((GLA_W, GLA_QK), 1)
    state_mask = (ht_r >> 7) == (ht_c >> 6)
    tok = _iota((C, GLA_QK), 0)
    halves = [C >> (lv + 1) for lv in range(GLA_LEVELS)]

    def stack(x):
        return jnp.concatenate([x.astype(BF16)] * GLA_HEADS, axis=0) * head_mask

    def level_operands(q, k, cb):
        last = {1: cb}
        for s in halves[:0:-1]:
            f = last[s]
            last[2 * s] = jnp.where((tok & s) != 0, f, pltpu.roll(f, C - s, 0))
        ops = []
        for hs in halves:
            f = last[hs]
            right = (tok & hs) != 0
            d = cb - jnp.where(right, pltpu.roll(f, hs, 0), f)
            ql = q * jnp.exp(jnp.where(right, d, GLA_NEG))
            kl = k * jnp.exp(jnp.where(right, GLA_NEG, -d))
            ops.append((stack(ql), kl.astype(BF16)))
        ops.append((stack(q), k.astype(BF16)))
        return ops

    def stages(c):
        cbs = [(b, pl.ds(pl.multiple_of((c * REC_CPI + sub) * C, C), C))
               for sub in range(REC_CPI) for b in range(n_batch)]
        xs = [_dot_split(a_ref[b, rows, :], walpha_hi, walpha_lo) + balpha for b, rows in cbs]
        yield
        cum = [_dot_01(tri, -_softplus(-x) * (1.0 / GLA_GATE_NORM)) for x in xs]
        yield
        st = []
        for (b, rows), cb in zip(cbs, cum):
            b_last = cb[C - 1:C, :]
            q = q_ref[b, rows, :] * GLA_DK ** -0.5
            k = k_ref[b, rows, :]
            st.append(dict(
                v=v_ref[b, rows, :].astype(BF16),
                qe=(q * jnp.exp(cb)).astype(BF16),
                ke=(k * jnp.exp(b_last - cb)).astype(BF16),
                gam=jnp.exp(b_last),
                ops=level_operands(q, k, cb)))
        scs = [None] * len(cbs)
        for lv in range(GLA_LEVELS + 1):
            for i, s in enumerate(st):
                ql, kl = s["ops"][lv]
                part = _dot_nt(ql, kl) * pair_ref[lv]
                scs[i] = part if lv == 0 else scs[i] + part
            yield
        osts = [_dot(sc.astype(BF16), s["v"]) for sc, s in zip(scs, st)]
        yield
        for (b, rows), s, ost in zip(cbs, st, osts):
            o_intra = jnp.concatenate(
                [ost[h * C:(h + 1) * C, h * GLA_DV:(h + 1) * GLA_DV] for h in range(GLA_HEADS)],
                axis=1)
            ht = ht_ref[b]
            o = o_intra + _dot_nt(s["qe"], ht.astype(BF16))
            ht_ref[b] = ht * s["gam"] + jnp.where(state_mask, _dot_tn(s["v"], s["ke"]), 0.0)
            outs = []
            for h in range(GLA_HEADS):
                oh = o[:, h * GLA_DV:(h + 1) * GLA_DV]
                outs.append(oh * lax.rsqrt(jnp.mean(oh * oh, axis=-1, keepdims=True) + NORM_EPS))
            on = jnp.concatenate(outs, axis=1) * gn
            g = g_ref[b, rows, :]
            o_ref[b, rows, :] = (on * (g * _sigmoid(g))).astype(o_ref.dtype)
            yield

    return stages


def _rwkv_stages(z_ref, mu_ref, w0_ref, wdec_ref, a0_ref, wiclr_ref, wgate_ref, kk_ref, ka_ref,
                 rk_ref, lnw_ref, lnb_ref, o_ref, h_ref, zlast_ref, n_batch):
    C, W, Q = CHUNK, RWKV_W, RWKV_QUAD
    mu = mu_ref[...]
    w0, a0 = w0_ref[...], a0_ref[...]
    wdec_hi, wdec_lo = wdec_ref[0], wdec_ref[1]
    wiclr_hi, wiclr_lo = wiclr_ref[0], wiclr_ref[1]
    wgate = wgate_ref[...]
    k_k, k_a, r_k = kk_ref[...], ka_ref[...], rk_ref[...]
    ln_w, ln_b = lnw_ref[...], lnb_ref[...]

    tri = (_iota((C, C), 1) <= _iota((C, C), 0)).astype(BF16)
    sq_r, sq_c = _iota((Q, Q), 0), _iota((Q, Q), 1)
    head_blk = (sq_r >> 6) == (sq_c >> 6)
    eye_q = sq_r == sq_c
    e_seg = head_blk.astype(BF16)
    wd_t, wd_s = _iota((C, Q), 0), _iota((C, Q), 1) & (C - 1)
    strict = wd_s < wd_t
    incl = wd_s <= wd_t
    eye_wide = (wd_s == wd_t).astype(F32)
    row0 = _iota((C, RWKV_COLS_PAD), 0) == 0

    def bd(x):
        return jnp.concatenate([x.astype(BF16)] * 4, axis=0) * e_seg

    def quad(x, i):
        return x[:, i * Q:(i + 1) * Q]

    def segsum(x):
        return jnp.concatenate([_segsum(quad(x, i), e_seg) for i in range(RWKV_NQ)], axis=1)

    def chunk_one(b, rows):
        z = z_ref[b, rows, :]
        zp = jnp.where(row0, zlast_ref[b, 0:1, :], pltpu.roll(z, 1, 0))
        zlast_ref[b, 0:1, :] = z[C - 1:C, :]
        z = z + mu * (zp - z)
        r = z[:, 0:W]
        k = z[:, W:2 * W]
        v = z[:, 2 * W:3 * W]
        m0 = z[:, 3 * W:3 * W + 128]
        m12 = z[:, 3 * W + 128:3 * W + 384]
        w_log = -_softplus(-(w0 + _dot_split(jnp.tanh(m0), wdec_hi, wdec_lo))) - 0.5
        lw = -jnp.exp(w_log)
        a = _sigmoid(a0 + _dot_split(m0, wiclr_hi, wiclr_lo))
        g = _dot(_sigmoid(m12).astype(BF16), wgate)
        kk = k * k_k
        kkn = kk * lax.rsqrt(jnp.maximum(segsum(kk * kk), 1e-24))
        k2 = k * (1.0 + (a - 1.0) * k_a)
        beta = kkn * a
        cs = _dot_01(tri, lw)
        c_last = cs[C - 1:C, :]
        dec_in = jnp.exp(-cs)
        dec_out = jnp.exp(c_last - cs)
        kt = kkn * jnp.exp(cs - lw)
        rt = r * jnp.exp(cs)
        bh = beta * dec_in
        kh = k2 * dec_in
        kbar = k2 * dec_out
        bbar = beta * dec_out
        gam = jnp.exp(c_last)
        chains = [dict(b=b, i=i, kt=quad(kt, i), rt=quad(rt, i), v=quad(v, i), bh=quad(bh, i),
                       kh=quad(kh, i), kbar=quad(kbar, i), bbar=quad(bbar, i), gam=quad(gam, i))
                  for i in range(RWKV_NQ)]
        return dict(r=r, k2=k2, v=v, g=g), chains

    def epilogue(b, rows, tok, y):
        mean = segsum(y) * (1.0 / RWKV_N)
        d = y - mean
        var = segsum(d * d) * (1.0 / RWKV_N)
        yn = d * lax.rsqrt(var + RWKV_GN_EPS) * ln_w + ln_b
        bonus = segsum(tok["r"] * tok["k2"] * r_k) * tok["v"]
        o_ref[b, rows, :] = ((yn + bonus) * tok["g"]).astype(o_ref.dtype)

    def stages(c):
        groups, chains = [], []
        for sub in range(REC_CPI):
            rows = pl.ds(pl.multiple_of((c * REC_CPI + sub) * C, C), C)
            for b in range(n_batch):
                tok, ch = chunk_one(b, rows)
                groups.append((b, rows, tok, ch))
                chains += ch
                yield
        for ch in chains:
            lhs = jnp.concatenate([ch["kt"], ch["rt"]], axis=0).astype(BF16)
            rhs = jnp.concatenate([bd(ch["bh"]), bd(ch["kh"])], axis=0)
            sc = _dot_nt(lhs, rhs)
            ch["a_kk"] = jnp.where(strict, sc[:C, Q:], 0.0)
            ch["a_rb"] = jnp.where(incl, sc[C:, :Q], 0.0)
            ch["a_rk"] = jnp.where(incl, sc[C:, Q:], 0.0)
            ch["n"] = -jnp.where(strict, sc[:C, :Q], 0.0)
            ch["t"] = eye_wide + ch["n"]
        yield
        for ch in chains:
            ch["p"] = _dot(ch["n"].astype(BF16), bd(ch["n"]))
            ch["bd_v"] = bd(ch["v"])
            ch["akk_v"] = _dot(ch["a_kk"].astype(BF16), ch["bd_v"])
        yield
        for _ in range(4):
            for ch in chains:
                tp = _dot(jnp.concatenate([ch["t"], ch["p"]], axis=0).astype(BF16), bd(ch["p"]))
                ch["t"] = ch["t"] + tp[:C]
                ch["p"] = tp[C:]
            yield
        for ch in chains:
            ch["t"] = ch["t"] + _dot(ch["t"].astype(BF16), bd(ch["p"]))
        yield
        for ch in chains:
            w12 = _dot(ch["t"].astype(BF16),
                       jnp.concatenate([bd(ch["akk_v"]), bd(ch["kt"])], axis=1))
            ch["w1"], ch["w2"] = w12[:, :Q], w12[:, Q:]
        yield
        for ch in chains:
            w1, w2, bbar = ch["w1"], ch["w2"], ch["bbar"]
            ch["y0"] = _dot(jnp.concatenate([ch["a_rk"], ch["a_rb"]], axis=1).astype(BF16),
                            jnp.concatenate([ch["bd_v"], bd(-w1)], axis=0))
            ch["rp"] = (ch["rt"] - _dot(ch["a_rb"].astype(BF16), bd(w2))).astype(BF16)
            ch["p_bd"] = (jnp.where(eye_q, ch["gam"], 0.0) - jnp.where(
                head_blk, _dot_tn(bbar.astype(BF16), w2.astype(BF16)), 0.0)).astype(BF16)
            ch["q_bd"] = jnp.where(
                head_blk,
                _dot_tn(jnp.concatenate([ch["kbar"], -bbar], axis=0).astype(BF16),
                        jnp.concatenate([ch["v"], w1], axis=0).astype(BF16)), 0.0)
        yield
        for b, rows, tok, chs in groups:
            ys = []
            for ch in chs:
                hb = h_ref[b, ch["i"]].astype(BF16)
                ys.append(_dot(ch["rp"], hb) + ch["y0"])
                h_ref[b, ch["i"]] = _dot(ch["p_bd"], hb) + ch["q_bd"]
            epilogue(b, rows, tok, jnp.concatenate(ys, axis=1))
            yield

    return stages


N_GLA_IN, N_RWKV_IN = 9, 12


def _recur_body(*refs, n_chunks, n_batch):
    gla_in = refs[:N_GLA_IN]
    rwkv_in = refs[N_GLA_IN:N_GLA_IN + N_RWKV_IN]
    ob_ref, oc_ref, ht_ref, h_ref, zlast_ref = refs[N_GLA_IN + N_RWKV_IN:]

    @pl.when(pl.program_id(0) == 0)
    def _():
        ht_ref[...] = jnp.zeros_like(ht_ref)
        h_ref[...] = jnp.zeros_like(h_ref)
        zlast_ref[...] = jnp.zeros_like(zlast_ref)

    gla = _gla_stages(*gla_in, ob_ref, ht_ref, n_batch)
    rwkv = _rwkv_stages(*rwkv_in, oc_ref, h_ref, zlast_ref, n_batch)

    def chunk(c, carry):
        live = [rwkv(c), gla(c)]
        while live:
            live = [g for g in live if next(g, True) is None]
        return carry

    lax.fori_loop(0, n_chunks // REC_CPI, chunk, 0)


def _recur(zb, zc, gp, rp, l, B, S, ts):
    ns = S // ts
    W, Q = RWKV_W, RWKV_QUAD
    blk = lambda n, j: pl.BlockSpec((B, ts, n), lambda s: (0, s, j))
    split = lambda c: pl.BlockSpec((None, 2, 128, c), lambda s: (l, 0, 0, 0))
    zb = zb.reshape(B, S, zb.shape[1])
    o_b, o_c = pl.pallas_call(
        functools.partial(_recur_body, n_chunks=ts // CHUNK, n_batch=B),
        grid=(ns,),
        in_specs=[
            blk(GLA_W, 0),
            blk(GLA_W, 1),
            blk(GLA_QK, 4),
            blk(GLA_QK, 5),
            blk(128, 12),
            split(GLA_QK), _vec_spec(GLA_QK, l), _vec_spec(GLA_W, l),
            pl.BlockSpec((GLA_LEVELS + 1, GLA_HEADS * CHUNK, CHUNK), lambda s: (0, 0, 0)),
            blk(RWKV_COLS_PAD, 0),
            _vec_spec(RWKV_COLS_PAD, l), _vec_spec(W, l), split(W),
            _vec_spec(W, l), split(W), _mat_spec(256, W, l),
            _vec_spec(W, l), _vec_spec(W, l), _vec_spec(W, l), _vec_spec(W, l), _vec_spec(W, l),
        ],
        out_specs=[blk(GLA_W, 0), blk(W, 0)],
        out_shape=[jax.ShapeDtypeStruct((B, S, GLA_W), BF16), jax.ShapeDtypeStruct((B, S, W), BF16)],
        scratch_shapes=[pltpu.VMEM((B, GLA_W, GLA_QK), F32),
                        pltpu.VMEM((B, RWKV_NQ, Q, Q), F32),
                        pltpu.VMEM((B, 8, RWKV_COLS_PAD), F32)],
        compiler_params=_cparams(1),
        name="recurrences",
    )(zb, zb, zb, zb, zb, gp["walpha"], gp["balpha"], gp["gn"], _gla_pair_masks(),
      zc.reshape(B, S, RWKV_COLS_PAD), rp["mu"], rp["w0"], rp["wdec"], rp["a0"], rp["wiclr"],
      rp["wgate"], rp["k_k"], rp["k_a"], rp["r_k"], rp["ln_w"], rp["ln_b"])
    return o_b.reshape(B * S, GLA_W), o_c.reshape(B * S, W)


def _merge_body(x_ref, oa_ref, ob_ref, oc_ref, gpre_ref, wd_ref, wa_ref, wb_ref, wc_ref, wo_ref,
                gpost_ref, o_ref):
    x = x_ref[...]
    D = D_MODEL
    h = _rms(x, gpre_ref[...]).astype(BF16)
    branches = ((oa_ref[...], wa_ref), (ob_ref[...], wb_ref), (oc_ref[...], wc_ref))
    y = None
    for c in range(0, D, MERGE_COLS):
        cols = slice(c, c + MERGE_COLS)
        merged = None
        for j, (o, w_r) in enumerate(branches):
            gate = _sigmoid(_dot(h, wd_ref[:, j * D + c:j * D + c + MERGE_COLS]))
            term = gate * _dot(o, w_r[:, cols])
            merged = term if merged is None else merged + term
        part = _dot(merged.astype(BF16), wo_ref[cols, :])
        y = part if y is None else y + part
    o_ref[...] = x + _rms(y, gpost_ref[...])


def _merge(x, oa, ob, oc, wd, wa, wb, wc, wo, ng, l, tm):
    T, D = x.shape
    tok = lambda n: pl.BlockSpec((tm, n), lambda i: (i, 0))
    res = lambda r, c: pl.BlockSpec((None, r, c), lambda i: (l, 0, 0), pipeline_mode=pl.Buffered(1))
    gain = lambda j: pl.BlockSpec((None, 1, D), lambda i: (l * 8 + j, 0, 0))
    return pl.pallas_call(
        _merge_body,
        grid=(T // tm,),
        in_specs=[tok(D), tok(MLA_W), tok(GLA_W), tok(RWKV_W), gain(2), res(D, N_BRANCH * D),
                  res(MLA_W, D), res(GLA_W, D), res(RWKV_W, D), res(D, D), gain(3)],
        out_specs=tok(D),
        out_shape=jax.ShapeDtypeStruct((T, D), F32),
        compiler_params=_cparams(1),
        name="merge",
    )(x, oa, ob, oc, ng, wd, wa, wb, wc, wo, ng)


def _mem_kv_body(mem_ref, g_ref, w_ref, o_ref):
    o_ref[...] = _dot(_rms(mem_ref[...], g_ref[...]).astype(BF16), w_ref[...]).astype(o_ref.dtype)


def _mem_kv(mem, mem_norm, wkv, l):
    B, M, D = mem.shape
    return pl.pallas_call(
        _mem_kv_body,
        grid=(B,),
        in_specs=[pl.BlockSpec((None, M, D), lambda b: (b, 0, 0)), _vec_spec(D, l),
                  _mat_spec(D, 2 * D, l)],
        out_specs=pl.BlockSpec((None, M, 2 * D), lambda b: (b, 0, 0)),
        out_shape=jax.ShapeDtypeStruct((B, M, 2 * D), BF16),
        compiler_params=_cparams(1),
        name="mem_kv",
    )(mem, mem_norm, wkv)


def _mem_attn_body(x_ref, gpre_ref, wq_ref, kv_ref, wo_ref, gpost_ref, o_ref):
    x = x_ref[...]
    D = D_MODEL
    h = _rms(x, gpre_ref[...]).astype(BF16)
    q = (_dot(h, wq_ref[...]) * MEM_HD ** -0.5).astype(BF16)
    kv = kv_ref[...]
    outs = []
    for hh in range(MEM_HEADS):
        sl = slice(hh * MEM_HD, (hh + 1) * MEM_HD)
        s = _dot_nt(q[:, sl], kv[:, sl])
        p = jnp.exp(s - jnp.max(s, axis=-1, keepdims=True))
        o = _dot(p.astype(BF16), kv[:, D + hh * MEM_HD:D + (hh + 1) * MEM_HD])
        outs.append(o / jnp.sum(p, axis=-1, keepdims=True))
    o = jnp.concatenate(outs, axis=1).astype(BF16)
    o_ref[...] = x + _rms(_dot(o, wo_ref[...]), gpost_ref[...])


def _mem_attn(x, ng, wq, kv, wo, l, S, tm):
    T, D = x.shape
    M = kv.shape[1]
    per_b = S // tm
    return pl.pallas_call(
        _mem_attn_body,
        grid=(T // tm,),
        in_specs=[pl.BlockSpec((tm, D), lambda i: (i, 0)),
                  pl.BlockSpec((None, 1, D), lambda i: (l * 8 + 4, 0, 0)),
                  _mat_spec(D, D, l),
                  pl.BlockSpec((None, M, 2 * D), lambda i: (i // per_b, 0, 0)),
                  _mat_spec(D, D, l),
                  pl.BlockSpec((None, 1, D), lambda i: (l * 8 + 5, 0, 0))],
        out_specs=pl.BlockSpec((tm, D), lambda i: (i, 0)),
        out_shape=jax.ShapeDtypeStruct((T, D), F32),
        compiler_params=_cparams(1),
        name="mem_attn",
    )(x, ng, wq, kv, wo, ng)


def _prepare_params(w_in, mla_w_uq, mla_w_ukv, gla_w_alpha, gla_norm, rwkv_mu, rwkv_w_decay,
                    rwkv_w_iclr, rwkv_w_gate, w_branch):
    L, D = w_in.shape[0], w_in.shape[1]
    zc = lambda n: jnp.zeros((L, D, n), F32)
    o = 0
    cuts = {}
    for name, n in (("c_q", 256), ("c_kv", 128), ("k_rope", 32), ("gla_q", 256), ("gla_k", 256),
                    ("gla_v", 512), ("gla_g", 512), ("gla_a", 16), ("rwkv", 1824), ("gates", 3072)):
        cuts[name] = w_in[:, :, o:o + n]
        o += n
    kr = cuts["k_rope"]
    kr_b = jnp.concatenate([-kr[..., 16:], kr[..., :16]], axis=-1)
    wa = jnp.concatenate([cuts["c_q"], cuts["c_kv"], zc(64), kr, zc(32), zc(64), kr_b, zc(32)], -1)
    wb = jnp.concatenate([cuts["gla_v"], cuts["gla_g"], cuts["gla_q"], cuts["gla_k"],
                          cuts["gla_a"], zc(112)], -1)
    wc = jnp.concatenate([cuts["rwkv"], zc(RWKV_MISC - 288)], -1)
    wd = cuts["gates"]

    wuq = mla_w_uq.reshape(L, MLA_Q_RANK, MLA_HEADS, MLA_NOPE + MLA_ROPE)
    nope, rope = wuq[..., :MLA_NOPE], wuq[..., MLA_NOPE:]
    zq = lambda n: jnp.zeros((L, MLA_Q_RANK, MLA_HEADS, n), F32)
    wqa = jnp.concatenate([nope, rope, zq(32)], -1).reshape(L, MLA_Q_RANK, MLA_W)
    rope_b = jnp.concatenate([-rope[..., 16:], rope[..., :16]], -1)
    wqb = jnp.concatenate([zq(64), rope_b, zq(32)], -1).reshape(L, MLA_Q_RANK, MLA_W)
    wukv = mla_w_ukv.reshape(L, MLA_KV_RANK, MLA_HEADS, 128)
    zk = jnp.zeros((L, MLA_KV_RANK, MLA_HEADS, 64), F32)
    wk = jnp.concatenate([wukv[..., :64], zk], -1).reshape(L, MLA_KV_RANK, MLA_W)
    wv = jnp.concatenate([wukv[..., 64:], zk[..., :MLA_VSLOT - 64]], -1).reshape(
        L, MLA_KV_RANK, MLA_HEADS * MLA_VSLOT)

    bra = w_branch[:, :512].reshape(L, MLA_HEADS, 64, D)
    bra = jnp.concatenate([bra, jnp.zeros_like(bra)], axis=2).reshape(L, MLA_W, D)
    brb = w_branch[:, 512:512 + GLA_W]
    brc = w_branch[:, 512 + GLA_W:]

    walpha = jnp.concatenate(
        [gla_w_alpha, jnp.zeros((L, 128 - GLA_GATE_RANK, GLA_QK), F32)], axis=1)
    gn = jnp.tile(gla_norm, (1, GLA_HEADS))[:, None, :]
    mu = jnp.concatenate([rwkv_mu, jnp.zeros((L, RWKV_MISC - 288), F32)], -1)[:, None, :]
    zr = lambda n: jnp.zeros((L, n, RWKV_W), F32)
    def hi_lo(w):
        hi = w.astype(BF16)
        return jnp.stack([hi, (w - hi.astype(F32)).astype(BF16)], axis=1)

    wdec = hi_lo(jnp.concatenate([rwkv_w_decay, zr(64)], axis=1))
    wiclr = hi_lo(jnp.concatenate([zr(64), rwkv_w_iclr], axis=1))
    wgate = jnp.concatenate([rwkv_w_gate, zr(256 - RWKV_GATE_RANK)], axis=1)
    bf = lambda w: w.astype(BF16)
    return dict(wa=bf(wa), wb=bf(wb), wc=bf(wc), wd=bf(wd), wqa=bf(wqa), wqb=bf(wqb), wk=bf(wk),
                wv=bf(wv), bra=bf(bra), brb=bf(brb), brc=bf(brc), walpha=hi_lo(walpha), gn=gn, mu=mu,
                wdec=wdec, wiclr=wiclr, wgate=bf(wgate))


def kernel(x, mem, positions, norm_g, w_ffn_in, w_ffn_out, w_in, mla_q_norm, mla_w_uq, mla_kv_norm, mla_w_ukv, gla_w_alpha, gla_b_alpha, gla_norm, rwkv_mu, rwkv_w0, rwkv_w_decay, rwkv_a0, rwkv_w_iclr, rwkv_w_gate, rwkv_k_k, rwkv_k_a, rwkv_r_k, rwkv_ln_w, rwkv_ln_b, w_branch, w_out, mem_norm, mem_wq, mem_wkv, mem_wo):
    B, S, D = x.shape
    L = norm_g.shape[0]
    T = B * S
    tm = min(512, S)
    tq = min(1024, S)
    tk = tq // 2
    ts = min(256, S)
    tff = min(1024, S)
    nsp = 11

    pp = _prepare_params(w_in, mla_w_uq, mla_w_ukv, gla_w_alpha, gla_norm, rwkv_mu, rwkv_w_decay,
                         rwkv_w_iclr, rwkv_w_gate, w_branch)
    bf = lambda w: w.astype(BF16)
    ffn_in, ffn_out = bf(w_ffn_in), bf(w_ffn_out)
    wout, wq, wkv, wo = bf(w_out), bf(mem_wq), bf(mem_wkv), bf(mem_wo)
    ng = norm_g.reshape(L * 8, 1, D)
    row = lambda p: p[:, None, :]
    gl = dict(walpha=pp["walpha"], balpha=row(gla_b_alpha), gn=pp["gn"])
    rw = dict(mu=pp["mu"], w0=row(rwkv_w0), wdec=pp["wdec"], a0=row(rwkv_a0), wiclr=pp["wiclr"],
              wgate=pp["wgate"], k_k=row(rwkv_k_k), k_a=row(rwkv_k_a), r_k=row(rwkv_r_k),
              ln_w=row(rwkv_ln_w), ln_b=row(rwkv_ln_b))

    inv_freq = ROPE_THETA ** (-jnp.arange(0, MLA_ROPE, 2, dtype=F32) / MLA_ROPE)
    invf_lane = jnp.concatenate(
        [jnp.zeros((MLA_NOPE,), F32), inv_freq, inv_freq, jnp.zeros((32,), F32)])[None, :]
    c0, s0 = _rope_tables(positions.astype(F32).reshape(T, 1), invf_lane, tm)

    x = x.reshape(T, D)
    for l in range(L):
        x = _ffn(x, ng, ffn_in, ffn_out, l, 0, tff, nsp)
        za, zb, zc = _normproj(x, ng, l * 8 + 2, [pp["wa"], pp["wb"], pp["wc"]], l, tm)
        q, k, vt = _mla_prep(za, c0, s0, row(mla_q_norm), row(mla_kv_norm), pp["wqa"], pp["wqb"],
                             pp["wk"], pp["wv"], l, tk)
        o_a = _mla_attn(q, k, vt, B, S, tq, tk)
        o_b, o_c = _recur(zb, zc, gl, rw, l, B, S, ts)
        x = _merge(x, o_a, o_b, o_c, pp["wd"], pp["bra"], pp["brb"], pp["brc"], wout, ng, l, tm)
        kv = _mem_kv(mem, row(mem_norm), wkv, l)
        x = _mem_attn(x, ng, wq, kv, wo, l, S, tm)
        x = _ffn(x, ng, ffn_in, ffn_out, l, 1, tff, nsp)
    return x.reshape(B, S, D)
```

```python
import functools

import jax
import jax.numpy as jnp
import numpy as np
from jax import lax
from jax.experimental import pallas as pl
from jax.experimental.pallas import tpu as pltpu

F32 = jnp.float32
BF16 = jnp.bfloat16
HI = lax.Precision.HIGHEST

D_MODEL = 1024
D_FF = 2816
NORM_EPS = 1e-6
LOG2_E = 1.4426950408889634
MLA_HEADS = 8
MLA_NOPE = 64
MLA_ROPE = 32
MLA_Q_RANK = 256
MLA_KV_RANK = 128
ROPE_THETA = 10000.0
MLA_SLOT = 128
MLA_W = MLA_HEADS * MLA_SLOT
MLA_HPS = 2
MLA_VSLOT = 80
MLA_ONE = 64
GLA_HEADS = 4
GLA_DK = 64
GLA_DV = 128
GLA_GATE_RANK = 16
GLA_GATE_NORM = 16.0
GLA_QK = GLA_HEADS * GLA_DK
GLA_W = GLA_HEADS * GLA_DV
GLA_LEVELS = 6
GLA_NEG = -1e30
RWKV_HEADS = 8
RWKV_N = 64
RWKV_DECAY_RANK = 64
RWKV_ICLR_RANK = 64
RWKV_GATE_RANK = 160
RWKV_GN_EPS = 64e-5
RWKV_W = RWKV_HEADS * RWKV_N
RWKV_MISC = 384
RWKV_COLS_PAD = 3 * RWKV_W + RWKV_MISC
RWKV_QUAD = 4 * RWKV_N
RWKV_NQ = RWKV_W // RWKV_QUAD
REC_CPI = 2
MEM_HEADS = 4
MEM_HD = D_MODEL // MEM_HEADS
N_BRANCH = 3
MERGE_COLS = 256
CHUNK = 64

VMEM_LIMIT_BYTES = 56 * 1024 * 1024


def _cparams(n_axes, flags=None):
    return pltpu.CompilerParams(
        dimension_semantics=("arbitrary",) * n_axes,
        vmem_limit_bytes=VMEM_LIMIT_BYTES,
        flags=flags,
    )


def _dot(a, b, precision=None):
    return jnp.dot(a, b, preferred_element_type=F32, precision=precision)


def _dot_nt(a, b, precision=None):
    return lax.dot_general(a, b, (((1,), (1,)), ((), ())),
                           preferred_element_type=F32, precision=precision)


def _dot_tn(a, b, precision=None):
    return lax.dot_general(a, b, (((0,), (0,)), ((), ())),
                           preferred_element_type=F32, precision=precision)


def _rms(x, g, eps=NORM_EPS):
    return x * lax.rsqrt(jnp.mean(x * x, axis=-1, keepdims=True) + eps) * g


def _sigmoid(x):
    return 1.0 / (1.0 + jnp.exp(-x))


def _softplus(x):
    return jnp.maximum(x, 0.0) + jnp.log(1.0 + jnp.exp(-jnp.abs(x)))


def _split2(x):
    hi = x.astype(BF16)
    return hi, (x - hi.astype(F32)).astype(BF16)


def _dot_split(x, w_hi, w_lo):
    hi, lo = _split2(x)
    m = x.shape[0]
    top = _dot(jnp.concatenate([hi, lo], axis=0), w_hi)
    return top[:m] + top[m:] + _dot(hi, w_lo)


def _dot_01(m01_bf16, x):
    p1 = x.astype(BF16)
    r1 = x - p1.astype(F32)
    p2 = r1.astype(BF16)
    p3 = (r1 - p2.astype(F32)).astype(BF16)
    n = x.shape[1]
    out = _dot(m01_bf16, jnp.concatenate([p1, p2, p3], axis=1))
    return out[:, :n] + out[:, n:2 * n] + out[:, 2 * n:]


def _iota(shape, dim):
    return lax.broadcasted_iota(jnp.int32, shape, dim)


def _segsum(x, e_bf16):
    hi = x.astype(BF16)
    lo = (x - hi.astype(F32)).astype(BF16)
    return _dot(hi, e_bf16) + _dot(lo, e_bf16)


def _vec_spec(n, l):
    return pl.BlockSpec((None, 1, n), lambda *_: (l, 0, 0))


def _mat_spec(r, c, l):
    return pl.BlockSpec((None, r, c), lambda *_: (l, 0, 0))


def _ffn_body(x_ref, gpre_ref, wi_ref, wo_ref, gpost_ref, o_ref, *, n_split):
    x = x_ref[...]
    h = _rms(x, gpre_ref[...]).astype(BF16)
    ff = wo_ref.shape[0]
    tf = ff // n_split
    y = None
    for j in range(n_split):
        g = _dot(h, wi_ref[:, j * tf:(j + 1) * tf])
        u = _dot(h, wi_ref[:, ff + j * tf:ff + (j + 1) * tf])
        act = (g * _sigmoid(g) * u).astype(BF16)
        part = _dot(act, wo_ref[j * tf:(j + 1) * tf, :])
        y = part if y is None else y + part
    o_ref[...] = x + 0.5 * _rms(y, gpost_ref[...])


def _ffn(x, ng, w_in, w_out, l, k, tm, n_split):
    T, D = x.shape
    ff = w_out.shape[2]
    g_pre, g_post = l * 8 + 6 * k, l * 8 + 6 * k + 1
    return pl.pallas_call(
        functools.partial(_ffn_body, n_split=n_split),
        grid=(T // tm,),
        in_specs=[
            pl.BlockSpec((tm, D), lambda i: (i, 0)),
            pl.BlockSpec((None, 1, D), lambda i: (g_pre, 0, 0)),
            pl.BlockSpec((None, None, D, 2 * ff), lambda i: (l, k, 0, 0),
                         pipeline_mode=pl.Buffered(1)),
            pl.BlockSpec((None, None, ff, D), lambda i: (l, k, 0, 0), pipeline_mode=pl.Buffered(1)),
            pl.BlockSpec((None, 1, D), lambda i: (g_post, 0, 0)),
        ],
        out_specs=pl.BlockSpec((tm, D), lambda i: (i, 0)),
        out_shape=jax.ShapeDtypeStruct((T, D), F32),
        compiler_params=_cparams(1),
        name="ffn",
    )(x, ng, w_in, w_out, ng)


def _normproj_body(x_ref, g_ref, *refs):
    n = len(refs) // 2
    h = _rms(x_ref[...], g_ref[...]).astype(BF16)
    for w_ref, o_ref in zip(refs[:n], refs[n:]):
        o_ref[...] = _dot(h, w_ref[...]).astype(o_ref.dtype)


def _normproj(x, ng, g_idx, ws, l, tm):
    T, D = x.shape
    widths = [w.shape[2] for w in ws]
    return pl.pallas_call(
        _normproj_body,
        grid=(T // tm,),
        in_specs=[pl.BlockSpec((tm, D), lambda i: (i, 0)),
                  pl.BlockSpec((None, 1, D), lambda i: (g_idx, 0, 0))]
        + [pl.BlockSpec((None, D, n), lambda i: (l, 0, 0), pipeline_mode=pl.Buffered(1))
           for n in widths],
        out_specs=[pl.BlockSpec((tm, n), lambda i: (i, 0)) for n in widths],
        out_shape=[jax.ShapeDtypeStruct((T, n), F32) for n in widths],
        compiler_params=_cparams(1),
        name="mixer_proj",
    )(x, ng, *ws)


def _rope_table_body(pos_ref, invf_ref, c_ref, s_ref):
    ang = pos_ref[...] * invf_ref[...]
    lane = _iota(ang.shape, 1)
    rot = (lane >= MLA_NOPE) & (lane < MLA_NOPE + MLA_ROPE)
    c_ref[...] = jnp.where(lane < MLA_NOPE, 1.0, jnp.where(rot, jnp.cos(ang), 0.0))
    s_ref[...] = jnp.where(rot, jnp.sin(ang), 0.0)


def _rope_tables(pos_f32, invf_lane, tm):
    T = pos_f32.shape[0]
    return pl.pallas_call(
        _rope_table_body,
        grid=(T // tm,),
        in_specs=[pl.BlockSpec((tm, 1), lambda i: (i, 0)),
                  pl.BlockSpec((1, MLA_SLOT), lambda i: (0, 0))],
        out_specs=[pl.BlockSpec((tm, MLA_SLOT), lambda i: (i, 0))] * 2,
        out_shape=[jax.ShapeDtypeStruct((T, MLA_SLOT), F32)] * 2,
        compiler_params=_cparams(1),
        name="rope_tables",
    )(pos_f32, invf_lane)


def _mla_prep_body(za_ref, c0_ref, s0_ref, qn_ref, kvn_ref, wqa_ref, wqb_ref, wk_ref, wv_ref,
                   one_ref, q_ref, k_ref, vt_ref):
    za = za_ref[...]
    cq = za[:, :MLA_Q_RANK]
    ckv = za[:, MLA_Q_RANK:MLA_Q_RANK + MLA_KV_RANK]
    kra = za[:, 384:512]
    krb = za[:, 512:640]
    cqn = _rms(cq, qn_ref[...]).astype(BF16)
    ckvn = _rms(ckv, kvn_ref[...]).astype(BF16)
    c0 = c0_ref[...]
    s0 = s0_ref[...]
    c8 = jnp.concatenate([c0] * MLA_HEADS, axis=1)
    s8 = jnp.concatenate([s0] * MLA_HEADS, axis=1)
    scale = (MLA_NOPE + MLA_ROPE) ** -0.5 * LOG2_E
    q = (_dot(cqn, wqa_ref[...]) * c8 + _dot(cqn, wqb_ref[...]) * s8) * scale
    q_ref[...] = q.astype(BF16)
    krot = kra * c0 + krb * s0
    k = _dot(ckvn, wk_ref[...]) + jnp.concatenate([krot] * MLA_HEADS, axis=1)
    k_ref[...] = k.astype(BF16)
    v = _dot(ckvn, wv_ref[...]) + one_ref[...]
    vt_ref[...] = v.T.astype(BF16)


def _mla_prep(za, c0, s0, qn, kvn, wqa, wqb, wk, wv, l, tk):
    T = za.shape[0]
    vw = MLA_HEADS * MLA_VSLOT
    tok = lambda n: pl.BlockSpec((tk, n), lambda i: (i, 0))
    one_lane = jnp.asarray((np.arange(vw) % MLA_VSLOT == MLA_ONE).astype(np.float32))[None, :]
    return pl.pallas_call(
        _mla_prep_body,
        grid=(T // tk,),
        in_specs=[tok(za.shape[1]), tok(MLA_SLOT), tok(MLA_SLOT),
                  _vec_spec(MLA_Q_RANK, l), _vec_spec(MLA_KV_RANK, l),
                  _mat_spec(MLA_Q_RANK, MLA_W, l), _mat_spec(MLA_Q_RANK, MLA_W, l),
                  _mat_spec(MLA_KV_RANK, MLA_W, l), _mat_spec(MLA_KV_RANK, vw, l),
                  pl.BlockSpec((1, vw), lambda i: (0, 0))],
        out_specs=[tok(MLA_W), tok(MLA_W),
                   pl.BlockSpec((None, vw, tk), lambda i: (i, 0, 0))],
        out_shape=[jax.ShapeDtypeStruct((T, MLA_W), BF16)] * 2
        + [jax.ShapeDtypeStruct((T // tk, vw, tk), BF16)],
        compiler_params=_cparams(1),
        name="mla_prep",
    )(za, c0, s0, qn, kvn, wqa, wqb, wk, wv, one_lane)


def _mla_attn_body(q_ref, k_ref, vt_ref, o_ref, s_ref, m_ref, acc_ref, *, tq, tk):
    qi = pl.program_id(2)
    m_ref[...] = jnp.full_like(m_ref, -1e30)
    acc_ref[...] = jnp.zeros_like(acc_ref)
    heads = [slice(hh * MLA_SLOT, (hh + 1) * MLA_SLOT) for hh in range(MLA_HPS)]

    def produce(slot, ki):
        r0 = pl.multiple_of(ki * tk, tk)
        for hh, sl in enumerate(heads):
            s_ref[slot, hh] = _dot_nt(k_ref[pl.ds(r0, tk), sl], q_ref[:, sl])

    def consume(slot, ki, diag):
        for hh, sl in enumerate(heads):
            st = s_ref[slot, hh]
            if diag is not None:
                key = _iota(st.shape, 0) + diag * tk
                st = jnp.where(key <= _iota(st.shape, 1), st, -1e30)
            m_old = m_ref[hh]
            m_new = jnp.maximum(m_old, jnp.max(st, axis=0, keepdims=True))
            p = jnp.exp2(st - m_new).astype(BF16)
            vt = vt_ref[ki, hh * MLA_VSLOT:(hh + 1) * MLA_VSLOT, :]
            acc_ref[hh] = jnp.exp2(m_old - m_new) * acc_ref[hh] + _dot(vt, p)
            m_ref[hh] = m_new

    def body(j, carry):
        produce(1, 2 * j + 1)
        consume(0, 2 * j, None)
        produce(0, 2 * j + 2)
        consume(1, 2 * j + 1, None)
        return carry

    produce(0, 0)
    lax.fori_loop(0, qi, body, 0)
    produce(1, 2 * qi + 1)
    consume(0, 2 * qi, 0)
    consume(1, 2 * qi + 1, 1)
    for hh in range(MLA_HPS):
        acc = acc_ref[hh]
        o_t = acc / acc[MLA_ONE:MLA_ONE + 1, :]
        o_t = jnp.concatenate([o_t, jnp.zeros((MLA_SLOT - MLA_VSLOT, tq), F32)], axis=0)
        o_ref[:, hh * MLA_SLOT:(hh + 1) * MLA_SLOT] = o_t.T.astype(o_ref.dtype)


def _mla_attn(q, k, vt, B, S, tq, tk):
    assert tq == 2 * tk
    T = q.shape[0]
    nq, nk = S // tq, S // tk
    w = MLA_HPS * MLA_SLOT
    return pl.pallas_call(
        functools.partial(_mla_attn_body, tq=tq, tk=tk),
        grid=(B, MLA_HEADS // MLA_HPS, nq),
        in_specs=[
            pl.BlockSpec((tq, w), lambda b, h, i: (b * nq + i, h)),
            pl.BlockSpec((S, w), lambda b, h, i: (b, h)),
            pl.BlockSpec((nk, MLA_HPS * MLA_VSLOT, tk), lambda b, h, i: (b, h, 0)),
        ],
        out_specs=pl.BlockSpec((tq, w), lambda b, h, i: (b * nq + i, h)),
        out_shape=jax.ShapeDtypeStruct((T, MLA_W), BF16),
        scratch_shapes=[pltpu.VMEM((2, MLA_HPS, tk, tq), F32),
                        pltpu.VMEM((MLA_HPS, 1, tq), F32),
                        pltpu.VMEM((MLA_HPS, MLA_VSLOT, tq), F32)],
        compiler_params=_cparams(3),
        name="mla_attn",
    )(q, k, vt)


def _gla_pair_masks():
    i = np.arange(GLA_HEADS * CHUNK)[:, None] % CHUNK
    j = np.arange(CHUNK)[None, :]
    same = [(i >> (6 - lv)) == (j >> (6 - lv)) for lv in range(GLA_LEVELS)]
    return jnp.asarray(np.stack(same + [i == j]).astype(np.float32))


def _gla_stages(v_ref, g_ref, q_ref, k_ref, a_ref, walpha_ref, balpha_ref, gn_ref, pair_ref,
                o_ref, ht_ref, n_batch):
    C = CHUNK
    walpha_hi, walpha_lo = walpha_ref[0], walpha_ref[1]
    balpha = balpha_ref[...]
    gn = gn_ref[...]
    tri = (_iota((C, C), 1) <= _iota((C, C), 0)).astype(BF16)
    st_r, st_c = _iota((GLA_HEADS * C, GLA_QK), 0), _iota((GLA_HEADS * C, GLA_QK), 1)
    head_mask = ((st_r >> 6) == (st_c >> 6)).astype(BF16)
    ht_r, ht_c = _iota((GLA_W, GLA_QK), 0), _iota((GLA_W, GLA_QK), 1)
    state_mask = (ht_r >> 7) == (ht_c >> 6)
    tok = _iota((C, GLA_QK), 0)
    halves = [C >> (lv + 1) for lv in range(GLA_LEVELS)]

    def stack(x):
        return jnp.concatenate([x.astype(BF16)] * GLA_HEADS, axis=0) * head_mask

    def level_operands(q, k, cb):
        last = {1: cb}
        for s in halves[:0:-1]:
            f = last[s]
            last[2 * s] = jnp.where((tok & s) != 0, f, pltpu.roll(f, C - s, 0))
        ops = []
        for hs in halves:
            f = last[hs]
            right = (tok & hs) != 0
            d = cb - jnp.where(right, pltpu.roll(f, hs, 0), f)
            ql = q * jnp.exp(jnp.where(right, d, GLA_NEG))
            kl = k * jnp.exp(jnp.where(right, GLA_NEG, -d))
            ops.append((stack(ql), kl.astype(BF16)))
        ops.append((stack(q), k.astype(BF16)))
        return ops

    def stages(c):
        cbs = [(b, pl.ds(pl.multiple_of((c * REC_CPI + sub) * C, C), C))
               for sub in range(REC_CPI) for b in range(n_batch)]
        xs = [_dot_split(a_ref[b, rows, :], walpha_hi, walpha_lo) + balpha for b, rows in cbs]
        yield
        cum = [_dot_01(tri, -_softplus(-x) * (1.0 / GLA_GATE_NORM)) for x in xs]
        yield
        st = []
        for (b, rows), cb in zip(cbs, cum):
            b_last = cb[C - 1:C, :]
            q = q_ref[b, rows, :] * GLA_DK ** -0.5
            k = k_ref[b, rows, :]
            st.append(dict(
                v=v_ref[b, rows, :].astype(BF16),
                qe=(q * jnp.exp(cb)).astype(BF16),
                ke=(k * jnp.exp(b_last - cb)).astype(BF16),
                gam=jnp.exp(b_last),
                ops=level_operands(q, k, cb)))
        scs = [None] * len(cbs)
        for lv in range(GLA_LEVELS + 1):
            for i, s in enumerate(st):
                ql, kl = s["ops"][lv]
                part = _dot_nt(ql, kl) * pair_ref[lv]
                scs[i] = part if lv == 0 else scs[i] + part
            yield
        osts = [_dot(sc.astype(BF16), s["v"]) for sc, s in zip(scs, st)]
        yield
        for (b, rows), s, ost in zip(cbs, st, osts):
            o_intra = jnp.concatenate(
                [ost[h * C:(h + 1) * C, h * GLA_DV:(h + 1) * GLA_DV] for h in range(GLA_HEADS)],
                axis=1)
            ht = ht_ref[b]
            o = o_intra + _dot_nt(s["qe"], ht.astype(BF16))
            ht_ref[b] = ht * s["gam"] + jnp.where(state_mask, _dot_tn(s["v"], s["ke"]), 0.0)
            outs = []
            for h in range(GLA_HEADS):
                oh = o[:, h * GLA_DV:(h + 1) * GLA_DV]
                outs.append(oh * lax.rsqrt(jnp.mean(oh * oh, axis=-1, keepdims=True) + NORM_EPS))
            on = jnp.concatenate(outs, axis=1) * gn
            g = g_ref[b, rows, :]
            o_ref[b, rows, :] = (on * (g * _sigmoid(g))).astype(o_ref.dtype)
            yield

    return stages


def _rwkv_stages(z_ref, mu_ref, w0_ref, wdec_ref, a0_ref, wiclr_ref, wgate_ref, kk_ref, ka_ref,
                 rk_ref, lnw_ref, lnb_ref, o_ref, h_ref, zlast_ref, n_batch):
    C, W, Q = CHUNK, RWKV_W, RWKV_QUAD
    mu = mu_ref[...]
    w0, a0 = w0_ref[...], a0_ref[...]
    wdec_hi, wdec_lo = wdec_ref[0], wdec_ref[1]
    wiclr_hi, wiclr_lo = wiclr_ref[0], wiclr_ref[1]
    wgate = wgate_ref[...]
    k_k, k_a, r_k = kk_ref[...], ka_ref[...], rk_ref[...]
    ln_w, ln_b = lnw_ref[...], lnb_ref[...]

    tri = (_iota((C, C), 1) <= _iota((C, C), 0)).astype(BF16)
    sq_r, sq_c = _iota((Q, Q), 0), _iota((Q, Q), 1)
    head_blk = (sq_r >> 6) == (sq_c >> 6)
    eye_q = sq_r == sq_c
    e_seg = head_blk.astype(BF16)
    wd_t, wd_s = _iota((C, Q), 0), _iota((C, Q), 1) & (C - 1)
    strict = wd_s < wd_t
    incl = wd_s <= wd_t
    eye_wide = (wd_s == wd_t).astype(F32)
    row0 = _iota((C, RWKV_COLS_PAD), 0) == 0

    def bd(x):
        return jnp.concatenate([x.astype(BF16)] * 4, axis=0) * e_seg

    def quad(x, i):
        return x[:, i * Q:(i + 1) * Q]

    def segsum(x):
        return jnp.concatenate([_segsum(quad(x, i), e_seg) for i in range(RWKV_NQ)], axis=1)

    def chunk_one(b, rows):
        z = z_ref[b, rows, :]
        zp = jnp.where(row0, zlast_ref[b, 0:1, :], pltpu.roll(z, 1, 0))
        zlast_ref[b, 0:1, :] = z[C - 1:C, :]
        z = z + mu * (zp - z)
        r = z[:, 0:W]
        k = z[:, W:2 * W]
        v = z[:, 2 * W:3 * W]
        m0 = z[:, 3 * W:3 * W + 128]
        m12 = z[:, 3 * W + 128:3 * W + 384]
        w_log = -_softplus(-(w0 + _dot_split(jnp.tanh(m0), wdec_hi, wdec_lo))) - 0.5
        lw = -jnp.exp(w_log)
        a = _sigmoid(a0 + _dot_split(m0, wiclr_hi, wiclr_lo))
        g = _dot(_sigmoid(m12).astype(BF16), wgate)
        kk = k * k_k
        kkn = kk * lax.rsqrt(jnp.maximum(segsum(kk * kk), 1e-24))
        k2 = k * (1.0 + (a - 1.0) * k_a)
        beta = kkn * a
        cs = _dot_01(tri, lw)
        c_last = cs[C - 1:C, :]
        dec_in = jnp.exp(-cs)
        dec_out = jnp.exp(c_last - cs)
        kt = kkn * jnp.exp(cs - lw)
        rt = r * jnp.exp(cs)
        bh = beta * dec_in
        kh = k2 * dec_in
        kbar = k2 * dec_out
        bbar = beta * dec_out
        gam = jnp.exp(c_last)
        chains = [dict(b=b, i=i, kt=quad(kt, i), rt=quad(rt, i), v=quad(v, i), bh=quad(bh, i),
                       kh=quad(kh, i), kbar=quad(kbar, i), bbar=quad(bbar, i), gam=quad(gam, i))
                  for i in range(RWKV_NQ)]
        return dict(r=r, k2=k2, v=v, g=g), chains

    def epilogue(b, rows, tok, y):
        mean = segsum(y) * (1.0 / RWKV_N)
        d = y - mean
        var = segsum(d * d) * (1.0 / RWKV_N)
        yn = d * lax.rsqrt(var + RWKV_GN_EPS) * ln_w + ln_b
        bonus = segsum(tok["r"] * tok["k2"] * r_k) * tok["v"]
        o_ref[b, rows, :] = ((yn + bonus) * tok["g"]).astype(o_ref.dtype)

    def stages(c):
        groups, chains = [], []
        for sub in range(REC_CPI):
            rows = pl.ds(pl.multiple_of((c * REC_CPI + sub) * C, C), C)
            for b in range(n_batch):
                tok, ch = chunk_one(b, rows)
                groups.append((b, rows, tok, ch))
                chains += ch
                yield
        for ch in chains:
            lhs = jnp.concatenate([ch["kt"], ch["rt"]], axis=0).astype(BF16)
            rhs = jnp.concatenate([bd(ch["bh"]), bd(ch["kh"])], axis=0)
            sc = _dot_nt(lhs, rhs)
            ch["a_kk"] = jnp.where(strict, sc[:C, Q:], 0.0)
            ch["a_rb"] = jnp.where(incl, sc[C:, :Q], 0.0)
            ch["a_rk"] = jnp.where(incl, sc[C:, Q:], 0.0)
            ch["n"] = -jnp.where(strict, sc[:C, :Q], 0.0)
            ch["t"] = eye_wide + ch["n"]
        yield
        for ch in chains:
            ch["p"] = _dot(ch["n"].astype(BF16), bd(ch["n"]))
            ch["bd_v"] = bd(ch["v"])
            ch["akk_v"] = _dot(ch["a_kk"].astype(BF16), ch["bd_v"])
        yield
        for _ in range(4):
            for ch in chains:
                tp = _dot(jnp.concatenate([ch["t"], ch["p"]], axis=0).astype(BF16), bd(ch["p"]))
                ch["t"] = ch["t"] + tp[:C]
                ch["p"] = tp[C:]
            yield
        for ch in chains:
            ch["t"] = ch["t"] + _dot(ch["t"].astype(BF16), bd(ch["p"]))
        yield
        for ch in chains:
            w12 = _dot(ch["t"].astype(BF16),
                       jnp.concatenate([bd(ch["akk_v"]), bd(ch["kt"])], axis=1))
            ch["w1"], ch["w2"] = w12[:, :Q], w12[:, Q:]
        yield
        for ch in chains:
            w1, w2, bbar = ch["w1"], ch["w2"], ch["bbar"]
            ch["y0"] = _dot(jnp.concatenate([ch["a_rk"], ch["a_rb"]], axis=1).astype(BF16),
                            jnp.concatenate([ch["bd_v"], bd(-w1)], axis=0))
            ch["rp"] = (ch["rt"] - _dot(ch["a_rb"].astype(BF16), bd(w2))).astype(BF16)
            ch["p_bd"] = (jnp.where(eye_q, ch["gam"], 0.0) - jnp.where(
                head_blk, _dot_tn(bbar.astype(BF16), w2.astype(BF16)), 0.0)).astype(BF16)
            ch["q_bd"] = jnp.where(
                head_blk,
                _dot_tn(jnp.concatenate([ch["kbar"], -bbar], axis=0).astype(BF16),
                        jnp.concatenate([ch["v"], w1], axis=0).astype(BF16)), 0.0)
        yield
        for b, rows, tok, chs in groups:
            ys = []
            for ch in chs:
                hb = h_ref[b, ch["i"]].astype(BF16)
                ys.append(_dot(ch["rp"], hb) + ch["y0"])
                h_ref[b, ch["i"]] = _dot(ch["p_bd"], hb) + ch["q_bd"]
            epilogue(b, rows, tok, jnp.concatenate(ys, axis=1))
            yield

    return stages


N_GLA_IN, N_RWKV_IN = 9, 12


def _recur_body(*refs, n_chunks, n_batch):
    gla_in = refs[:N_GLA_IN]
    rwkv_in = refs[N_GLA_IN:N_GLA_IN + N_RWKV_IN]
    ob_ref, oc_ref, ht_ref, h_ref, zlast_ref = refs[N_GLA_IN + N_RWKV_IN:]

    @pl.when(pl.program_id(0) == 0)
    def _():
        ht_ref[...] = jnp.zeros_like(ht_ref)
        h_ref[...] = jnp.zeros_like(h_ref)
        zlast_ref[...] = jnp.zeros_like(zlast_ref)

    gla = _gla_stages(*gla_in, ob_ref, ht_ref, n_batch)
    rwkv = _rwkv_stages(*rwkv_in, oc_ref, h_ref, zlast_ref, n_batch)

    def chunk(c, carry):
        live = [rwkv(c), gla(c)]
        while live:
            live = [g for g in live if next(g, True) is None]
        return carry

    lax.fori_loop(0, n_chunks // REC_CPI, chunk, 0)


def _recur(zb, zc, gp, rp, l, B, S, ts):
    ns = S // ts
    W, Q = RWKV_W, RWKV_QUAD
    blk = lambda n, j: pl.BlockSpec((B, ts, n), lambda s: (0, s, j))
    split = lambda c: pl.BlockSpec((None, 2, 128, c), lambda s: (l, 0, 0, 0))
    zb = zb.reshape(B, S, zb.shape[1])
    o_b, o_c = pl.pallas_call(
        functools.partial(_recur_body, n_chunks=ts // CHUNK, n_batch=B),
        grid=(ns,),
        in_specs=[
            blk(GLA_W, 0),
            blk(GLA_W, 1),
            blk(GLA_QK, 4),
            blk(GLA_QK, 5),
            blk(128, 12),
            split(GLA_QK), _vec_spec(GLA_QK, l), _vec_spec(GLA_W, l),
            pl.BlockSpec((GLA_LEVELS + 1, GLA_HEADS * CHUNK, CHUNK), lambda s: (0, 0, 0)),
            blk(RWKV_COLS_PAD, 0),
            _vec_spec(RWKV_COLS_PAD, l), _vec_spec(W, l), split(W),
            _vec_spec(W, l), split(W), _mat_spec(256, W, l),
            _vec_spec(W, l), _vec_spec(W, l), _vec_spec(W, l), _vec_spec(W, l), _vec_spec(W, l),
        ],
        out_specs=[blk(GLA_W, 0), blk(W, 0)],
        out_shape=[jax.ShapeDtypeStruct((B, S, GLA_W), BF16), jax.ShapeDtypeStruct((B, S, W), BF16)],
        scratch_shapes=[pltpu.VMEM((B, GLA_W, GLA_QK), F32),
                        pltpu.VMEM((B, RWKV_NQ, Q, Q), F32),
                        pltpu.VMEM((B, 8, RWKV_COLS_PAD), F32)],
        compiler_params=_cparams(1),
        name="recurrences",
    )(zb, zb, zb, zb, zb, gp["walpha"], gp["balpha"], gp["gn"], _gla_pair_masks(),
      zc.reshape(B, S, RWKV_COLS_PAD), rp["mu"], rp["w0"], rp["wdec"], rp["a0"], rp["wiclr"],
      rp["wgate"], rp["k_k"], rp["k_a"], rp["r_k"], rp["ln_w"], rp["ln_b"])
    return o_b.reshape(B * S, GLA_W), o_c.reshape(B * S, W)


def _merge_body(x_ref, oa_ref, ob_ref, oc_ref, gpre_ref, wd_ref, wa_ref, wb_ref, wc_ref, wo_ref,
                gpost_ref, o_ref):
    x = x_ref[...]
    D = D_MODEL
    h = _rms(x, gpre_ref[...]).astype(BF16)
    branches = ((oa_ref[...], wa_ref), (ob_ref[...], wb_ref), (oc_ref[...], wc_ref))
    y = None
    for c in range(0, D, MERGE_COLS):
        cols = slice(c, c + MERGE_COLS)
        merged = None
        for j, (o, w_r) in enumerate(branches):
            gate = _sigmoid(_dot(h, wd_ref[:, j * D + c:j * D + c + MERGE_COLS]))
            term = gate * _dot(o, w_r[:, cols])
            merged = term if merged is None else merged + term
        part = _dot(merged.astype(BF16), wo_ref[cols, :])
        y = part if y is None else y + part
    o_ref[...] = x + _rms(y, gpost_ref[...])


def _merge(x, oa, ob, oc, wd, wa, wb, wc, wo, ng, l, tm):
    T, D = x.shape
    tok = lambda n: pl.BlockSpec((tm, n), lambda i: (i, 0))
    res = lambda r, c: pl.BlockSpec((None, r, c), lambda i: (l, 0, 0), pipeline_mode=pl.Buffered(1))
    gain = lambda j: pl.BlockSpec((None, 1, D), lambda i: (l * 8 + j, 0, 0))
    return pl.pallas_call(
        _merge_body,
        grid=(T // tm,),
        in_specs=[tok(D), tok(MLA_W), tok(GLA_W), tok(RWKV_W), gain(2), res(D, N_BRANCH * D),
                  res(MLA_W, D), res(GLA_W, D), res(RWKV_W, D), res(D, D), gain(3)],
        out_specs=tok(D),
        out_shape=jax.ShapeDtypeStruct((T, D), F32),
        compiler_params=_cparams(1),
        name="merge",
    )(x, oa, ob, oc, ng, wd, wa, wb, wc, wo, ng)


def _mem_kv_body(mem_ref, g_ref, w_ref, o_ref):
    o_ref[...] = _dot(_rms(mem_ref[...], g_ref[...]).astype(BF16), w_ref[...]).astype(o_ref.dtype)


def _mem_kv(mem, mem_norm, wkv, l):
    B, M, D = mem.shape
    return pl.pallas_call(
        _mem_kv_body,
        grid=(B,),
        in_specs=[pl.BlockSpec((None, M, D), lambda b: (b, 0, 0)), _vec_spec(D, l),
                  _mat_spec(D, 2 * D, l)],
        out_specs=pl.BlockSpec((None, M, 2 * D), lambda b: (b, 0, 0)),
        out_shape=jax.ShapeDtypeStruct((B, M, 2 * D), BF16),
        compiler_params=_cparams(1),
        name="mem_kv",
    )(mem, mem_norm, wkv)


def _mem_attn_body(x_ref, gpre_ref, wq_ref, kv_ref, wo_ref, gpost_ref, o_ref):
    x = x_ref[...]
    D = D_MODEL
    h = _rms(x, gpre_ref[...]).astype(BF16)
    q = (_dot(h, wq_ref[...]) * (MEM_HD ** -0.5 * LOG2_E)).astype(BF16)
    kv = kv_ref[...]
    outs = []
    for hh in range(MEM_HEADS):
        sl = slice(hh * MEM_HD, (hh + 1) * MEM_HD)
        s = _dot_nt(q[:, sl], kv[:, sl])
        p = jnp.exp2(s - jnp.max(s, axis=-1, keepdims=True))
        o = _dot(p.astype(BF16), kv[:, D + hh * MEM_HD:D + (hh + 1) * MEM_HD])
        outs.append(o / jnp.sum(p, axis=-1, keepdims=True))
    o = jnp.concatenate(outs, axis=1).astype(BF16)
    o_ref[...] = x + _rms(_dot(o, wo_ref[...]), gpost_ref[...])


def _mem_attn(x, ng, wq, kv, wo, l, S, tm):
    T, D = x.shape
    M = kv.shape[1]
    per_b = S // tm
    return pl.pallas_call(
        _mem_attn_body,
        grid=(T // tm,),
        in_specs=[pl.BlockSpec((tm, D), lambda i: (i, 0)),
                  pl.BlockSpec((None, 1, D), lambda i: (l * 8 + 4, 0, 0)),
                  _mat_spec(D, D, l),
                  pl.BlockSpec((None, M, 2 * D), lambda i: (i // per_b, 0, 0)),
                  _mat_spec(D, D, l),
                  pl.BlockSpec((None, 1, D), lambda i: (l * 8 + 5, 0, 0))],
        out_specs=pl.BlockSpec((tm, D), lambda i: (i, 0)),
        out_shape=jax.ShapeDtypeStruct((T, D), F32),
        compiler_params=_cparams(1),
        name="mem_attn",
    )(x, ng, wq, kv, wo, ng)


def _prepare_params(w_in, mla_w_uq, mla_w_ukv, gla_w_alpha, gla_norm, rwkv_mu, rwkv_w_decay,
                    rwkv_w_iclr, rwkv_w_gate, w_branch):
    L, D = w_in.shape[0], w_in.shape[1]
    zc = lambda n: jnp.zeros((L, D, n), F32)
    o = 0
    cuts = {}
    for name, n in (("c_q", 256), ("c_kv", 128), ("k_rope", 32), ("gla_q", 256), ("gla_k", 256),
                    ("gla_v", 512), ("gla_g", 512), ("gla_a", 16), ("rwkv", 1824), ("gates", 3072)):
        cuts[name] = w_in[:, :, o:o + n]
        o += n
    kr = cuts["k_rope"]
    kr_b = jnp.concatenate([-kr[..., 16:], kr[..., :16]], axis=-1)
    wa = jnp.concatenate([cuts["c_q"], cuts["c_kv"], zc(64), kr, zc(32), zc(64), kr_b, zc(32)], -1)
    wb = jnp.concatenate([cuts["gla_v"], cuts["gla_g"], cuts["gla_q"], cuts["gla_k"],
                          cuts["gla_a"], zc(112)], -1)
    wc = jnp.concatenate([cuts["rwkv"], zc(RWKV_MISC - 288)], -1)
    wd = cuts["gates"]

    wuq = mla_w_uq.reshape(L, MLA_Q_RANK, MLA_HEADS, MLA_NOPE + MLA_ROPE)
    nope, rope = wuq[..., :MLA_NOPE], wuq[..., MLA_NOPE:]
    zq = lambda n: jnp.zeros((L, MLA_Q_RANK, MLA_HEADS, n), F32)
    wqa = jnp.concatenate([nope, rope, zq(32)], -1).reshape(L, MLA_Q_RANK, MLA_W)
    rope_b = jnp.concatenate([-rope[..., 16:], rope[..., :16]], -1)
    wqb = jnp.concatenate([zq(64), rope_b, zq(32)], -1).reshape(L, MLA_Q_RANK, MLA_W)
    wukv = mla_w_ukv.reshape(L, MLA_KV_RANK, MLA_HEADS, 128)
    zk = jnp.zeros((L, MLA_KV_RANK, MLA_HEADS, 64), F32)
    wk = jnp.concatenate([wukv[..., :64], zk], -1).reshape(L, MLA_KV_RANK, MLA_W)
    wv = jnp.concatenate([wukv[..., 64:], zk[..., :MLA_VSLOT - 64]], -1).reshape(
        L, MLA_KV_RANK, MLA_HEADS * MLA_VSLOT)

    bra = w_branch[:, :512].reshape(L, MLA_HEADS, 64, D)
    bra = jnp.concatenate([bra, jnp.zeros_like(bra)], axis=2).reshape(L, MLA_W, D)
    brb = w_branch[:, 512:512 + GLA_W]
    brc = w_branch[:, 512 + GLA_W:]

    walpha = jnp.concatenate(
        [gla_w_alpha, jnp.zeros((L, 128 - GLA_GATE_RANK, GLA_QK), F32)], axis=1)
    gn = jnp.tile(gla_norm, (1, GLA_HEADS))[:, None, :]
    mu = jnp.concatenate([rwkv_mu, jnp.zeros((L, RWKV_MISC - 288), F32)], -1)[:, None, :]
    zr = lambda n: jnp.zeros((L, n, RWKV_W), F32)
    def hi_lo(w):
        hi = w.astype(BF16)
        return jnp.stack([hi, (w - hi.astype(F32)).astype(BF16)], axis=1)

    wdec = hi_lo(jnp.concatenate([rwkv_w_decay, zr(64)], axis=1))
    wiclr = hi_lo(jnp.concatenate([zr(64), rwkv_w_iclr], axis=1))
    wgate = jnp.concatenate([rwkv_w_gate, zr(256 - RWKV_GATE_RANK)], axis=1)
    bf = lambda w: w.astype(BF16)
    return dict(wa=bf(wa), wb=bf(wb), wc=bf(wc), wd=bf(wd), wqa=bf(wqa), wqb=bf(wqb), wk=bf(wk),
                wv=bf(wv), bra=bf(bra), brb=bf(brb), brc=bf(brc), walpha=hi_lo(walpha), gn=gn, mu=mu,
                wdec=wdec, wiclr=wiclr, wgate=bf(wgate))


def kernel(x, mem, positions, norm_g, w_ffn_in, w_ffn_out, w_in, mla_q_norm, mla_w_uq, mla_kv_norm, mla_w_ukv, gla_w_alpha, gla_b_alpha, gla_norm, rwkv_mu, rwkv_w0, rwkv_w_decay, rwkv_a0, rwkv_w_iclr, rwkv_w_gate, rwkv_k_k, rwkv_k_a, rwkv_r_k, rwkv_ln_w, rwkv_ln_b, w_branch, w_out, mem_norm, mem_wq, mem_wkv, mem_wo):
    B, S, D = x.shape
    L = norm_g.shape[0]
    T = B * S
    tm = min(512, S)
    tq = min(1024, S)
    tk = tq // 2
    ts = min(256, S)
    tff = min(1024, S)
    nsp = 11

    pp = _prepare_params(w_in, mla_w_uq, mla_w_ukv, gla_w_alpha, gla_norm, rwkv_mu, rwkv_w_decay,
                         rwkv_w_iclr, rwkv_w_gate, w_branch)
    bf = lambda w: w.astype(BF16)
    ffn_in, ffn_out = bf(w_ffn_in), bf(w_ffn_out)
    wout, wq, wkv, wo = bf(w_out), bf(mem_wq), bf(mem_wkv), bf(mem_wo)
    ng = norm_g.reshape(L * 8, 1, D)
    row = lambda p: p[:, None, :]
    gl = dict(walpha=pp["walpha"], balpha=row(gla_b_alpha), gn=pp["gn"])
    rw = dict(mu=pp["mu"], w0=row(rwkv_w0), wdec=pp["wdec"], a0=row(rwkv_a0), wiclr=pp["wiclr"],
              wgate=pp["wgate"], k_k=row(rwkv_k_k), k_a=row(rwkv_k_a), r_k=row(rwkv_r_k),
              ln_w=row(rwkv_ln_w), ln_b=row(rwkv_ln_b))

    inv_freq = ROPE_THETA ** (-jnp.arange(0, MLA_ROPE, 2, dtype=F32) / MLA_ROPE)
    invf_lane = jnp.concatenate(
        [jnp.zeros((MLA_NOPE,), F32), inv_freq, inv_freq, jnp.zeros((32,), F32)])[None, :]
    c0, s0 = _rope_tables(positions.astype(F32).reshape(T, 1), invf_lane, tm)

    x = x.reshape(T, D)
    for l in range(L):
        x = _ffn(x, ng, ffn_in, ffn_out, l, 0, tff, nsp)
        za, zb, zc = _normproj(x, ng, l * 8 + 2, [pp["wa"], pp["wb"], pp["wc"]], l, tm)
        q, k, vt = _mla_prep(za, c0, s0, row(mla_q_norm), row(mla_kv_norm), pp["wqa"], pp["wqb"],
                             pp["wk"], pp["wv"], l, tk)
        o_a = _mla_attn(q, k, vt, B, S, tq, tk)
        o_b, o_c = _recur(zb, zc, gl, rw, l, B, S, ts)
        x = _merge(x, o_a, o_b, o_c, pp["wd"], pp["bra"], pp["brb"], pp["brc"], wout, ng, l, tff)
        kv = _mem_kv(mem, row(mem_norm), wkv, l)
        x = _mem_attn(x, ng, wq, kv, wo, l, S, tff)
        x = _ffn(x, ng, ffn_in, ffn_out, l, 1, tff, nsp)
    return x.reshape(B, S, D)
```

```python
import functools

import jax
import jax.numpy as jnp
import numpy as np
from jax import lax
from jax.experimental import pallas as pl
from jax.experimental.pallas import tpu as pltpu

F32 = jnp.float32
BF16 = jnp.bfloat16
HI = lax.Precision.HIGHEST

D_MODEL = 1024
D_FF = 2816
NORM_EPS = 1e-6
LOG2_E = 1.4426950408889634
MLA_HEADS = 8
MLA_NOPE = 64
MLA_ROPE = 32
MLA_Q_RANK = 256
MLA_KV_RANK = 128
ROPE_THETA = 10000.0
MLA_SLOT = 128
MLA_W = MLA_HEADS * MLA_SLOT
MLA_HPS = 2
MLA_VSLOT = 80
MLA_ONE = 64
GLA_HEADS = 4
GLA_DK = 64
GLA_DV = 128
GLA_GATE_RANK = 16
GLA_GATE_NORM = 16.0
GLA_QK = GLA_HEADS * GLA_DK
GLA_W = GLA_HEADS * GLA_DV
GLA_LEVELS = 6
GLA_NEG = -1e30
RWKV_HEADS = 8
RWKV_N = 64
RWKV_DECAY_RANK = 64
RWKV_ICLR_RANK = 64
RWKV_GATE_RANK = 160
RWKV_GN_EPS = 64e-5
RWKV_W = RWKV_HEADS * RWKV_N
RWKV_MISC = 384
RWKV_COLS_PAD = 3 * RWKV_W + RWKV_MISC
RWKV_QUAD = 4 * RWKV_N
RWKV_NQ = RWKV_W // RWKV_QUAD
REC_CPI = 2
MEM_HEADS = 4
MEM_HD = D_MODEL // MEM_HEADS
N_BRANCH = 3
MERGE_COLS = 256
CHUNK = 64

VMEM_LIMIT_BYTES = 56 * 1024 * 1024


def _cparams(n_axes, flags=None):
    return pltpu.CompilerParams(
        dimension_semantics=("arbitrary",) * n_axes,
        vmem_limit_bytes=VMEM_LIMIT_BYTES,
        flags=flags,
    )


def _dot(a, b, precision=None):
    return jnp.dot(a, b, preferred_element_type=F32, precision=precision)


def _dot_nt(a, b, precision=None):
    return lax.dot_general(a, b, (((1,), (1,)), ((), ())),
                           preferred_element_type=F32, precision=precision)


def _dot_tn(a, b, precision=None):
    return lax.dot_general(a, b, (((0,), (0,)), ((), ())),
                           preferred_element_type=F32, precision=precision)


def _rms(x, g, eps=NORM_EPS):
    return x * lax.rsqrt(jnp.mean(x * x, axis=-1, keepdims=True) + eps) * g


def _sigmoid(x):
    return 1.0 / (1.0 + jnp.exp(-x))


def _softplus(x):
    return jnp.maximum(x, 0.0) + jnp.log(1.0 + jnp.exp(-jnp.abs(x)))


def _split2(x):
    hi = x.astype(BF16)
    return hi, (x - hi.astype(F32)).astype(BF16)


def _dot_split(x, w_hi, w_lo):
    hi, lo = _split2(x)
    m = x.shape[0]
    top = _dot(jnp.concatenate([hi, lo], axis=0), w_hi)
    return top[:m] + top[m:] + _dot(hi, w_lo)


def _dot_01(m01_bf16, x):
    p1 = x.astype(BF16)
    r1 = x - p1.astype(F32)
    p2 = r1.astype(BF16)
    p3 = (r1 - p2.astype(F32)).astype(BF16)
    n = x.shape[1]
    out = _dot(m01_bf16, jnp.concatenate([p1, p2, p3], axis=1))
    return out[:, :n] + out[:, n:2 * n] + out[:, 2 * n:]


def _iota(shape, dim):
    return lax.broadcasted_iota(jnp.int32, shape, dim)


def _segsum(x, e_bf16):
    hi = x.astype(BF16)
    lo = (x - hi.astype(F32)).astype(BF16)
    return _dot(hi, e_bf16) + _dot(lo, e_bf16)


def _vec_spec(n, l):
    return pl.BlockSpec((None, 1, n), lambda *_: (l, 0, 0))


def _mat_spec(r, c, l):
    return pl.BlockSpec((None, r, c), lambda *_: (l, 0, 0))


def _ffn_body(x_ref, gpre_ref, wi_ref, wo_ref, gpost_ref, o_ref, *, n_split):
    x = x_ref[...]
    h = _rms(x, gpre_ref[...]).astype(BF16)
    ff = wo_ref.shape[0]
    tf = ff // n_split
    y = None
    for j in range(n_split):
        g = _dot(h, wi_ref[:, j * tf:(j + 1) * tf])
        u = _dot(h, wi_ref[:, ff + j * tf:ff + (j + 1) * tf])
        act = (g * _sigmoid(g) * u).astype(BF16)
        part = _dot(act, wo_ref[j * tf:(j + 1) * tf, :])
        y = part if y is None else y + part
    o_ref[...] = x + 0.5 * _rms(y, gpost_ref[...])


def _ffn(x, ng, w_in, w_out, l, k, tm, n_split):
    T, D = x.shape
    ff = w_out.shape[2]
    g_pre, g_post = l * 8 + 6 * k, l * 8 + 6 * k + 1
    return pl.pallas_call(
        functools.partial(_ffn_body, n_split=n_split),
        grid=(T // tm,),
        in_specs=[
            pl.BlockSpec((tm, D), lambda i: (i, 0)),
            pl.BlockSpec((None, 1, D), lambda i: (g_pre, 0, 0)),
            pl.BlockSpec((None, None, D, 2 * ff), lambda i: (l, k, 0, 0),
                         pipeline_mode=pl.Buffered(1)),
            pl.BlockSpec((None, None, ff, D), lambda i: (l, k, 0, 0), pipeline_mode=pl.Buffered(1)),
            pl.BlockSpec((None, 1, D), lambda i: (g_post, 0, 0)),
        ],
        out_specs=pl.BlockSpec((tm, D), lambda i: (i, 0)),
        out_shape=jax.ShapeDtypeStruct((T, D), F32),
        compiler_params=_cparams(1),
        name="ffn",
    )(x, ng, w_in, w_out, ng)


def _normproj_body(x_ref, g_ref, *refs):
    n = len(refs) // 2
    h = _rms(x_ref[...], g_ref[...]).astype(BF16)
    for w_ref, o_ref in zip(refs[:n], refs[n:]):
        o_ref[...] = _dot(h, w_ref[...]).astype(o_ref.dtype)


def _normproj(x, ng, g_idx, ws, l, tm):
    T, D = x.shape
    widths = [w.shape[2] for w in ws]
    return pl.pallas_call(
        _normproj_body,
        grid=(T // tm,),
        in_specs=[pl.BlockSpec((tm, D), lambda i: (i, 0)),
                  pl.BlockSpec((None, 1, D), lambda i: (g_idx, 0, 0))]
        + [pl.BlockSpec((None, D, n), lambda i: (l, 0, 0), pipeline_mode=pl.Buffered(1))
           for n in widths],
        out_specs=[pl.BlockSpec((tm, n), lambda i: (i, 0)) for n in widths],
        out_shape=[jax.ShapeDtypeStruct((T, n), F32) for n in widths],
        compiler_params=_cparams(1),
        name="mixer_proj",
    )(x, ng, *ws)


def _rope_table_body(pos_ref, invf_ref, c_ref, s_ref):
    ang = pos_ref[...] * invf_ref[...]
    lane = _iota(ang.shape, 1)
    rot = (lane >= MLA_NOPE) & (lane < MLA_NOPE + MLA_ROPE)
    c_ref[...] = jnp.where(lane < MLA_NOPE, 1.0, jnp.where(rot, jnp.cos(ang), 0.0))
    s_ref[...] = jnp.where(rot, jnp.sin(ang), 0.0)


def _rope_tables(pos_f32, invf_lane, tm):
    T = pos_f32.shape[0]
    return pl.pallas_call(
        _rope_table_body,
        grid=(T // tm,),
        in_specs=[pl.BlockSpec((tm, 1), lambda i: (i, 0)),
                  pl.BlockSpec((1, MLA_SLOT), lambda i: (0, 0))],
        out_specs=[pl.BlockSpec((tm, MLA_SLOT), lambda i: (i, 0))] * 2,
        out_shape=[jax.ShapeDtypeStruct((T, MLA_SLOT), F32)] * 2,
        compiler_params=_cparams(1),
        name="rope_tables",
    )(pos_f32, invf_lane)


def _mla_prep_body(za_ref, c0_ref, s0_ref, qn_ref, kvn_ref, wqa_ref, wqb_ref, wk_ref, wv_ref,
                   one_ref, q_ref, k_ref, vt_ref):
    za = za_ref[...]
    cq = za[:, :MLA_Q_RANK]
    ckv = za[:, MLA_Q_RANK:MLA_Q_RANK + MLA_KV_RANK]
    kra = za[:, 384:512]
    krb = za[:, 512:640]
    cqn = _rms(cq, qn_ref[...]).astype(BF16)
    ckvn = _rms(ckv, kvn_ref[...]).astype(BF16)
    c0 = c0_ref[...]
    s0 = s0_ref[...]
    c8 = jnp.concatenate([c0] * MLA_HEADS, axis=1)
    s8 = jnp.concatenate([s0] * MLA_HEADS, axis=1)
    scale = (MLA_NOPE + MLA_ROPE) ** -0.5 * LOG2_E
    q = (_dot(cqn, wqa_ref[...]) * c8 + _dot(cqn, wqb_ref[...]) * s8) * scale
    q_ref[...] = q.astype(BF16)
    krot = kra * c0 + krb * s0
    k = _dot(ckvn, wk_ref[...]) + jnp.concatenate([krot] * MLA_HEADS, axis=1)
    k_ref[...] = k.astype(BF16)
    v = _dot(ckvn, wv_ref[...]) + one_ref[...]
    vt_ref[...] = v.T.astype(BF16)


def _mla_prep(za, c0, s0, qn, kvn, wqa, wqb, wk, wv, l, tk):
    T = za.shape[0]
    vw = MLA_HEADS * MLA_VSLOT
    tok = lambda n: pl.BlockSpec((tk, n), lambda i: (i, 0))
    one_lane = jnp.asarray((np.arange(vw) % MLA_VSLOT == MLA_ONE).astype(np.float32))[None, :]
    return pl.pallas_call(
        _mla_prep_body,
        grid=(T // tk,),
        in_specs=[tok(za.shape[1]), tok(MLA_SLOT), tok(MLA_SLOT),
                  _vec_spec(MLA_Q_RANK, l), _vec_spec(MLA_KV_RANK, l),
                  _mat_spec(MLA_Q_RANK, MLA_W, l), _mat_spec(MLA_Q_RANK, MLA_W, l),
                  _mat_spec(MLA_KV_RANK, MLA_W, l), _mat_spec(MLA_KV_RANK, vw, l),
                  pl.BlockSpec((1, vw), lambda i: (0, 0))],
        out_specs=[tok(MLA_W), tok(MLA_W),
                   pl.BlockSpec((None, vw, tk), lambda i: (i, 0, 0))],
        out_shape=[jax.ShapeDtypeStruct((T, MLA_W), BF16)] * 2
        + [jax.ShapeDtypeStruct((T // tk, vw, tk), BF16)],
        compiler_params=_cparams(1),
        name="mla_prep",
    )(za, c0, s0, qn, kvn, wqa, wqb, wk, wv, one_lane)


def _mla_attn_body(q_ref, k_ref, vt_ref, o_ref, s_ref, m_ref, acc_ref, *, tq, tk):
    qi = pl.program_id(2)
    m_ref[...] = jnp.full_like(m_ref, -1e30)
    acc_ref[...] = jnp.zeros_like(acc_ref)
    heads = [slice(hh * MLA_SLOT, (hh + 1) * MLA_SLOT) for hh in range(MLA_HPS)]

    def produce(slot, ki):
        r0 = pl.multiple_of(ki * tk, tk)
        for hh, sl in enumerate(heads):
            s_ref[slot, hh] = _dot_nt(k_ref[pl.ds(r0, tk), sl], q_ref[:, sl])

    def consume(slot, ki, diag):
        for hh, sl in enumerate(heads):
            st = s_ref[slot, hh]
            if diag is not None:
                key = _iota(st.shape, 0) + diag * tk
                st = jnp.where(key <= _iota(st.shape, 1), st, -1e30)
            m_old = m_ref[hh]
            m_new = jnp.maximum(m_old, jnp.max(st, axis=0, keepdims=True))
            p = jnp.exp2(st - m_new).astype(BF16)
            vt = vt_ref[ki, hh * MLA_VSLOT:(hh + 1) * MLA_VSLOT, :]
            acc_ref[hh] = jnp.exp2(m_old - m_new) * acc_ref[hh] + _dot(vt, p)
            m_ref[hh] = m_new

    def body(j, carry):
        produce(1, 2 * j + 1)
        consume(0, 2 * j, None)
        produce(0, 2 * j + 2)
        consume(1, 2 * j + 1, None)
        return carry

    produce(0, 0)
    lax.fori_loop(0, qi, body, 0)
    r1 = pl.multiple_of((2 * qi + 1) * tk, tk)
    late = slice(tk, tq)
    for hh, sl in enumerate(heads):
        s_ref[1, hh, :, 0:tk] = _dot_nt(k_ref[pl.ds(r1, tk), sl], q_ref[late, sl])
    consume(0, 2 * qi, 0)
    for hh, sl in enumerate(heads):
        st = s_ref[1, hh, :, 0:tk]
        st = jnp.where(_iota(st.shape, 0) <= _iota(st.shape, 1), st, -1e30)
        m_old = m_ref[hh, :, late]
        m_new = jnp.maximum(m_old, jnp.max(st, axis=0, keepdims=True))
        p = jnp.exp2(st - m_new).astype(BF16)
        vt = vt_ref[2 * qi + 1, hh * MLA_VSLOT:(hh + 1) * MLA_VSLOT, :]
        acc_ref[hh, :, late] = jnp.exp2(m_old - m_new) * acc_ref[hh, :, late] + _dot(vt, p)
        m_ref[hh, :, late] = m_new
    for hh in range(MLA_HPS):
        acc = acc_ref[hh]
        o_t = acc / acc[MLA_ONE:MLA_ONE + 1, :]
        o_t = jnp.concatenate([o_t, jnp.zeros((MLA_SLOT - MLA_VSLOT, tq), F32)], axis=0)
        o_ref[:, hh * MLA_SLOT:(hh + 1) * MLA_SLOT] = o_t.T.astype(o_ref.dtype)


def _mla_attn(q, k, vt, B, S, tq, tk):
    assert tq == 2 * tk
    T = q.shape[0]
    nq, nk = S // tq, S // tk
    w = MLA_HPS * MLA_SLOT
    return pl.pallas_call(
        functools.partial(_mla_attn_body, tq=tq, tk=tk),
        grid=(B, MLA_HEADS // MLA_HPS, nq),
        in_specs=[
            pl.BlockSpec((tq, w), lambda b, h, i: (b * nq + i, h)),
            pl.BlockSpec((S, w), lambda b, h, i: (b, h)),
            pl.BlockSpec((nk, MLA_HPS * MLA_VSLOT, tk), lambda b, h, i: (b, h, 0)),
        ],
        out_specs=pl.BlockSpec((tq, w), lambda b, h, i: (b * nq + i, h)),
        out_shape=jax.ShapeDtypeStruct((T, MLA_W), BF16),
        scratch_shapes=[pltpu.VMEM((2, MLA_HPS, tk, tq), F32),
                        pltpu.VMEM((MLA_HPS, 1, tq), F32),
                        pltpu.VMEM((MLA_HPS, MLA_VSLOT, tq), F32)],
        compiler_params=_cparams(3),
        name="mla_attn",
    )(q, k, vt)


def _gla_pair_masks():
    i = np.arange(CHUNK)[:, None]
    j = np.arange(GLA_HEADS * CHUNK)[None, :] % CHUNK
    same = [(i >> (6 - lv)) == (j >> (6 - lv)) for lv in range(GLA_LEVELS)]
    return jnp.asarray(np.stack(same + [i == j]).astype(np.float32))


def _gla_stages(v_ref, g_ref, q_ref, k_ref, a_ref, walpha_ref, balpha_ref, gn_ref, pair_ref,
                o_ref, ht_ref, n_batch):
    C = CHUNK
    walpha_hi, walpha_lo = walpha_ref[0], walpha_ref[1]
    balpha = balpha_ref[...]
    gn = gn_ref[...]
    tri = (_iota((C, C), 1) <= _iota((C, C), 0)).astype(BF16)
    st_r, st_c = _iota((GLA_HEADS * C, GLA_QK), 0), _iota((GLA_HEADS * C, GLA_QK), 1)
    head_mask = ((st_r >> 6) == (st_c >> 6)).astype(BF16)
    sv_r, sv_c = _iota((GLA_HEADS * C, GLA_W), 0), _iota((GLA_HEADS * C, GLA_W), 1)
    value_mask = ((sv_r >> 6) == (sv_c >> 7)).astype(BF16)
    ht_r, ht_c = _iota((GLA_W, GLA_QK), 0), _iota((GLA_W, GLA_QK), 1)
    state_mask = ((ht_r >> 7) == (ht_c >> 6)).astype(F32)
    tok = _iota((C, GLA_QK), 0)
    halves = [C >> (lv + 1) for lv in range(GLA_LEVELS)]

    def stack(x):
        return jnp.concatenate([x.astype(BF16)] * GLA_HEADS, axis=0) * head_mask

    def level_operands(q, k, cb):
        last = {1: cb}
        for s in halves[:0:-1]:
            f = last[s]
            last[2 * s] = jnp.where((tok & s) != 0, f, pltpu.roll(f, C - s, 0))
        ops = []
        for hs in halves:
            f = last[hs]
            right = (tok & hs) != 0
            d = cb - jnp.where(right, pltpu.roll(f, hs, 0), f)
            ql = q * jnp.exp(jnp.where(right, d, GLA_NEG))
            kl = k * jnp.exp(jnp.where(right, GLA_NEG, -d))
            ops.append((ql.astype(BF16), stack(kl)))
        ops.append((q.astype(BF16), stack(k)))
        return ops

    def stages(c):
        cbs = [(b, pl.ds(pl.multiple_of((c * REC_CPI + sub) * C, C), C))
               for sub in range(REC_CPI) for b in range(n_batch)]
        xs = [_dot_split(a_ref[b, rows, :], walpha_hi, walpha_lo) + balpha for b, rows in cbs]
        yield
        cum = [_dot_01(tri, -_softplus(-x) * (1.0 / GLA_GATE_NORM)) for x in xs]
        yield
        st = []
        for (b, rows), cb in zip(cbs, cum):
            b_last = cb[C - 1:C, :]
            q = q_ref[b, rows, :] * GLA_DK ** -0.5
            k = k_ref[b, rows, :]
            vb = v_ref[b, rows, :].astype(BF16)
            st.append(dict(
                v=vb,
                bd_v=jnp.concatenate([vb] * GLA_HEADS, axis=0) * value_mask,
                qe=(q * jnp.exp(cb)).astype(BF16),
                ke=(k * jnp.exp(b_last - cb)).astype(BF16),
                gam=jnp.exp(b_last),
                ops=level_operands(q, k, cb)))
        scs = [None] * len(cbs)
        for lv in range(GLA_LEVELS + 1):
            for i, s in enumerate(st):
                ql, kl = s["ops"][lv]
                part = _dot_nt(ql, kl) * pair_ref[lv]
                scs[i] = part if lv == 0 else scs[i] + part
            yield
        intra = [_dot(sc.astype(BF16), s["bd_v"]) for sc, s in zip(scs, st)]
        yield
        for (b, rows), s, o_intra in zip(cbs, st, intra):
            ht = ht_ref[b]
            o = o_intra + _dot_nt(s["qe"], ht.astype(BF16))
            ht_ref[b] = ht * s["gam"] + state_mask * _dot_tn(s["v"], s["ke"])
            outs = []
            for h in range(GLA_HEADS):
                oh = o[:, h * GLA_DV:(h + 1) * GLA_DV]
                outs.append(oh * lax.rsqrt(jnp.mean(oh * oh, axis=-1, keepdims=True) + NORM_EPS))
            on = jnp.concatenate(outs, axis=1) * gn
            g = g_ref[b, rows, :]
            o_ref[b, rows, :] = (on * (g * _sigmoid(g))).astype(o_ref.dtype)
            yield

    return stages


def _rwkv_stages(z_ref, mu_ref, w0_ref, wdec_ref, a0_ref, wiclr_ref, wgate_ref, kk_ref, ka_ref,
                 rk_ref, lnw_ref, lnb_ref, o_ref, h_ref, zlast_ref, n_batch):
    C, W, Q = CHUNK, RWKV_W, RWKV_QUAD
    mu = mu_ref[...]
    w0, a0 = w0_ref[...], a0_ref[...]
    wdec_hi, wdec_lo = wdec_ref[0], wdec_ref[1]
    wiclr_hi, wiclr_lo = wiclr_ref[0], wiclr_ref[1]
    wgate = wgate_ref[...]
    k_k, k_a, r_k = kk_ref[...], ka_ref[...], rk_ref[...]
    ln_w, ln_b = lnw_ref[...], lnb_ref[...]

    tri = (_iota((C, C), 1) <= _iota((C, C), 0)).astype(BF16)
    sq_r, sq_c = _iota((Q, Q), 0), _iota((Q, Q), 1)
    head_blk = ((sq_r >> 6) == (sq_c >> 6)).astype(F32)
    eye_q = (sq_r == sq_c).astype(F32)
    e_seg = head_blk.astype(BF16)
    wd_t, wd_s = _iota((C, Q), 0), _iota((C, Q), 1) & (C - 1)
    strict = (wd_s < wd_t).astype(F32)
    incl = (wd_s <= wd_t).astype(F32)
    eye_wide = (wd_s == wd_t).astype(F32)
    row0 = _iota((C, RWKV_COLS_PAD), 0) == 0

    def bd(x):
        return jnp.concatenate([x.astype(BF16)] * 4, axis=0) * e_seg

    def quad(x, i):
        return x[:, i * Q:(i + 1) * Q]

    def segsum(x):
        return jnp.concatenate([_segsum(quad(x, i), e_seg) for i in range(RWKV_NQ)], axis=1)

    def chunk_one(b, rows):
        z = z_ref[b, rows, :]
        zp = jnp.where(row0, zlast_ref[b, 0:1, :], pltpu.roll(z, 1, 0))
        zlast_ref[b, 0:1, :] = z[C - 1:C, :]
        z = z + mu * (zp - z)
        r = z[:, 0:W]
        k = z[:, W:2 * W]
        v = z[:, 2 * W:3 * W]
        m0 = z[:, 3 * W:3 * W + 128]
        m12 = z[:, 3 * W + 128:3 * W + 384]
        w_log = -_softplus(-(w0 + _dot_split(jnp.tanh(m0), wdec_hi, wdec_lo))) - 0.5
        lw = -jnp.exp(w_log)
        a = _sigmoid(a0 + _dot_split(m0, wiclr_hi, wiclr_lo))
        g = _dot(_sigmoid(m12).astype(BF16), wgate)
        kk = k * k_k
        kkn = kk * lax.rsqrt(jnp.maximum(segsum(kk * kk), 1e-24))
        k2 = k * (1.0 + (a - 1.0) * k_a)
        beta = kkn * a
        cs = _dot_01(tri, lw)
        c_last = cs[C - 1:C, :]
        dec_in = jnp.exp(-cs)
        dec_out = jnp.exp(c_last - cs)
        kt = kkn * jnp.exp(cs - lw)
        rt = r * jnp.exp(cs)
        bh = beta * dec_in
        kh = k2 * dec_in
        kbar = k2 * dec_out
        bbar = beta * dec_out
        gam = jnp.exp(c_last)
        chains = [dict(b=b, i=i, kt=quad(kt, i), rt=quad(rt, i), v=quad(v, i), bh=quad(bh, i),
                       kh=quad(kh, i), kbar=quad(kbar, i), bbar=quad(bbar, i), gam=quad(gam, i))
                  for i in range(RWKV_NQ)]
        return dict(r=r, k2=k2, v=v, g=g), chains

    def epilogue(b, rows, tok, y):
        mean = segsum(y) * (1.0 / RWKV_N)
        d = y - mean
        var = segsum(d * d) * (1.0 / RWKV_N)
        yn = d * lax.rsqrt(var + RWKV_GN_EPS) * ln_w + ln_b
        bonus = segsum(tok["r"] * tok["k2"] * r_k) * tok["v"]
        o_ref[b, rows, :] = ((yn + bonus) * tok["g"]).astype(o_ref.dtype)

    def stages(c):
        groups, chains = [], []
        for sub in range(REC_CPI):
            rows = pl.ds(pl.multiple_of((c * REC_CPI + sub) * C, C), C)
            for b in range(n_batch):
                tok, ch = chunk_one(b, rows)
                groups.append((b, rows, tok, ch))
                chains += ch
                yield
        for ch in chains:
            lhs = jnp.concatenate([ch["kt"], ch["rt"]], axis=0).astype(BF16)
            rhs = jnp.concatenate([bd(ch["bh"]), bd(ch["kh"])], axis=0)
            sc = _dot_nt(lhs, rhs)
            ch["a_kk"] = sc[:C, Q:] * strict
            ch["a_rb"] = sc[C:, :Q] * incl
            ch["a_rk"] = sc[C:, Q:] * incl
            ch["n"] = -(sc[:C, :Q] * strict)
            ch["t"] = eye_wide + ch["n"]
        yield
        for ch in chains:
            ch["p"] = _dot(ch["n"].astype(BF16), bd(ch["n"]))
            ch["bd_v"] = bd(ch["v"])
            ch["akk_v"] = _dot(ch["a_kk"].astype(BF16), ch["bd_v"])
        yield
        for _ in range(4):
            for ch in chains:
                tp = _dot(jnp.concatenate([ch["t"], ch["p"]], axis=0).astype(BF16), bd(ch["p"]))
                ch["t"] = ch["t"] + tp[:C]
                ch["p"] = tp[C:]
            yield
        for ch in chains:
            ch["t"] = ch["t"] + _dot(ch["t"].astype(BF16), bd(ch["p"]))
        yield
        for ch in chains:
            w12 = _dot(ch["t"].astype(BF16),
                       jnp.concatenate([bd(ch["akk_v"]), bd(ch["kt"])], axis=1))
            ch["w1"], ch["w2"] = w12[:, :Q], w12[:, Q:]
        yield
        for ch in chains:
            w1, w2, bbar = ch["w1"], ch["w2"], ch["bbar"]
            ch["y0"] = _dot(jnp.concatenate([ch["a_rk"], ch["a_rb"]], axis=1).astype(BF16),
                            jnp.concatenate([ch["bd_v"], bd(-w1)], axis=0))
            ch["rp"] = (ch["rt"] - _dot(ch["a_rb"].astype(BF16), bd(w2))).astype(BF16)
            ch["p_bd"] = (eye_q * ch["gam"] - head_blk * _dot_tn(
                bbar.astype(BF16), w2.astype(BF16))).astype(BF16)
            ch["q_bd"] = head_blk * _dot_tn(
                jnp.concatenate([ch["kbar"], -bbar], axis=0).astype(BF16),
                jnp.concatenate([ch["v"], w1], axis=0).astype(BF16))
        yield
        for b, rows, tok, chs in groups:
            ys = []
            for ch in chs:
                hb = h_ref[b, ch["i"]].astype(BF16)
                ys.append(_dot(ch["rp"], hb) + ch["y0"])
                h_ref[b, ch["i"]] = _dot(ch["p_bd"], hb) + ch["q_bd"]
            epilogue(b, rows, tok, jnp.concatenate(ys, axis=1))
            yield

    return stages


N_GLA_IN, N_RWKV_IN = 9, 12


def _recur_body(*refs, n_chunks, n_batch):
    gla_in = refs[:N_GLA_IN]
    rwkv_in = refs[N_GLA_IN:N_GLA_IN + N_RWKV_IN]
    ob_ref, oc_ref, ht_ref, h_ref, zlast_ref = refs[N_GLA_IN + N_RWKV_IN:]

    @pl.when(pl.program_id(0) == 0)
    def _():
        ht_ref[...] = jnp.zeros_like(ht_ref)
        h_ref[...] = jnp.zeros_like(h_ref)
        zlast_ref[...] = jnp.zeros_like(zlast_ref)

    gla = _gla_stages(*gla_in, ob_ref, ht_ref, n_batch)
    rwkv = _rwkv_stages(*rwkv_in, oc_ref, h_ref, zlast_ref, n_batch)

    def chunk(c, carry):
        live = [rwkv(c), gla(c)]
        while live:
            live = [g for g in live if next(g, True) is None]
        return carry

    lax.fori_loop(0, n_chunks // REC_CPI, chunk, 0)


def _recur(zb, zc, gp, rp, l, B, S, ts):
    ns = S // ts
    W, Q = RWKV_W, RWKV_QUAD
    blk = lambda n, j: pl.BlockSpec((B, ts, n), lambda s: (0, s, j))
    split = lambda c: pl.BlockSpec((None, 2, 128, c), lambda s: (l, 0, 0, 0))
    zb = zb.reshape(B, S, zb.shape[1])
    o_b, o_c = pl.pallas_call(
        functools.partial(_recur_body, n_chunks=ts // CHUNK, n_batch=B),
        grid=(ns,),
        in_specs=[
            blk(GLA_W, 0),
            blk(GLA_W, 1),
            blk(GLA_QK, 4),
            blk(GLA_QK, 5),
            blk(128, 12),
            split(GLA_QK), _vec_spec(GLA_QK, l), _vec_spec(GLA_W, l),
            pl.BlockSpec((GLA_LEVELS + 1, CHUNK, GLA_HEADS * CHUNK), lambda s: (0, 0, 0)),
            blk(RWKV_COLS_PAD, 0),
            _vec_spec(RWKV_COLS_PAD, l), _vec_spec(W, l), split(W),
            _vec_spec(W, l), split(W), _mat_spec(256, W, l),
            _vec_spec(W, l), _vec_spec(W, l), _vec_spec(W, l), _vec_spec(W, l), _vec_spec(W, l),
        ],
        out_specs=[blk(GLA_W, 0), blk(W, 0)],
        out_shape=[jax.ShapeDtypeStruct((B, S, GLA_W), BF16), jax.ShapeDtypeStruct((B, S, W), BF16)],
        scratch_shapes=[pltpu.VMEM((B, GLA_W, GLA_QK), F32),
                        pltpu.VMEM((B, RWKV_NQ, Q, Q), F32),
                        pltpu.VMEM((B, 8, RWKV_COLS_PAD), F32)],
        compiler_params=_cparams(1),
        name="recurrences",
    )(zb, zb, zb, zb, zb, gp["walpha"], gp["balpha"], gp["gn"], _gla_pair_masks(),
      zc.reshape(B, S, RWKV_COLS_PAD), rp["mu"], rp["w0"], rp["wdec"], rp["a0"], rp["wiclr"],
      rp["wgate"], rp["k_k"], rp["k_a"], rp["r_k"], rp["ln_w"], rp["ln_b"])
    return o_b.reshape(B * S, GLA_W), o_c.reshape(B * S, W)


def _merge_body(x_ref, oa_ref, ob_ref, oc_ref, gpre_ref, wd_ref, wa_ref, wb_ref, wc_ref, wo_ref,
                gpost_ref, o_ref):
    x = x_ref[...]
    D = D_MODEL
    h = _rms(x, gpre_ref[...]).astype(BF16)
    branches = ((oa_ref[...], wa_ref), (ob_ref[...], wb_ref), (oc_ref[...], wc_ref))
    y = None
    for c in range(0, D, MERGE_COLS):
        cols = slice(c, c + MERGE_COLS)
        merged = None
        for j, (o, w_r) in enumerate(branches):
            gate = _sigmoid(_dot(h, wd_ref[:, j * D + c:j * D + c + MERGE_COLS]))
            term = gate * _dot(o, w_r[:, cols])
            merged = term if merged is None else merged + term
        part = _dot(merged.astype(BF16), wo_ref[cols, :])
        y = part if y is None else y + part
    o_ref[...] = x + _rms(y, gpost_ref[...])


def _merge(x, oa, ob, oc, wd, wa, wb, wc, wo, ng, l, tm):
    T, D = x.shape
    tok = lambda n: pl.BlockSpec((tm, n), lambda i: (i, 0))
    res = lambda r, c: pl.BlockSpec((None, r, c), lambda i: (l, 0, 0), pipeline_mode=pl.Buffered(1))
    gain = lambda j: pl.BlockSpec((None, 1, D), lambda i: (l * 8 + j, 0, 0))
    return pl.pallas_call(
        _merge_body,
        grid=(T // tm,),
        in_specs=[tok(D), tok(MLA_W), tok(GLA_W), tok(RWKV_W), gain(2), res(D, N_BRANCH * D),
                  res(MLA_W, D), res(GLA_W, D), res(RWKV_W, D), res(D, D), gain(3)],
        out_specs=tok(D),
        out_shape=jax.ShapeDtypeStruct((T, D), F32),
        compiler_params=_cparams(1),
        name="merge",
    )(x, oa, ob, oc, ng, wd, wa, wb, wc, wo, ng)


def _mem_kv_body(mem_ref, g_ref, w_ref, o_ref):
    o_ref[...] = _dot(_rms(mem_ref[...], g_ref[...]).astype(BF16), w_ref[...]).astype(o_ref.dtype)


def _mem_kv(mem, mem_norm, wkv, l):
    B, M, D = mem.shape
    return pl.pallas_call(
        _mem_kv_body,
        grid=(B,),
        in_specs=[pl.BlockSpec((None, M, D), lambda b: (b, 0, 0)), _vec_spec(D, l),
                  _mat_spec(D, 2 * D, l)],
        out_specs=pl.BlockSpec((None, M, 2 * D), lambda b: (b, 0, 0)),
        out_shape=jax.ShapeDtypeStruct((B, M, 2 * D), BF16),
        compiler_params=_cparams(1),
        name="mem_kv",
    )(mem, mem_norm, wkv)


def _mem_attn_body(x_ref, gpre_ref, wq_ref, kv_ref, wo_ref, gpost_ref, o_ref):
    x = x_ref[...]
    D = D_MODEL
    h = _rms(x, gpre_ref[...]).astype(BF16)
    q = (_dot(h, wq_ref[...]) * (MEM_HD ** -0.5 * LOG2_E)).astype(BF16)
    kv = kv_ref[...]
    outs = []
    for hh in range(MEM_HEADS):
        sl = slice(hh * MEM_HD, (hh + 1) * MEM_HD)
        s = _dot_nt(q[:, sl], kv[:, sl])
        p = jnp.exp2(s - jnp.max(s, axis=-1, keepdims=True))
        o = _dot(p.astype(BF16), kv[:, D + hh * MEM_HD:D + (hh + 1) * MEM_HD])
        outs.append(o / jnp.sum(p, axis=-1, keepdims=True))
    o = jnp.concatenate(outs, axis=1).astype(BF16)
    o_ref[...] = x + _rms(_dot(o, wo_ref[...]), gpost_ref[...])


def _mem_attn(x, ng, wq, kv, wo, l, S, tm):
    T, D = x.shape
    M = kv.shape[1]
    per_b = S // tm
    return pl.pallas_call(
        _mem_attn_body,
        grid=(T // tm,),
        in_specs=[pl.BlockSpec((tm, D), lambda i: (i, 0)),
                  pl.BlockSpec((None, 1, D), lambda i: (l * 8 + 4, 0, 0)),
                  _mat_spec(D, D, l),
                  pl.BlockSpec((None, M, 2 * D), lambda i: (i // per_b, 0, 0)),
                  _mat_spec(D, D, l),
                  pl.BlockSpec((None, 1, D), lambda i: (l * 8 + 5, 0, 0))],
        out_specs=pl.BlockSpec((tm, D), lambda i: (i, 0)),
        out_shape=jax.ShapeDtypeStruct((T, D), F32),
        compiler_params=_cparams(1),
        name="mem_attn",
    )(x, ng, wq, kv, wo, ng)


def _prepare_params(w_in, mla_w_uq, mla_w_ukv, gla_w_alpha, gla_norm, rwkv_mu, rwkv_w_decay,
                    rwkv_w_iclr, rwkv_w_gate, w_branch):
    L, D = w_in.shape[0], w_in.shape[1]
    zc = lambda n: jnp.zeros((L, D, n), F32)
    o = 0
    cuts = {}
    for name, n in (("c_q", 256), ("c_kv", 128), ("k_rope", 32), ("gla_q", 256), ("gla_k", 256),
                    ("gla_v", 512), ("gla_g", 512), ("gla_a", 16), ("rwkv", 1824), ("gates", 3072)):
        cuts[name] = w_in[:, :, o:o + n]
        o += n
    kr = cuts["k_rope"]
    kr_b = jnp.concatenate([-kr[..., 16:], kr[..., :16]], axis=-1)
    wa = jnp.concatenate([cuts["c_q"], cuts["c_kv"], zc(64), kr, zc(32), zc(64), kr_b, zc(32)], -1)
    wb = jnp.concatenate([cuts["gla_v"], cuts["gla_g"], cuts["gla_q"], cuts["gla_k"],
                          cuts["gla_a"], zc(112)], -1)
    wc = jnp.concatenate([cuts["rwkv"], zc(RWKV_MISC - 288)], -1)
    wd = cuts["gates"]

    wuq = mla_w_uq.reshape(L, MLA_Q_RANK, MLA_HEADS, MLA_NOPE + MLA_ROPE)
    nope, rope = wuq[..., :MLA_NOPE], wuq[..., MLA_NOPE:]
    zq = lambda n: jnp.zeros((L, MLA_Q_RANK, MLA_HEADS, n), F32)
    wqa = jnp.concatenate([nope, rope, zq(32)], -1).reshape(L, MLA_Q_RANK, MLA_W)
    rope_b = jnp.concatenate([-rope[..., 16:], rope[..., :16]], -1)
    wqb = jnp.concatenate([zq(64), rope_b, zq(32)], -1).reshape(L, MLA_Q_RANK, MLA_W)
    wukv = mla_w_ukv.reshape(L, MLA_KV_RANK, MLA_HEADS, 128)
    zk = jnp.zeros((L, MLA_KV_RANK, MLA_HEADS, 64), F32)
    wk = jnp.concatenate([wukv[..., :64], zk], -1).reshape(L, MLA_KV_RANK, MLA_W)
    wv = jnp.concatenate([wukv[..., 64:], zk[..., :MLA_VSLOT - 64]], -1).reshape(
        L, MLA_KV_RANK, MLA_HEADS * MLA_VSLOT)

    bra = w_branch[:, :512].reshape(L, MLA_HEADS, 64, D)
    bra = jnp.concatenate([bra, jnp.zeros_like(bra)], axis=2).reshape(L, MLA_W, D)
    brb = w_branch[:, 512:512 + GLA_W]
    brc = w_branch[:, 512 + GLA_W:]

    walpha = jnp.concatenate(
        [gla_w_alpha, jnp.zeros((L, 128 - GLA_GATE_RANK, GLA_QK), F32)], axis=1)
    gn = jnp.tile(gla_norm, (1, GLA_HEADS))[:, None, :]
    mu = jnp.concatenate([rwkv_mu, jnp.zeros((L, RWKV_MISC - 288), F32)], -1)[:, None, :]
    zr = lambda n: jnp.zeros((L, n, RWKV_W), F32)
    def hi_lo(w):
        hi = w.astype(BF16)
        return jnp.stack([hi, (w - hi.astype(F32)).astype(BF16)], axis=1)

    wdec = hi_lo(jnp.concatenate([rwkv_w_decay, zr(64)], axis=1))
    wiclr = hi_lo(jnp.concatenate([zr(64), rwkv_w_iclr], axis=1))
    wgate = jnp.concatenate([rwkv_w_gate, zr(256 - RWKV_GATE_RANK)], axis=1)
    bf = lambda w: w.astype(BF16)
    return dict(wa=bf(wa), wb=bf(wb), wc=bf(wc), wd=bf(wd), wqa=bf(wqa), wqb=bf(wqb), wk=bf(wk),
                wv=bf(wv), bra=bf(bra), brb=bf(brb), brc=bf(brc), walpha=hi_lo(walpha), gn=gn, mu=mu,
                wdec=wdec, wiclr=wiclr, wgate=bf(wgate))


def kernel(x, mem, positions, norm_g, w_ffn_in, w_ffn_out, w_in, mla_q_norm, mla_w_uq, mla_kv_norm, mla_w_ukv, gla_w_alpha, gla_b_alpha, gla_norm, rwkv_mu, rwkv_w0, rwkv_w_decay, rwkv_a0, rwkv_w_iclr, rwkv_w_gate, rwkv_k_k, rwkv_k_a, rwkv_r_k, rwkv_ln_w, rwkv_ln_b, w_branch, w_out, mem_norm, mem_wq, mem_wkv, mem_wo):
    B, S, D = x.shape
    L = norm_g.shape[0]
    T = B * S
    tm = min(512, S)
    tq = min(1024, S)
    tk = tq // 2
    ts = min(256, S)
    tff = min(1024, S)
    nsp = 11

    pp = _prepare_params(w_in, mla_w_uq, mla_w_ukv, gla_w_alpha, gla_norm, rwkv_mu, rwkv_w_decay,
                         rwkv_w_iclr, rwkv_w_gate, w_branch)
    bf = lambda w: w.astype(BF16)
    ffn_in, ffn_out = bf(w_ffn_in), bf(w_ffn_out)
    wout, wq, wkv, wo = bf(w_out), bf(mem_wq), bf(mem_wkv), bf(mem_wo)
    ng = norm_g.reshape(L * 8, 1, D)
    row = lambda p: p[:, None, :]
    gl = dict(walpha=pp["walpha"], balpha=row(gla_b_alpha), gn=pp["gn"])
    rw = dict(mu=pp["mu"], w0=row(rwkv_w0), wdec=pp["wdec"], a0=row(rwkv_a0), wiclr=pp["wiclr"],
              wgate=pp["wgate"], k_k=row(rwkv_k_k), k_a=row(rwkv_k_a), r_k=row(rwkv_r_k),
              ln_w=row(rwkv_ln_w), ln_b=row(rwkv_ln_b))

    inv_freq = ROPE_THETA ** (-jnp.arange(0, MLA_ROPE, 2, dtype=F32) / MLA_ROPE)
    invf_lane = jnp.concatenate(
        [jnp.zeros((MLA_NOPE,), F32), inv_freq, inv_freq, jnp.zeros((32,), F32)])[None, :]
    c0, s0 = _rope_tables(positions.astype(F32).reshape(T, 1), invf_lane, tm)

    x = x.reshape(T, D)
    for l in range(L):
        x = _ffn(x, ng, ffn_in, ffn_out, l, 0, tff, nsp)
        za, zb, zc = _normproj(x, ng, l * 8 + 2, [pp["wa"], pp["wb"], pp["wc"]], l, tm)
        q, k, vt = _mla_prep(za, c0, s0, row(mla_q_norm), row(mla_kv_norm), pp["wqa"], pp["wqb"],
                             pp["wk"], pp["wv"], l, tk)
        o_a = _mla_attn(q, k, vt, B, S, tq, tk)
        o_b, o_c = _recur(zb, zc, gl, rw, l, B, S, ts)
        x = _merge(x, o_a, o_b, o_c, pp["wd"], pp["bra"], pp["brb"], pp["brc"], wout, ng, l, tff)
        kv = _mem_kv(mem, row(mem_norm), wkv, l)
        x = _mem_attn(x, ng, wq, kv, wo, l, S, tff)
        x = _ffn(x, ng, ffn_in, ffn_out, l, 1, tff, nsp)
    return x.reshape(B, S, D)
```

```python
import functools

import jax
import jax.numpy as jnp
import numpy as np
from jax import lax
from jax.experimental import pallas as pl
from jax.experimental.pallas import tpu as pltpu

F32 = jnp.float32
BF16 = jnp.bfloat16
HI = lax.Precision.HIGHEST

D_MODEL = 1024
D_FF = 2816
NORM_EPS = 1e-6
LOG2_E = 1.4426950408889634
MLA_HEADS = 8
MLA_NOPE = 64
MLA_ROPE = 32
MLA_Q_RANK = 256
MLA_KV_RANK = 128
ROPE_THETA = 10000.0
MLA_SLOT = 128
MLA_W = MLA_HEADS * MLA_SLOT
MLA_HPS = 2
MLA_VSLOT = 80
MLA_ONE = 64
GLA_HEADS = 4
GLA_DK = 64
GLA_DV = 128
GLA_GATE_RANK = 16
GLA_GATE_NORM = 16.0
GLA_QK = GLA_HEADS * GLA_DK
GLA_W = GLA_HEADS * GLA_DV
GLA_LEVELS = 6
GLA_NEG = -1e30
RWKV_HEADS = 8
RWKV_N = 64
RWKV_DECAY_RANK = 64
RWKV_ICLR_RANK = 64
RWKV_GATE_RANK = 160
RWKV_GN_EPS = 64e-5
RWKV_W = RWKV_HEADS * RWKV_N
RWKV_MISC = 384
RWKV_COLS_PAD = 3 * RWKV_W + RWKV_MISC
RWKV_QUAD = 4 * RWKV_N
RWKV_NQ = RWKV_W // RWKV_QUAD
REC_CPI = 2
MEM_HEADS = 4
MEM_HD = D_MODEL // MEM_HEADS
N_BRANCH = 3
MERGE_COLS = 256
CHUNK = 64

VMEM_LIMIT_BYTES = 56 * 1024 * 1024


def _cparams(n_axes, flags=None):
    return pltpu.CompilerParams(
        dimension_semantics=("arbitrary",) * n_axes,
        vmem_limit_bytes=VMEM_LIMIT_BYTES,
        flags=flags,
    )


def _dot(a, b, precision=None):
    return jnp.dot(a, b, preferred_element_type=F32, precision=precision)


def _dot_nt(a, b, precision=None):
    return lax.dot_general(a, b, (((1,), (1,)), ((), ())),
                           preferred_element_type=F32, precision=precision)


def _dot_tn(a, b, precision=None):
    return lax.dot_general(a, b, (((0,), (0,)), ((), ())),
                           preferred_element_type=F32, precision=precision)


def _rms(x, g, eps=NORM_EPS):
    return x * lax.rsqrt(jnp.mean(x * x, axis=-1, keepdims=True) + eps) * g


def _sigmoid(x):
    return 1.0 / (1.0 + jnp.exp(-x))


def _softplus(x):
    return jnp.maximum(x, 0.0) + jnp.log(1.0 + jnp.exp(-jnp.abs(x)))


def _split2(x):
    hi = x.astype(BF16)
    return hi, (x - hi.astype(F32)).astype(BF16)


def _dot_split(x, w_hi, w_lo):
    hi, lo = _split2(x)
    m = x.shape[0]
    top = _dot(jnp.concatenate([hi, lo], axis=0), w_hi)
    return top[:m] + top[m:] + _dot(hi, w_lo)


def _dot_01(m01_bf16, x):
    p1 = x.astype(BF16)
    r1 = x - p1.astype(F32)
    p2 = r1.astype(BF16)
    p3 = (r1 - p2.astype(F32)).astype(BF16)
    n = x.shape[1]
    out = _dot(m01_bf16, jnp.concatenate([p1, p2, p3], axis=1))
    return out[:, :n] + out[:, n:2 * n] + out[:, 2 * n:]


def _iota(shape, dim):
    return lax.broadcasted_iota(jnp.int32, shape, dim)


def _segsum(x, e_bf16):
    hi = x.astype(BF16)
    lo = (x - hi.astype(F32)).astype(BF16)
    return _dot(hi, e_bf16) + _dot(lo, e_bf16)


def _vec_spec(n, l):
    return pl.BlockSpec((None, 1, n), lambda *_: (l, 0, 0))


def _mat_spec(r, c, l):
    return pl.BlockSpec((None, r, c), lambda *_: (l, 0, 0))


def _ffn_body(x_ref, gpre_ref, wi_ref, wo_ref, gpost_ref, o_ref, *, n_split):
    x = x_ref[...]
    h = _rms(x, gpre_ref[...]).astype(BF16)
    ff = wo_ref.shape[0]
    tf = ff // n_split
    y = None
    for j in range(n_split):
        g = _dot(h, wi_ref[:, j * tf:(j + 1) * tf])
        u = _dot(h, wi_ref[:, ff + j * tf:ff + (j + 1) * tf])
        act = (g * _sigmoid(g) * u).astype(BF16)
        part = _dot(act, wo_ref[j * tf:(j + 1) * tf, :])
        y = part if y is None else y + part
    o_ref[...] = x + 0.5 * _rms(y, gpost_ref[...])


def _ffn(x, ng, w_in, w_out, l, k, tm, n_split):
    T, D = x.shape
    ff = w_out.shape[2]
    g_pre, g_post = l * 8 + 6 * k, l * 8 + 6 * k + 1
    return pl.pallas_call(
        functools.partial(_ffn_body, n_split=n_split),
        grid=(T // tm,),
        in_specs=[
            pl.BlockSpec((tm, D), lambda i: (i, 0)),
            pl.BlockSpec((None, 1, D), lambda i: (g_pre, 0, 0)),
            pl.BlockSpec((None, None, D, 2 * ff), lambda i: (l, k, 0, 0),
                         pipeline_mode=pl.Buffered(1)),
            pl.BlockSpec((None, None, ff, D), lambda i: (l, k, 0, 0), pipeline_mode=pl.Buffered(1)),
            pl.BlockSpec((None, 1, D), lambda i: (g_post, 0, 0)),
        ],
        out_specs=pl.BlockSpec((tm, D), lambda i: (i, 0)),
        out_shape=jax.ShapeDtypeStruct((T, D), F32),
        compiler_params=_cparams(1),
        name="ffn",
    )(x, ng, w_in, w_out, ng)


def _mixer_proj_body(x_ref, g_ref, wa_ref, wb_ref, wc_ref, c0_ref, s0_ref, qn_ref, kvn_ref,
                     wqa_ref, wqb_ref, wk_ref, wv_ref, one_ref,
                     zb_ref, zc_ref, q_ref, k_ref, vt_ref):
    h = _rms(x_ref[...], g_ref[...]).astype(BF16)
    zb_ref[...] = _dot(h, wb_ref[...])
    zc_ref[...] = _dot(h, wc_ref[...])
    _mla_prep(_dot(h, wa_ref[...]), c0_ref, s0_ref, qn_ref, kvn_ref, wqa_ref, wqb_ref, wk_ref,
              wv_ref, one_ref, q_ref, k_ref, vt_ref)


def _mixer_proj(x, ng, g_idx, pp, c0, s0, qn, kvn, l, tk):
    T, D = x.shape
    vw = MLA_HEADS * MLA_VSLOT
    tok = lambda n: pl.BlockSpec((tk, n), lambda i: (i, 0))
    res = lambda r, c: pl.BlockSpec((None, r, c), lambda i: (l, 0, 0), pipeline_mode=pl.Buffered(1))
    nb, nc = pp["wb"].shape[2], pp["wc"].shape[2]
    one_lane = jnp.asarray((np.arange(vw) % MLA_VSLOT == MLA_ONE).astype(np.float32))[None, :]
    return pl.pallas_call(
        _mixer_proj_body,
        grid=(T // tk,),
        in_specs=[tok(D), pl.BlockSpec((None, 1, D), lambda i: (g_idx, 0, 0)),
                  res(D, pp["wa"].shape[2]), res(D, nb), res(D, nc),
                  tok(MLA_SLOT), tok(MLA_SLOT), _vec_spec(MLA_Q_RANK, l), _vec_spec(MLA_KV_RANK, l),
                  _mat_spec(MLA_Q_RANK, MLA_W, l), _mat_spec(MLA_Q_RANK, MLA_W, l),
                  _mat_spec(MLA_KV_RANK, MLA_W, l), _mat_spec(MLA_KV_RANK, vw, l),
                  pl.BlockSpec((1, vw), lambda i: (0, 0))],
        out_specs=[tok(nb), tok(nc), tok(MLA_W), tok(MLA_W),
                   pl.BlockSpec((None, vw, tk), lambda i: (i, 0, 0))],
        out_shape=[jax.ShapeDtypeStruct((T, nb), F32), jax.ShapeDtypeStruct((T, nc), F32),
                   jax.ShapeDtypeStruct((T, MLA_W), BF16), jax.ShapeDtypeStruct((T, MLA_W), BF16),
                   jax.ShapeDtypeStruct((T // tk, vw, tk), BF16)],
        compiler_params=_cparams(1),
        name="mixer_proj",
    )(x, ng, pp["wa"], pp["wb"], pp["wc"], c0, s0, qn, kvn, pp["wqa"], pp["wqb"], pp["wk"],
      pp["wv"], one_lane)


def _rope_table_body(pos_ref, invf_ref, c_ref, s_ref):
    ang = pos_ref[...] * invf_ref[...]
    lane = _iota(ang.shape, 1)
    rot = (lane >= MLA_NOPE) & (lane < MLA_NOPE + MLA_ROPE)
    c_ref[...] = jnp.where(lane < MLA_NOPE, 1.0, jnp.where(rot, jnp.cos(ang), 0.0))
    s_ref[...] = jnp.where(rot, jnp.sin(ang), 0.0)


def _rope_tables(pos_f32, invf_lane, tm):
    T = pos_f32.shape[0]
    return pl.pallas_call(
        _rope_table_body,
        grid=(T // tm,),
        in_specs=[pl.BlockSpec((tm, 1), lambda i: (i, 0)),
                  pl.BlockSpec((1, MLA_SLOT), lambda i: (0, 0))],
        out_specs=[pl.BlockSpec((tm, MLA_SLOT), lambda i: (i, 0))] * 2,
        out_shape=[jax.ShapeDtypeStruct((T, MLA_SLOT), F32)] * 2,
        compiler_params=_cparams(1),
        name="rope_tables",
    )(pos_f32, invf_lane)


def _mla_prep(za, c0_ref, s0_ref, qn_ref, kvn_ref, wqa_ref, wqb_ref, wk_ref, wv_ref,
              one_ref, q_ref, k_ref, vt_ref):
    cq = za[:, :MLA_Q_RANK]
    ckv = za[:, MLA_Q_RANK:MLA_Q_RANK + MLA_KV_RANK]
    kra = za[:, 384:512]
    krb = za[:, 512:640]
    cqn = _rms(cq, qn_ref[...]).astype(BF16)
    ckvn = _rms(ckv, kvn_ref[...]).astype(BF16)
    c0 = c0_ref[...]
    s0 = s0_ref[...]
    c8 = jnp.concatenate([c0] * MLA_HEADS, axis=1)
    s8 = jnp.concatenate([s0] * MLA_HEADS, axis=1)
    scale = (MLA_NOPE + MLA_ROPE) ** -0.5 * LOG2_E
    q = (_dot(cqn, wqa_ref[...]) * c8 + _dot(cqn, wqb_ref[...]) * s8) * scale
    q_ref[...] = q.astype(BF16)
    krot = kra * c0 + krb * s0
    k = _dot(ckvn, wk_ref[...]) + jnp.concatenate([krot] * MLA_HEADS, axis=1)
    k_ref[...] = k.astype(BF16)
    v = _dot(ckvn, wv_ref[...]) + one_ref[...]
    vt_ref[...] = v.T.astype(BF16)


def _mla_attn_body(q_ref, k_ref, vt_ref, o_ref, s_ref, m_ref, acc_ref, *, tq, tk):
    qi = pl.program_id(2)
    m_ref[...] = jnp.full_like(m_ref, -1e30)
    acc_ref[...] = jnp.zeros_like(acc_ref)
    heads = [slice(hh * MLA_SLOT, (hh + 1) * MLA_SLOT) for hh in range(MLA_HPS)]

    def produce(slot, ki):
        r0 = pl.multiple_of(ki * tk, tk)
        for hh, sl in enumerate(heads):
            s_ref[slot, hh] = _dot_nt(k_ref[pl.ds(r0, tk), sl], q_ref[:, sl])

    def consume(slot, ki, diag):
        for hh, sl in enumerate(heads):
            st = s_ref[slot, hh]
            if diag is not None:
                key = _iota(st.shape, 0) + diag * tk
                st = jnp.where(key <= _iota(st.shape, 1), st, -1e30)
            m_old = m_ref[hh]
            m_new = jnp.maximum(m_old, jnp.max(st, axis=0, keepdims=True))
            p = jnp.exp2(st - m_new).astype(BF16)
            vt = vt_ref[ki, hh * MLA_VSLOT:(hh + 1) * MLA_VSLOT, :]
            acc_ref[hh] = jnp.exp2(m_old - m_new) * acc_ref[hh] + _dot(vt, p)
            m_ref[hh] = m_new

    def body(j, carry):
        produce(1, 2 * j + 1)
        consume(0, 2 * j, None)
        produce(0, 2 * j + 2)
        consume(1, 2 * j + 1, None)
        return carry

    produce(0, 0)
    lax.fori_loop(0, qi, body, 0)
    r1 = pl.multiple_of((2 * qi + 1) * tk, tk)
    late = slice(tk, tq)
    for hh, sl in enumerate(heads):
        s_ref[1, hh, :, 0:tk] = _dot_nt(k_ref[pl.ds(r1, tk), sl], q_ref[late, sl])
    consume(0, 2 * qi, 0)
    for hh, sl in enumerate(heads):
        st = s_ref[1, hh, :, 0:tk]
        st = jnp.where(_iota(st.shape, 0) <= _iota(st.shape, 1), st, -1e30)
        m_old = m_ref[hh, :, late]
        m_new = jnp.maximum(m_old, jnp.max(st, axis=0, keepdims=True))
        p = jnp.exp2(st - m_new).astype(BF16)
        vt = vt_ref[2 * qi + 1, hh * MLA_VSLOT:(hh + 1) * MLA_VSLOT, :]
        acc_ref[hh, :, late] = jnp.exp2(m_old - m_new) * acc_ref[hh, :, late] + _dot(vt, p)
        m_ref[hh, :, late] = m_new
    for hh in range(MLA_HPS):
        acc = acc_ref[hh]
        o_t = acc / acc[MLA_ONE:MLA_ONE + 1, :]
        o_t = jnp.concatenate([o_t, jnp.zeros((MLA_SLOT - MLA_VSLOT, tq), F32)], axis=0)
        o_ref[:, hh * MLA_SLOT:(hh + 1) * MLA_SLOT] = o_t.T.astype(o_ref.dtype)


def _mla_attn(q, k, vt, B, S, tq, tk):
    assert tq == 2 * tk
    T = q.shape[0]
    nq, nk = S // tq, S // tk
    w = MLA_HPS * MLA_SLOT
    return pl.pallas_call(
        functools.partial(_mla_attn_body, tq=tq, tk=tk),
        grid=(B, MLA_HEADS // MLA_HPS, nq),
        in_specs=[
            pl.BlockSpec((tq, w), lambda b, h, i: (b * nq + i, h)),
            pl.BlockSpec((S, w), lambda b, h, i: (b, h)),
            pl.BlockSpec((nk, MLA_HPS * MLA_VSLOT, tk), lambda b, h, i: (b, h, 0)),
        ],
        out_specs=pl.BlockSpec((tq, w), lambda b, h, i: (b * nq + i, h)),
        out_shape=jax.ShapeDtypeStruct((T, MLA_W), BF16),
        scratch_shapes=[pltpu.VMEM((2, MLA_HPS, tk, tq), F32),
                        pltpu.VMEM((MLA_HPS, 1, tq), F32),
                        pltpu.VMEM((MLA_HPS, MLA_VSLOT, tq), F32)],
        compiler_params=_cparams(3),
        name="mla_attn",
    )(q, k, vt)


def _gla_pair_masks():
    i = np.arange(CHUNK)[:, None]
    j = np.arange(GLA_HEADS * CHUNK)[None, :] % CHUNK
    same = [(i >> (6 - lv)) == (j >> (6 - lv)) for lv in range(GLA_LEVELS)]
    return jnp.asarray(np.stack(same + [i == j]).astype(np.float32))


def _gla_stages(v_ref, g_ref, q_ref, k_ref, a_ref, walpha_ref, balpha_ref, gn_ref, pair_ref,
                o_ref, ht_ref, n_batch):
    C = CHUNK
    walpha_hi, walpha_lo = walpha_ref[0], walpha_ref[1]
    balpha = balpha_ref[...]
    gn = gn_ref[...]
    tri = (_iota((C, C), 1) <= _iota((C, C), 0)).astype(BF16)
    st_r, st_c = _iota((GLA_HEADS * C, GLA_QK), 0), _iota((GLA_HEADS * C, GLA_QK), 1)
    head_mask = ((st_r >> 6) == (st_c >> 6)).astype(BF16)
    sv_r, sv_c = _iota((GLA_HEADS * C, GLA_W), 0), _iota((GLA_HEADS * C, GLA_W), 1)
    value_mask = ((sv_r >> 6) == (sv_c >> 7)).astype(BF16)
    ht_r, ht_c = _iota((GLA_W, GLA_QK), 0), _iota((GLA_W, GLA_QK), 1)
    state_mask = ((ht_r >> 7) == (ht_c >> 6)).astype(F32)
    tok = _iota((C, GLA_QK), 0)
    halves = [C >> (lv + 1) for lv in range(GLA_LEVELS)]

    def stack(x):
        return jnp.concatenate([x.astype(BF16)] * GLA_HEADS, axis=0) * head_mask

    def level_operands(q, k, cb):
        last = {1: cb}
        for s in halves[:0:-1]:
            f = last[s]
            last[2 * s] = jnp.where((tok & s) != 0, f, pltpu.roll(f, C - s, 0))
        ops = []
        for hs in halves:
            f = last[hs]
            right = (tok & hs) != 0
            d = cb - jnp.where(right, pltpu.roll(f, hs, 0), f)
            ql = q * jnp.exp(jnp.where(right, d, GLA_NEG))
            kl = k * jnp.exp(jnp.where(right, GLA_NEG, -d))
            ops.append((ql.astype(BF16), stack(kl)))
        ops.append((q.astype(BF16), stack(k)))
        return ops

    def stages(c):
        cbs = [(b, pl.ds(pl.multiple_of((c * REC_CPI + sub) * C, C), C))
               for sub in range(REC_CPI) for b in range(n_batch)]
        xs = [_dot_split(a_ref[b, rows, :], walpha_hi, walpha_lo) + balpha for b, rows in cbs]
        yield
        cum = [_dot_01(tri, -_softplus(-x) * (1.0 / GLA_GATE_NORM)) for x in xs]
        yield
        st = []
        for (b, rows), cb in zip(cbs, cum):
            b_last = cb[C - 1:C, :]
            q = q_ref[b, rows, :] * GLA_DK ** -0.5
            k = k_ref[b, rows, :]
            vb = v_ref[b, rows, :].astype(BF16)
            st.append(dict(
                v=vb,
                bd_v=jnp.concatenate([vb] * GLA_HEADS, axis=0) * value_mask,
                qe=(q * jnp.exp(cb)).astype(BF16),
                ke=(k * jnp.exp(b_last - cb)).astype(BF16),
                gam=jnp.exp(b_last),
                ops=level_operands(q, k, cb)))
        scs = [None] * len(cbs)
        for lv in range(GLA_LEVELS + 1):
            for i, s in enumerate(st):
                ql, kl = s["ops"][lv]
                part = _dot_nt(ql, kl) * pair_ref[lv]
                scs[i] = part if lv == 0 else scs[i] + part
            yield
        intra = [_dot(sc.astype(BF16), s["bd_v"]) for sc, s in zip(scs, st)]
        yield
        for (b, rows), s, o_intra in zip(cbs, st, intra):
            ht = ht_ref[b]
            o = o_intra + _dot_nt(s["qe"], ht.astype(BF16))
            ht_ref[b] = ht * s["gam"] + state_mask * _dot_tn(s["v"], s["ke"])
            outs = []
            for h in range(GLA_HEADS):
                oh = o[:, h * GLA_DV:(h + 1) * GLA_DV]
                outs.append(oh * lax.rsqrt(jnp.mean(oh * oh, axis=-1, keepdims=True) + NORM_EPS))
            on = jnp.concatenate(outs, axis=1) * gn
            g = g_ref[b, rows, :]
            o_ref[b, rows, :] = (on * (g * _sigmoid(g))).astype(o_ref.dtype)
            yield

    return stages


def _rwkv_stages(z_ref, mu_ref, w0_ref, wdec_ref, a0_ref, wiclr_ref, wgate_ref, kk_ref, ka_ref,
                 rk_ref, lnw_ref, lnb_ref, o_ref, h_ref, zlast_ref, n_batch):
    C, W, Q = CHUNK, RWKV_W, RWKV_QUAD
    mu = mu_ref[...]
    w0, a0 = w0_ref[...], a0_ref[...]
    wdec_hi, wdec_lo = wdec_ref[0], wdec_ref[1]
    wiclr_hi, wiclr_lo = wiclr_ref[0], wiclr_ref[1]
    wgate = wgate_ref[...]
    k_k, k_a, r_k = kk_ref[...], ka_ref[...], rk_ref[...]
    ln_w, ln_b = lnw_ref[...], lnb_ref[...]

    tri = (_iota((C, C), 1) <= _iota((C, C), 0)).astype(BF16)
    sq_r, sq_c = _iota((Q, Q), 0), _iota((Q, Q), 1)
    head_blk = ((sq_r >> 6) == (sq_c >> 6)).astype(F32)
    eye_q = (sq_r == sq_c).astype(F32)
    e_seg = head_blk.astype(BF16)
    wd_t, wd_s = _iota((C, Q), 0), _iota((C, Q), 1) & (C - 1)
    strict = (wd_s < wd_t).astype(F32)
    incl = (wd_s <= wd_t).astype(F32)
    eye_wide = (wd_s == wd_t).astype(F32)
    row0 = _iota((C, RWKV_COLS_PAD), 0) == 0

    def bd(x):
        return jnp.concatenate([x.astype(BF16)] * 4, axis=0) * e_seg

    def quad(x, i):
        return x[:, i * Q:(i + 1) * Q]

    def segsum(x):
        return jnp.concatenate([_segsum(quad(x, i), e_seg) for i in range(RWKV_NQ)], axis=1)

    def chunk_one(b, rows):
        z = z_ref[b, rows, :]
        zp = jnp.where(row0, zlast_ref[b, 0:1, :], pltpu.roll(z, 1, 0))
        zlast_ref[b, 0:1, :] = z[C - 1:C, :]
        z = z + mu * (zp - z)
        r = z[:, 0:W]
        k = z[:, W:2 * W]
        v = z[:, 2 * W:3 * W]
        m0 = z[:, 3 * W:3 * W + 128]
        m12 = z[:, 3 * W + 128:3 * W + 384]
        w_log = -_softplus(-(w0 + _dot_split(jnp.tanh(m0), wdec_hi, wdec_lo))) - 0.5
        lw = -jnp.exp(w_log)
        a = _sigmoid(a0 + _dot_split(m0, wiclr_hi, wiclr_lo))
        g = _dot(_sigmoid(m12).astype(BF16), wgate)
        kk = k * k_k
        kkn = kk * lax.rsqrt(jnp.maximum(segsum(kk * kk), 1e-24))
        k2 = k * (1.0 + (a - 1.0) * k_a)
        beta = kkn * a
        cs = _dot_01(tri, lw)
        c_last = cs[C - 1:C, :]
        dec_in = jnp.exp(-cs)
        dec_out = jnp.exp(c_last - cs)
        kt = kkn * jnp.exp(cs - lw)
        rt = r * jnp.exp(cs)
        bh = beta * dec_in
        kh = k2 * dec_in
        kbar = k2 * dec_out
        bbar = beta * dec_out
        gam = jnp.exp(c_last)
        chains = [dict(b=b, i=i, kt=quad(kt, i), rt=quad(rt, i), v=quad(v, i), bh=quad(bh, i),
                       kh=quad(kh, i), kbar=quad(kbar, i), bbar=quad(bbar, i), gam=quad(gam, i))
                  for i in range(RWKV_NQ)]
        return dict(r=r, k2=k2, v=v, g=g), chains

    def epilogue(b, rows, tok, y):
        mean = segsum(y) * (1.0 / RWKV_N)
        d = y - mean
        var = segsum(d * d) * (1.0 / RWKV_N)
        yn = d * lax.rsqrt(var + RWKV_GN_EPS) * ln_w + ln_b
        bonus = segsum(tok["r"] * tok["k2"] * r_k) * tok["v"]
        o_ref[b, rows, :] = ((yn + bonus) * tok["g"]).astype(o_ref.dtype)

    def stages(c):
        groups, chains = [], []
        for sub in range(REC_CPI):
            rows = pl.ds(pl.multiple_of((c * REC_CPI + sub) * C, C), C)
            for b in range(n_batch):
                tok, ch = chunk_one(b, rows)
                groups.append((b, rows, tok, ch))
                chains += ch
                yield
        for ch in chains:
            lhs = jnp.concatenate([ch["kt"], ch["rt"]], axis=0).astype(BF16)
            rhs = jnp.concatenate([bd(ch["bh"]), bd(ch["kh"])], axis=0)
            sc = _dot_nt(lhs, rhs)
            ch["a_kk"] = sc[:C, Q:] * strict
            ch["a_rb"] = sc[C:, :Q] * incl
            ch["a_rk"] = sc[C:, Q:] * incl
            ch["n"] = -(sc[:C, :Q] * strict)
            ch["t"] = eye_wide + ch["n"]
        yield
        for ch in chains:
            ch["p"] = _dot(ch["n"].astype(BF16), bd(ch["n"]))
            ch["bd_v"] = bd(ch["v"])
            ch["akk_v"] = _dot(ch["a_kk"].astype(BF16), ch["bd_v"])
        yield
        for _ in range(4):
            for ch in chains:
                tp = _dot(jnp.concatenate([ch["t"], ch["p"]], axis=0).astype(BF16), bd(ch["p"]))
                ch["t"] = ch["t"] + tp[:C]
                ch["p"] = tp[C:]
            yield
        for ch in chains:
            ch["t"] = ch["t"] + _dot(ch["t"].astype(BF16), bd(ch["p"]))
        yield
        for ch in chains:
            w12 = _dot(ch["t"].astype(BF16),
                       jnp.concatenate([bd(ch["akk_v"]), bd(ch["kt"])], axis=1))
            ch["w1"], ch["w2"] = w12[:, :Q], w12[:, Q:]
        yield
        for ch in chains:
            w1, w2, bbar = ch["w1"], ch["w2"], ch["bbar"]
            ch["y0"] = _dot(jnp.concatenate([ch["a_rk"], ch["a_rb"]], axis=1).astype(BF16),
                            jnp.concatenate([ch["bd_v"], bd(-w1)], axis=0))
            ch["rp"] = (ch["rt"] - _dot(ch["a_rb"].astype(BF16), bd(w2))).astype(BF16)
            ch["p_bd"] = (eye_q * ch["gam"] - head_blk * _dot_tn(
                bbar.astype(BF16), w2.astype(BF16))).astype(BF16)
            ch["q_bd"] = head_blk * _dot_tn(
                jnp.concatenate([ch["kbar"], -bbar], axis=0).astype(BF16),
                jnp.concatenate([ch["v"], w1], axis=0).astype(BF16))
        yield
        for b, rows, tok, chs in groups:
            ys = []
            for ch in chs:
                hb = h_ref[b, ch["i"]].astype(BF16)
                ys.append(_dot(ch["rp"], hb) + ch["y0"])
                h_ref[b, ch["i"]] = _dot(ch["p_bd"], hb) + ch["q_bd"]
            epilogue(b, rows, tok, jnp.concatenate(ys, axis=1))
            yield

    return stages


N_GLA_IN, N_RWKV_IN = 9, 12


def _recur_body(*refs, n_chunks, n_batch):
    gla_in = refs[:N_GLA_IN]
    rwkv_in = refs[N_GLA_IN:N_GLA_IN + N_RWKV_IN]
    ob_ref, oc_ref, ht_ref, h_ref, zlast_ref = refs[N_GLA_IN + N_RWKV_IN:]

    @pl.when(pl.program_id(0) == 0)
    def _():
        ht_ref[...] = jnp.zeros_like(ht_ref)
        h_ref[...] = jnp.zeros_like(h_ref)
        zlast_ref[...] = jnp.zeros_like(zlast_ref)

    gla = _gla_stages(*gla_in, ob_ref, ht_ref, n_batch)
    rwkv = _rwkv_stages(*rwkv_in, oc_ref, h_ref, zlast_ref, n_batch)

    def chunk(c, carry):
        live = [rwkv(c), gla(c)]
        while live:
            live = [g for g in live if next(g, True) is None]
        return carry

    lax.fori_loop(0, n_chunks // REC_CPI, chunk, 0)


def _recur(zb, zc, gp, rp, l, B, S, ts):
    ns = S // ts
    W, Q = RWKV_W, RWKV_QUAD
    blk = lambda n, j: pl.BlockSpec((B, ts, n), lambda s: (0, s, j))
    split = lambda c: pl.BlockSpec((None, 2, 128, c), lambda s: (l, 0, 0, 0))
    zb = zb.reshape(B, S, zb.shape[1])
    o_b, o_c = pl.pallas_call(
        functools.partial(_recur_body, n_chunks=ts // CHUNK, n_batch=B),
        grid=(ns,),
        in_specs=[
            blk(GLA_W, 0),
            blk(GLA_W, 1),
            blk(GLA_QK, 4),
            blk(GLA_QK, 5),
            blk(128, 12),
            split(GLA_QK), _vec_spec(GLA_QK, l), _vec_spec(GLA_W, l),
            pl.BlockSpec((GLA_LEVELS + 1, CHUNK, GLA_HEADS * CHUNK), lambda s: (0, 0, 0)),
            blk(RWKV_COLS_PAD, 0),
            _vec_spec(RWKV_COLS_PAD, l), _vec_spec(W, l), split(W),
            _vec_spec(W, l), split(W), _mat_spec(256, W, l),
            _vec_spec(W, l), _vec_spec(W, l), _vec_spec(W, l), _vec_spec(W, l), _vec_spec(W, l),
        ],
        out_specs=[blk(GLA_W, 0), blk(W, 0)],
        out_shape=[jax.ShapeDtypeStruct((B, S, GLA_W), BF16), jax.ShapeDtypeStruct((B, S, W), BF16)],
        scratch_shapes=[pltpu.VMEM((B, GLA_W, GLA_QK), F32),
                        pltpu.VMEM((B, RWKV_NQ, Q, Q), F32),
                        pltpu.VMEM((B, 8, RWKV_COLS_PAD), F32)],
        compiler_params=_cparams(1),
        name="recurrences",
    )(zb, zb, zb, zb, zb, gp["walpha"], gp["balpha"], gp["gn"], _gla_pair_masks(),
      zc.reshape(B, S, RWKV_COLS_PAD), rp["mu"], rp["w0"], rp["wdec"], rp["a0"], rp["wiclr"],
      rp["wgate"], rp["k_k"], rp["k_a"], rp["r_k"], rp["ln_w"], rp["ln_b"])
    return o_b.reshape(B * S, GLA_W), o_c.reshape(B * S, W)


def _merge_body(x_ref, oa_ref, ob_ref, oc_ref, gpre_ref, wd_ref, wa_ref, wb_ref, wc_ref, wo_ref,
                gpost_ref, o_ref):
    x = x_ref[...]
    D = D_MODEL
    h = _rms(x, gpre_ref[...]).astype(BF16)
    branches = ((oa_ref[...], wa_ref), (ob_ref[...], wb_ref), (oc_ref[...], wc_ref))
    y = None
    for c in range(0, D, MERGE_COLS):
        cols = slice(c, c + MERGE_COLS)
        merged = None
        for j, (o, w_r) in enumerate(branches):
            gate = _sigmoid(_dot(h, wd_ref[:, j * D + c:j * D + c + MERGE_COLS]))
            term = gate * _dot(o, w_r[:, cols])
            merged = term if merged is None else merged + term
        part = _dot(merged.astype(BF16), wo_ref[cols, :])
        y = part if y is None else y + part
    o_ref[...] = x + _rms(y, gpost_ref[...])


def _merge(x, oa, ob, oc, wd, wa, wb, wc, wo, ng, l, tm):
    T, D = x.shape
    tok = lambda n: pl.BlockSpec((tm, n), lambda i: (i, 0))
    res = lambda r, c: pl.BlockSpec((None, r, c), lambda i: (l, 0, 0), pipeline_mode=pl.Buffered(1))
    gain = lambda j: pl.BlockSpec((None, 1, D), lambda i: (l * 8 + j, 0, 0))
    return pl.pallas_call(
        _merge_body,
        grid=(T // tm,),
        in_specs=[tok(D), tok(MLA_W), tok(GLA_W), tok(RWKV_W), gain(2), res(D, N_BRANCH * D),
                  res(MLA_W, D), res(GLA_W, D), res(RWKV_W, D), res(D, D), gain(3)],
        out_specs=tok(D),
        out_shape=jax.ShapeDtypeStruct((T, D), F32),
        compiler_params=_cparams(1),
        name="merge",
    )(x, oa, ob, oc, ng, wd, wa, wb, wc, wo, ng)


def _mem_kv_body(mem_ref, g_ref, w_ref, o_ref):
    o_ref[...] = _dot(_rms(mem_ref[...], g_ref[...]).astype(BF16), w_ref[...]).astype(o_ref.dtype)


def _mem_kv(mem, mem_norm, wkv, l):
    B, M, D = mem.shape
    return pl.pallas_call(
        _mem_kv_body,
        grid=(B,),
        in_specs=[pl.BlockSpec((None, M, D), lambda b: (b, 0, 0)), _vec_spec(D, l),
                  _mat_spec(D, 2 * D, l)],
        out_specs=pl.BlockSpec((None, M, 2 * D), lambda b: (b, 0, 0)),
        out_shape=jax.ShapeDtypeStruct((B, M, 2 * D), BF16),
        compiler_params=_cparams(1),
        name="mem_kv",
    )(mem, mem_norm, wkv)


def _mem_attn_body(x_ref, gpre_ref, wq_ref, kv_ref, wo_ref, gpost_ref, o_ref):
    x = x_ref[...]
    D = D_MODEL
    h = _rms(x, gpre_ref[...]).astype(BF16)
    q = (_dot(h, wq_ref[...]) * (MEM_HD ** -0.5 * LOG2_E)).astype(BF16)
    kv = kv_ref[...]
    outs = []
    for hh in range(MEM_HEADS):
        sl = slice(hh * MEM_HD, (hh + 1) * MEM_HD)
        s = _dot_nt(q[:, sl], kv[:, sl])
        p = jnp.exp2(s - jnp.max(s, axis=-1, keepdims=True))
        o = _dot(p.astype(BF16), kv[:, D + hh * MEM_HD:D + (hh + 1) * MEM_HD])
        outs.append(o / jnp.sum(p, axis=-1, keepdims=True))
    o = jnp.concatenate(outs, axis=1).astype(BF16)
    o_ref[...] = x + _rms(_dot(o, wo_ref[...]), gpost_ref[...])


def _mem_attn(x, ng, wq, kv, wo, l, S, tm):
    T, D = x.shape
    M = kv.shape[1]
    per_b = S // tm
    return pl.pallas_call(
        _mem_attn_body,
        grid=(T // tm,),
        in_specs=[pl.BlockSpec((tm, D), lambda i: (i, 0)),
                  pl.BlockSpec((None, 1, D), lambda i: (l * 8 + 4, 0, 0)),
                  _mat_spec(D, D, l),
                  pl.BlockSpec((None, M, 2 * D), lambda i: (i // per_b, 0, 0)),
                  _mat_spec(D, D, l),
                  pl.BlockSpec((None, 1, D), lambda i: (l * 8 + 5, 0, 0))],
        out_specs=pl.BlockSpec((tm, D), lambda i: (i, 0)),
        out_shape=jax.ShapeDtypeStruct((T, D), F32),
        compiler_params=_cparams(1),
        name="mem_attn",
    )(x, ng, wq, kv, wo, ng)


def _prepare_params(w_in, mla_w_uq, mla_w_ukv, gla_w_alpha, gla_norm, rwkv_mu, rwkv_w_decay,
                    rwkv_w_iclr, rwkv_w_gate, w_branch):
    L, D = w_in.shape[0], w_in.shape[1]
    zc = lambda n: jnp.zeros((L, D, n), F32)
    o = 0
    cuts = {}
    for name, n in (("c_q", 256), ("c_kv", 128), ("k_rope", 32), ("gla_q", 256), ("gla_k", 256),
                    ("gla_v", 512), ("gla_g", 512), ("gla_a", 16), ("rwkv", 1824), ("gates", 3072)):
        cuts[name] = w_in[:, :, o:o + n]
        o += n
    kr = cuts["k_rope"]
    kr_b = jnp.concatenate([-kr[..., 16:], kr[..., :16]], axis=-1)
    wa = jnp.concatenate([cuts["c_q"], cuts["c_kv"], zc(64), kr, zc(32), zc(64), kr_b, zc(32)], -1)
    wb = jnp.concatenate([cuts["gla_v"], cuts["gla_g"], cuts["gla_q"], cuts["gla_k"],
                          cuts["gla_a"], zc(112)], -1)
    wc = jnp.concatenate([cuts["rwkv"], zc(RWKV_MISC - 288)], -1)
    wd = cuts["gates"]

    wuq = mla_w_uq.reshape(L, MLA_Q_RANK, MLA_HEADS, MLA_NOPE + MLA_ROPE)
    nope, rope = wuq[..., :MLA_NOPE], wuq[..., MLA_NOPE:]
    zq = lambda n: jnp.zeros((L, MLA_Q_RANK, MLA_HEADS, n), F32)
    wqa = jnp.concatenate([nope, rope, zq(32)], -1).reshape(L, MLA_Q_RANK, MLA_W)
    rope_b = jnp.concatenate([-rope[..., 16:], rope[..., :16]], -1)
    wqb = jnp.concatenate([zq(64), rope_b, zq(32)], -1).reshape(L, MLA_Q_RANK, MLA_W)
    wukv = mla_w_ukv.reshape(L, MLA_KV_RANK, MLA_HEADS, 128)
    zk = jnp.zeros((L, MLA_KV_RANK, MLA_HEADS, 64), F32)
    wk = jnp.concatenate([wukv[..., :64], zk], -1).reshape(L, MLA_KV_RANK, MLA_W)
    wv = jnp.concatenate([wukv[..., 64:], zk[..., :MLA_VSLOT - 64]], -1).reshape(
        L, MLA_KV_RANK, MLA_HEADS * MLA_VSLOT)

    bra = w_branch[:, :512].reshape(L, MLA_HEADS, 64, D)
    bra = jnp.concatenate([bra, jnp.zeros_like(bra)], axis=2).reshape(L, MLA_W, D)
    brb = w_branch[:, 512:512 + GLA_W]
    brc = w_branch[:, 512 + GLA_W:]

    walpha = jnp.concatenate(
        [gla_w_alpha, jnp.zeros((L, 128 - GLA_GATE_RANK, GLA_QK), F32)], axis=1)
    gn = jnp.tile(gla_norm, (1, GLA_HEADS))[:, None, :]
    mu = jnp.concatenate([rwkv_mu, jnp.zeros((L, RWKV_MISC - 288), F32)], -1)[:, None, :]
    zr = lambda n: jnp.zeros((L, n, RWKV_W), F32)
    def hi_lo(w):
        hi = w.astype(BF16)
        return jnp.stack([hi, (w - hi.astype(F32)).astype(BF16)], axis=1)

    wdec = hi_lo(jnp.concatenate([rwkv_w_decay, zr(64)], axis=1))
    wiclr = hi_lo(jnp.concatenate([zr(64), rwkv_w_iclr], axis=1))
    wgate = jnp.concatenate([rwkv_w_gate, zr(256 - RWKV_GATE_RANK)], axis=1)
    bf = lambda w: w.astype(BF16)
    return dict(wa=bf(wa), wb=bf(wb), wc=bf(wc), wd=bf(wd), wqa=bf(wqa), wqb=bf(wqb), wk=bf(wk),
                wv=bf(wv), bra=bf(bra), brb=bf(brb), brc=bf(brc), walpha=hi_lo(walpha), gn=gn, mu=mu,
                wdec=wdec, wiclr=wiclr, wgate=bf(wgate))


def kernel(x, mem, positions, norm_g, w_ffn_in, w_ffn_out, w_in, mla_q_norm, mla_w_uq, mla_kv_norm, mla_w_ukv, gla_w_alpha, gla_b_alpha, gla_norm, rwkv_mu, rwkv_w0, rwkv_w_decay, rwkv_a0, rwkv_w_iclr, rwkv_w_gate, rwkv_k_k, rwkv_k_a, rwkv_r_k, rwkv_ln_w, rwkv_ln_b, w_branch, w_out, mem_norm, mem_wq, mem_wkv, mem_wo):
    B, S, D = x.shape
    L = norm_g.shape[0]
    T = B * S
    tm = min(512, S)
    tq = min(1024, S)
    tk = tq // 2
    ts = min(512, S)
    tff = min(1024, S)
    nsp = 11

    pp = _prepare_params(w_in, mla_w_uq, mla_w_ukv, gla_w_alpha, gla_norm, rwkv_mu, rwkv_w_decay,
                         rwkv_w_iclr, rwkv_w_gate, w_branch)
    bf = lambda w: w.astype(BF16)
    ffn_in, ffn_out = bf(w_ffn_in), bf(w_ffn_out)
    wout, wq, wkv, wo = bf(w_out), bf(mem_wq), bf(mem_wkv), bf(mem_wo)
    ng = norm_g.reshape(L * 8, 1, D)
    row = lambda p: p[:, None, :]
    gl = dict(walpha=pp["walpha"], balpha=row(gla_b_alpha), gn=pp["gn"])
    rw = dict(mu=pp["mu"], w0=row(rwkv_w0), wdec=pp["wdec"], a0=row(rwkv_a0), wiclr=pp["wiclr"],
              wgate=pp["wgate"], k_k=row(rwkv_k_k), k_a=row(rwkv_k_a), r_k=row(rwkv_r_k),
              ln_w=row(rwkv_ln_w), ln_b=row(rwkv_ln_b))

    inv_freq = ROPE_THETA ** (-jnp.arange(0, MLA_ROPE, 2, dtype=F32) / MLA_ROPE)
    invf_lane = jnp.concatenate(
        [jnp.zeros((MLA_NOPE,), F32), inv_freq, inv_freq, jnp.zeros((32,), F32)])[None, :]
    c0, s0 = _rope_tables(positions.astype(F32).reshape(T, 1), invf_lane, tm)

    x = x.reshape(T, D)
    for l in range(L):
        x = _ffn(x, ng, ffn_in, ffn_out, l, 0, tff, nsp)
        zb, zc, q, k, vt = _mixer_proj(x, ng, l * 8 + 2, pp, c0, s0, row(mla_q_norm),
                                       row(mla_kv_norm), l, tk)
        o_a = _mla_attn(q, k, vt, B, S, tq, tk)
        o_b, o_c = _recur(zb, zc, gl, rw, l, B, S, ts)
        x = _merge(x, o_a, o_b, o_c, pp["wd"], pp["bra"], pp["brb"], pp["brc"], wout, ng, l, tff)
        kv = _mem_kv(mem, row(mem_norm), wkv, l)
        x = _mem_attn(x, ng, wq, kv, wo, l, S, tff)
        x = _ffn(x, ng, ffn_in, ffn_out, l, 1, tff, nsp)
    return x.reshape(B, S, D)
```

```python
import functools

import jax
import jax.numpy as jnp
import numpy as np
from jax import lax
from jax.experimental import pallas as pl
from jax.experimental.pallas import tpu as pltpu

F32 = jnp.float32
BF16 = jnp.bfloat16

D_MODEL = 1024
D_FF = 2816
NORM_EPS = 1e-6
LOG2_E = 1.4426950408889634
MLA_HEADS = 8
MLA_NOPE = 64
MLA_ROPE = 32
MLA_Q_RANK = 256
MLA_KV_RANK = 128
ROPE_THETA = 10000.0
MLA_SLOT = 128
MLA_W = MLA_HEADS * MLA_SLOT
MLA_HPS = 4
MLA_VSLOT = 80
MLA_ONE = 64
GLA_HEADS = 4
GLA_DK = 64
GLA_DV = 128
GLA_GATE_RANK = 16
GLA_GATE_NORM = 16.0
GLA_QK = GLA_HEADS * GLA_DK
GLA_W = GLA_HEADS * GLA_DV
GLA_LEVELS = 6
GLA_NEG = -1e30
RWKV_HEADS = 8
RWKV_N = 64
RWKV_DECAY_RANK = 64
RWKV_ICLR_RANK = 64
RWKV_GATE_RANK = 160
RWKV_GN_EPS = 64e-5
RWKV_W = RWKV_HEADS * RWKV_N
RWKV_MISC = 384
RWKV_COLS_PAD = 3 * RWKV_W + RWKV_MISC
RWKV_QUAD = 4 * RWKV_N
RWKV_NQ = RWKV_W // RWKV_QUAD
REC_CPI = 2
MEM_HEADS = 4
MEM_HD = D_MODEL // MEM_HEADS
N_BRANCH = 3
MERGE_COLS = 256
CHUNK = 64

VMEM_LIMIT_BYTES = 56 * 1024 * 1024


def _cparams(n_axes):
    return pltpu.CompilerParams(
        dimension_semantics=("arbitrary",) * n_axes,
        vmem_limit_bytes=VMEM_LIMIT_BYTES,
    )


def _dot(a, b):
    return jnp.dot(a, b, preferred_element_type=F32)


def _dot_nt(a, b):
    return lax.dot_general(a, b, (((1,), (1,)), ((), ())), preferred_element_type=F32)


def _dot_tn(a, b):
    return lax.dot_general(a, b, (((0,), (0,)), ((), ())), preferred_element_type=F32)


def _rms(x, g, eps=NORM_EPS):
    return x * lax.rsqrt(jnp.mean(x * x, axis=-1, keepdims=True) + eps) * g


def _sigmoid(x):
    return 1.0 / (1.0 + jnp.exp(-x))


def _softplus(x):
    return jnp.maximum(x, 0.0) + jnp.log(1.0 + jnp.exp(-jnp.abs(x)))


def _split2(x):
    hi = x.astype(BF16)
    return hi, (x - hi.astype(F32)).astype(BF16)


def _dot_split(x, w_hi, w_lo):
    hi, lo = _split2(x)
    m = x.shape[0]
    top = _dot(jnp.concatenate([hi, lo], axis=0), w_hi)
    return top[:m] + top[m:] + _dot(hi, w_lo)


def _dot_01(m01_bf16, x):
    p1 = x.astype(BF16)
    r1 = x - p1.astype(F32)
    p2 = r1.astype(BF16)
    p3 = (r1 - p2.astype(F32)).astype(BF16)
    n = x.shape[1]
    out = _dot(m01_bf16, jnp.concatenate([p1, p2, p3], axis=1))
    return out[:, :n] + out[:, n:2 * n] + out[:, 2 * n:]


def _iota(shape, dim):
    return lax.broadcasted_iota(jnp.int32, shape, dim)


def _segsum(x, e_bf16):
    hi = x.astype(BF16)
    lo = (x - hi.astype(F32)).astype(BF16)
    return _dot(hi, e_bf16) + _dot(lo, e_bf16)


def _vec_spec(n, l):
    return pl.BlockSpec((None, 1, n), lambda *_: (l, 0, 0))


def _mat_spec(r, c, l):
    return pl.BlockSpec((None, r, c), lambda *_: (l, 0, 0))


def _ffn_body(x_ref, gpre_ref, wi_ref, wo_ref, gpost_ref, o_ref, *, n_split):
    x = x_ref[...]
    h = _rms(x, gpre_ref[...]).astype(BF16)
    ff = wo_ref.shape[0]
    tf = ff // n_split
    y = None
    for j in range(n_split):
        g = _dot(h, wi_ref[:, j * tf:(j + 1) * tf])
        u = _dot(h, wi_ref[:, ff + j * tf:ff + (j + 1) * tf])
        act = (g * _sigmoid(g) * u).astype(BF16)
        part = _dot(act, wo_ref[j * tf:(j + 1) * tf, :])
        y = part if y is None else y + part
    o_ref[...] = x + 0.5 * _rms(y, gpost_ref[...])


def _ffn(x, ng, w_in, w_out, l, k, tm, n_split):
    T, D = x.shape
    ff = w_out.shape[2]
    g_pre, g_post = l * 8 + 6 * k, l * 8 + 6 * k + 1
    return pl.pallas_call(
        functools.partial(_ffn_body, n_split=n_split),
        grid=(T // tm,),
        in_specs=[
            pl.BlockSpec((tm, D), lambda i: (i, 0)),
            pl.BlockSpec((None, 1, D), lambda i: (g_pre, 0, 0)),
            pl.BlockSpec((None, None, D, 2 * ff), lambda i: (l, k, 0, 0),
                         pipeline_mode=pl.Buffered(1)),
            pl.BlockSpec((None, None, ff, D), lambda i: (l, k, 0, 0), pipeline_mode=pl.Buffered(1)),
            pl.BlockSpec((None, 1, D), lambda i: (g_post, 0, 0)),
        ],
        out_specs=pl.BlockSpec((tm, D), lambda i: (i, 0)),
        out_shape=jax.ShapeDtypeStruct((T, D), F32),
        compiler_params=_cparams(1),
        name="ffn",
    )(x, ng, w_in, w_out, ng)


def _mixer_proj_body(x_ref, g_ref, wa_ref, wb_ref, wc_ref, c0_ref, s0_ref, qn_ref, kvn_ref,
                     wqa_ref, wqb_ref, wk_ref, wv_ref, one_ref,
                     zb_ref, zc_ref, q_ref, k_ref, vt_ref):
    h = _rms(x_ref[...], g_ref[...]).astype(BF16)
    zb_ref[...] = _dot(h, wb_ref[...])
    zc_ref[...] = _dot(h, wc_ref[...])
    _mla_prep(_dot(h, wa_ref[...]), c0_ref, s0_ref, qn_ref, kvn_ref, wqa_ref, wqb_ref, wk_ref,
              wv_ref, one_ref, q_ref, k_ref, vt_ref)


def _mixer_proj(x, ng, g_idx, pp, c0, s0, qn, kvn, l, tk):
    T, D = x.shape
    vw = MLA_HEADS * MLA_VSLOT
    tok = lambda n: pl.BlockSpec((tk, n), lambda i: (i, 0))
    res = lambda r, c: pl.BlockSpec((None, r, c), lambda i: (l, 0, 0), pipeline_mode=pl.Buffered(1))
    nb, nc = pp["wb"].shape[2], pp["wc"].shape[2]
    one_lane = jnp.asarray((np.arange(vw) % MLA_VSLOT == MLA_ONE).astype(np.float32))[None, :]
    return pl.pallas_call(
        _mixer_proj_body,
        grid=(T // tk,),
        in_specs=[tok(D), pl.BlockSpec((None, 1, D), lambda i: (g_idx, 0, 0)),
                  res(D, pp["wa"].shape[2]), res(D, nb), res(D, nc),
                  tok(MLA_SLOT), tok(MLA_SLOT), _vec_spec(MLA_Q_RANK, l), _vec_spec(MLA_KV_RANK, l),
                  _mat_spec(MLA_Q_RANK, MLA_W, l), _mat_spec(MLA_Q_RANK, MLA_W, l),
                  _mat_spec(MLA_KV_RANK, MLA_W, l), _mat_spec(MLA_KV_RANK, vw, l),
                  pl.BlockSpec((1, vw), lambda i: (0, 0))],
        out_specs=[tok(nb), tok(nc), tok(MLA_W), tok(MLA_W),
                   pl.BlockSpec((None, vw, tk), lambda i: (i, 0, 0))],
        out_shape=[jax.ShapeDtypeStruct((T, nb), F32), jax.ShapeDtypeStruct((T, nc), F32),
                   jax.ShapeDtypeStruct((T, MLA_W), BF16), jax.ShapeDtypeStruct((T, MLA_W), BF16),
                   jax.ShapeDtypeStruct((T // tk, vw, tk), BF16)],
        compiler_params=_cparams(1),
        name="mixer_proj",
    )(x, ng, pp["wa"], pp["wb"], pp["wc"], c0, s0, qn, kvn, pp["wqa"], pp["wqb"], pp["wk"],
      pp["wv"], one_lane)


def _rope_table_body(pos_ref, invf_ref, c_ref, s_ref):
    ang = pos_ref[...] * invf_ref[...]
    lane = _iota(ang.shape, 1)
    rot = (lane >= MLA_NOPE) & (lane < MLA_NOPE + MLA_ROPE)
    c_ref[...] = jnp.where(lane < MLA_NOPE, 1.0, jnp.where(rot, jnp.cos(ang), 0.0))
    s_ref[...] = jnp.where(rot, jnp.sin(ang), 0.0)


def _rope_tables(pos_f32, invf_lane, tm):
    T = pos_f32.shape[0]
    return pl.pallas_call(
        _rope_table_body,
        grid=(T // tm,),
        in_specs=[pl.BlockSpec((tm, 1), lambda i: (i, 0)),
                  pl.BlockSpec((1, MLA_SLOT), lambda i: (0, 0))],
        out_specs=[pl.BlockSpec((tm, MLA_SLOT), lambda i: (i, 0))] * 2,
        out_shape=[jax.ShapeDtypeStruct((T, MLA_SLOT), F32)] * 2,
        compiler_params=_cparams(1),
        name="rope_tables",
    )(pos_f32, invf_lane)


def _mla_prep(za, c0_ref, s0_ref, qn_ref, kvn_ref, wqa_ref, wqb_ref, wk_ref, wv_ref,
              one_ref, q_ref, k_ref, vt_ref):
    cq = za[:, :MLA_Q_RANK]
    ckv = za[:, MLA_Q_RANK:MLA_Q_RANK + MLA_KV_RANK]
    kra = za[:, 384:512]
    krb = za[:, 512:640]
    cqn = _rms(cq, qn_ref[...]).astype(BF16)
    ckvn = _rms(ckv, kvn_ref[...]).astype(BF16)
    c0 = c0_ref[...]
    s0 = s0_ref[...]
    c8 = jnp.concatenate([c0] * MLA_HEADS, axis=1)
    s8 = jnp.concatenate([s0] * MLA_HEADS, axis=1)
    scale = (MLA_NOPE + MLA_ROPE) ** -0.5 * LOG2_E
    q = (_dot(cqn, wqa_ref[...]) * c8 + _dot(cqn, wqb_ref[...]) * s8) * scale
    q_ref[...] = q.astype(BF16)
    krot = kra * c0 + krb * s0
    k = _dot(ckvn, wk_ref[...]) + jnp.concatenate([krot] * MLA_HEADS, axis=1)
    k_ref[...] = k.astype(BF16)
    v = _dot(ckvn, wv_ref[...]) + one_ref[...]
    vt_ref[...] = v.T.astype(BF16)


def _mla_attn_body(q_ref, k_ref, vt_ref, o_ref, s_ref, m_ref, acc_ref, *, tq, tk):
    qi = pl.program_id(2)
    m_ref[...] = jnp.full_like(m_ref, -1e30)
    acc_ref[...] = jnp.zeros_like(acc_ref)
    heads = [slice(hh * MLA_SLOT, (hh + 1) * MLA_SLOT) for hh in range(MLA_HPS)]

    def produce(slot, ki):
        r0 = pl.multiple_of(ki * tk, tk)
        for hh, sl in enumerate(heads):
            s_ref[slot, hh] = _dot_nt(k_ref[pl.ds(r0, tk), sl], q_ref[:, sl])

    def consume(slot, ki, diag):
        for hh, sl in enumerate(heads):
            st = s_ref[slot, hh]
            if diag is not None:
                key = _iota(st.shape, 0) + diag * tk
                st = jnp.where(key <= _iota(st.shape, 1), st, -1e30)
            m_old = m_ref[hh]
            m_new = jnp.maximum(m_old, jnp.max(st, axis=0, keepdims=True))
            p = jnp.exp2(st - m_new).astype(BF16)
            vt = vt_ref[ki, hh * MLA_VSLOT:(hh + 1) * MLA_VSLOT, :]
            acc_ref[hh] = jnp.exp2(m_old - m_new) * acc_ref[hh] + _dot(vt, p)
            m_ref[hh] = m_new

    def body(j, carry):
        produce(1, 2 * j + 1)
        consume(0, 2 * j, None)
        produce(0, 2 * j + 2)
        consume(1, 2 * j + 1, None)
        return carry

    produce(0, 0)
    lax.fori_loop(0, qi, body, 0)
    r1 = pl.multiple_of((2 * qi + 1) * tk, tk)
    late = slice(tk, tq)
    for hh, sl in enumerate(heads):
        s_ref[1, hh, :, 0:tk] = _dot_nt(k_ref[pl.ds(r1, tk), sl], q_ref[late, sl])
    consume(0, 2 * qi, 0)
    for hh, sl in enumerate(heads):
        st = s_ref[1, hh, :, 0:tk]
        st = jnp.where(_iota(st.shape, 0) <= _iota(st.shape, 1), st, -1e30)
        m_old = m_ref[hh, :, late]
        m_new = jnp.maximum(m_old, jnp.max(st, axis=0, keepdims=True))
        p = jnp.exp2(st - m_new).astype(BF16)
        vt = vt_ref[2 * qi + 1, hh * MLA_VSLOT:(hh + 1) * MLA_VSLOT, :]
        acc_ref[hh, :, late] = jnp.exp2(m_old - m_new) * acc_ref[hh, :, late] + _dot(vt, p)
        m_ref[hh, :, late] = m_new
    for hh in range(MLA_HPS):
        acc = acc_ref[hh]
        o_t = acc / acc[MLA_ONE:MLA_ONE + 1, :]
        o_t = jnp.concatenate([o_t, jnp.zeros((MLA_SLOT - MLA_VSLOT, tq), F32)], axis=0)
        o_ref[:, hh * MLA_SLOT:(hh + 1) * MLA_SLOT] = o_t.T.astype(o_ref.dtype)


def _mla_attn(q, k, vt, B, S, tq, tk):
    assert tq == 2 * tk
    T = q.shape[0]
    nq, nk = S // tq, S // tk
    w = MLA_HPS * MLA_SLOT
    return pl.pallas_call(
        functools.partial(_mla_attn_body, tq=tq, tk=tk),
        grid=(B, MLA_HEADS // MLA_HPS, nq),
        in_specs=[
            pl.BlockSpec((tq, w), lambda b, h, i: (b * nq + i, h)),
            pl.BlockSpec((S, w), lambda b, h, i: (b, h)),
            pl.BlockSpec((nk, MLA_HPS * MLA_VSLOT, tk), lambda b, h, i: (b, h, 0)),
        ],
        out_specs=pl.BlockSpec((tq, w), lambda b, h, i: (b * nq + i, h)),
        out_shape=jax.ShapeDtypeStruct((T, MLA_W), BF16),
        scratch_shapes=[pltpu.VMEM((2, MLA_HPS, tk, tq), F32),
                        pltpu.VMEM((MLA_HPS, 1, tq), F32),
                        pltpu.VMEM((MLA_HPS, MLA_VSLOT, tq), F32)],
        compiler_params=_cparams(3),
        name="mla_attn",
    )(q, k, vt)


def _gla_pair_masks():
    i = np.arange(CHUNK)[:, None]
    j = np.arange(GLA_HEADS * CHUNK)[None, :] % CHUNK
    same = [(i >> (6 - lv)) == (j >> (6 - lv)) for lv in range(GLA_LEVELS)]
    return jnp.asarray(np.stack(same + [i == j]).astype(np.float32))


def _gla_stages(v_ref, g_ref, q_ref, k_ref, a_ref, walpha_ref, balpha_ref, gn_ref, pair_ref,
                o_ref, ht_ref, n_batch):
    C = CHUNK
    walpha_hi, walpha_lo = walpha_ref[0], walpha_ref[1]
    balpha = balpha_ref[...]
    gn = gn_ref[...]
    tri = (_iota((C, C), 1) <= _iota((C, C), 0)).astype(BF16)
    st_r, st_c = _iota((GLA_HEADS * C, GLA_QK), 0), _iota((GLA_HEADS * C, GLA_QK), 1)
    head_mask = ((st_r >> 6) == (st_c >> 6)).astype(BF16)
    sv_r, sv_c = _iota((GLA_HEADS * C, GLA_W), 0), _iota((GLA_HEADS * C, GLA_W), 1)
    value_mask = ((sv_r >> 6) == (sv_c >> 7)).astype(BF16)
    ht_r, ht_c = _iota((GLA_W, GLA_QK), 0), _iota((GLA_W, GLA_QK), 1)
    state_mask = ((ht_r >> 7) == (ht_c >> 6)).astype(F32)
    tok = _iota((C, GLA_QK), 0)
    halves = [C >> (lv + 1) for lv in range(GLA_LEVELS)]

    def stack(x):
        return jnp.concatenate([x.astype(BF16)] * GLA_HEADS, axis=0) * head_mask

    def level_operands(q, k, cb):
        last = {1: cb}
        for s in halves[:0:-1]:
            f = last[s]
            last[2 * s] = jnp.where((tok & s) != 0, f, pltpu.roll(f, C - s, 0))
        ops = []
        for hs in halves:
            f = last[hs]
            right = (tok & hs) != 0
            d = cb - jnp.where(right, pltpu.roll(f, hs, 0), f)
            ql = q * jnp.exp(jnp.where(right, d, GLA_NEG))
            kl = k * jnp.exp(jnp.where(right, GLA_NEG, -d))
            ops.append((ql.astype(BF16), stack(kl)))
        ops.append((q.astype(BF16), stack(k)))
        return ops

    def stages(c):
        cbs = [(b, pl.ds(pl.multiple_of((c * REC_CPI + sub) * C, C), C))
               for sub in range(REC_CPI) for b in range(n_batch)]
        xs = [_dot_split(a_ref[b, rows, :], walpha_hi, walpha_lo) + balpha for b, rows in cbs]
        yield
        cum = [_dot_01(tri, -_softplus(-x) * (1.0 / GLA_GATE_NORM)) for x in xs]
        yield
        st = []
        for (b, rows), cb in zip(cbs, cum):
            b_last = cb[C - 1:C, :]
            q = q_ref[b, rows, :] * GLA_DK ** -0.5
            k = k_ref[b, rows, :]
            vb = v_ref[b, rows, :].astype(BF16)
            st.append(dict(
                v=vb,
                bd_v=jnp.concatenate([vb] * GLA_HEADS, axis=0) * value_mask,
                qe=(q * jnp.exp(cb)).astype(BF16),
                ke=(k * jnp.exp(b_last - cb)).astype(BF16),
                gam=jnp.exp(b_last),
                ops=level_operands(q, k, cb)))
        scs = [None] * len(cbs)
        for lv in range(GLA_LEVELS + 1):
            for i, s in enumerate(st):
                ql, kl = s["ops"][lv]
                part = _dot_nt(ql, kl) * pair_ref[lv]
                scs[i] = part if lv == 0 else scs[i] + part
            yield
        intra = [_dot(sc.astype(BF16), s["bd_v"]) for sc, s in zip(scs, st)]
        yield
        for (b, rows), s, o_intra in zip(cbs, st, intra):
            ht = ht_ref[b]
            o = o_intra + _dot_nt(s["qe"], ht.astype(BF16))
            ht_ref[b] = ht * s["gam"] + state_mask * _dot_tn(s["v"], s["ke"])
            outs = []
            for h in range(GLA_HEADS):
                oh = o[:, h * GLA_DV:(h + 1) * GLA_DV]
                outs.append(oh * lax.rsqrt(jnp.mean(oh * oh, axis=-1, keepdims=True) + NORM_EPS))
            on = jnp.concatenate(outs, axis=1) * gn
            g = g_ref[b, rows, :]
            o_ref[b, rows, :] = (on * (g * _sigmoid(g))).astype(o_ref.dtype)
            yield

    return stages


def _rwkv_stages(z_ref, mu_ref, w0_ref, wdec_ref, a0_ref, wiclr_ref, wgate_ref, kk_ref, ka_ref,
                 rk_ref, lnw_ref, lnb_ref, o_ref, h_ref, zlast_ref, n_batch):
    C, W, Q = CHUNK, RWKV_W, RWKV_QUAD
    mu = mu_ref[...]
    w0, a0 = w0_ref[...], a0_ref[...]
    wdec_hi, wdec_lo = wdec_ref[0], wdec_ref[1]
    wiclr_hi, wiclr_lo = wiclr_ref[0], wiclr_ref[1]
    wgate = wgate_ref[...]
    k_k, k_a, r_k = kk_ref[...], ka_ref[...], rk_ref[...]
    ln_w, ln_b = lnw_ref[...], lnb_ref[...]

    tri = (_iota((C, C), 1) <= _iota((C, C), 0)).astype(BF16)
    sq_r, sq_c = _iota((Q, Q), 0), _iota((Q, Q), 1)
    head_blk = ((sq_r >> 6) == (sq_c >> 6)).astype(F32)
    eye_q = (sq_r == sq_c).astype(F32)
    e_seg = head_blk.astype(BF16)
    wd_t, wd_s = _iota((C, Q), 0), _iota((C, Q), 1) & (C - 1)
    strict = (wd_s < wd_t).astype(F32)
    incl = (wd_s <= wd_t).astype(F32)
    eye_wide = (wd_s == wd_t).astype(F32)
    row0 = _iota((C, RWKV_COLS_PAD), 0) == 0

    def bd(x):
        return jnp.concatenate([x.astype(BF16)] * 4, axis=0) * e_seg

    def quad(x, i):
        return x[:, i * Q:(i + 1) * Q]

    def segsum(x):
        return jnp.concatenate([_segsum(quad(x, i), e_seg) for i in range(RWKV_NQ)], axis=1)

    def chunk_one(b, rows):
        z = z_ref[b, rows, :]
        zp = jnp.where(row0, zlast_ref[b, 0:1, :], pltpu.roll(z, 1, 0))
        zlast_ref[b, 0:1, :] = z[C - 1:C, :]
        z = z + mu * (zp - z)
        r = z[:, 0:W]
        k = z[:, W:2 * W]
        v = z[:, 2 * W:3 * W]
        m0 = z[:, 3 * W:3 * W + 128]
        m12 = z[:, 3 * W + 128:3 * W + 384]
        w_log = -_softplus(-(w0 + _dot_split(jnp.tanh(m0), wdec_hi, wdec_lo))) - 0.5
        lw = -jnp.exp(w_log)
        a = _sigmoid(a0 + _dot_split(m0, wiclr_hi, wiclr_lo))
        g = _dot(_sigmoid(m12).astype(BF16), wgate)
        kk = k * k_k
        kkn = kk * lax.rsqrt(jnp.maximum(segsum(kk * kk), 1e-24))
        k2 = k * (1.0 + (a - 1.0) * k_a)
        beta = kkn * a
        cs = _dot_01(tri, lw)
        c_last = cs[C - 1:C, :]
        dec_in = jnp.exp(-cs)
        dec_out = jnp.exp(c_last - cs)
        kt = kkn * jnp.exp(cs - lw)
        rt = r * jnp.exp(cs)
        bh = beta * dec_in
        kh = k2 * dec_in
        kbar = k2 * dec_out
        bbar = beta * dec_out
        gam = jnp.exp(c_last)
        chains = [dict(b=b, i=i, kt=quad(kt, i), rt=quad(rt, i), v=quad(v, i), bh=quad(bh, i),
                       kh=quad(kh, i), kbar=quad(kbar, i), bbar=quad(bbar, i), gam=quad(gam, i))
                  for i in range(RWKV_NQ)]
        return dict(r=r, k2=k2, v=v, g=g), chains

    def epilogue(b, rows, tok, y):
        mean = segsum(y) * (1.0 / RWKV_N)
        d = y - mean
        var = segsum(d * d) * (1.0 / RWKV_N)
        yn = d * lax.rsqrt(var + RWKV_GN_EPS) * ln_w + ln_b
        bonus = segsum(tok["r"] * tok["k2"] * r_k) * tok["v"]
        o_ref[b, rows, :] = ((yn + bonus) * tok["g"]).astype(o_ref.dtype)

    def stages(c):
        groups, chains = [], []
        for sub in range(REC_CPI):
            rows = pl.ds(pl.multiple_of((c * REC_CPI + sub) * C, C), C)
            for b in range(n_batch):
                tok, ch = chunk_one(b, rows)
                groups.append((b, rows, tok, ch))
                chains += ch
                yield
        for ch in chains:
            lhs = jnp.concatenate([ch["kt"], ch["rt"]], axis=0).astype(BF16)
            rhs = jnp.concatenate([bd(ch["bh"]), bd(ch["kh"])], axis=0)
            sc = _dot_nt(lhs, rhs)
            ch["a_kk"] = sc[:C, Q:] * strict
            ch["a_rb"] = sc[C:, :Q] * incl
            ch["a_rk"] = sc[C:, Q:] * incl
            ch["n"] = -(sc[:C, :Q] * strict)
            ch["t"] = eye_wide + ch["n"]
        yield
        for ch in chains:
            ch["p"] = _dot(ch["n"].astype(BF16), bd(ch["n"]))
            ch["bd_v"] = bd(ch["v"])
            ch["akk_v"] = _dot(ch["a_kk"].astype(BF16), ch["bd_v"])
        yield
        for _ in range(4):
            for ch in chains:
                tp = _dot(jnp.concatenate([ch["t"], ch["p"]], axis=0).astype(BF16), bd(ch["p"]))
                ch["t"] = ch["t"] + tp[:C]
                ch["p"] = tp[C:]
            yield
        for ch in chains:
            ch["t"] = ch["t"] + _dot(ch["t"].astype(BF16), bd(ch["p"]))
        yield
        for ch in chains:
            w12 = _dot(ch["t"].astype(BF16),
                       jnp.concatenate([bd(ch["akk_v"]), bd(ch["kt"])], axis=1))
            ch["w1"], ch["w2"] = w12[:, :Q], w12[:, Q:]
        yield
        for ch in chains:
            w1, w2, bbar = ch["w1"], ch["w2"], ch["bbar"]
            ch["y0"] = _dot(jnp.concatenate([ch["a_rk"], ch["a_rb"]], axis=1).astype(BF16),
                            jnp.concatenate([ch["bd_v"], bd(-w1)], axis=0))
            ch["rp"] = (ch["rt"] - _dot(ch["a_rb"].astype(BF16), bd(w2))).astype(BF16)
            ch["p_bd"] = (eye_q * ch["gam"] - head_blk * _dot_tn(
                bbar.astype(BF16), w2.astype(BF16))).astype(BF16)
            ch["q_bd"] = head_blk * _dot_tn(
                jnp.concatenate([ch["kbar"], -bbar], axis=0).astype(BF16),
                jnp.concatenate([ch["v"], w1], axis=0).astype(BF16))
        yield
        for b, rows, tok, chs in groups:
            ys = []
            for ch in chs:
                hb = h_ref[b, ch["i"]].astype(BF16)
                ys.append(_dot(ch["rp"], hb) + ch["y0"])
                h_ref[b, ch["i"]] = _dot(ch["p_bd"], hb) + ch["q_bd"]
            epilogue(b, rows, tok, jnp.concatenate(ys, axis=1))
            yield

    return stages


N_GLA_IN, N_RWKV_IN = 9, 12


def _recur_body(*refs, n_chunks, n_batch):
    gla_in = refs[:N_GLA_IN]
    rwkv_in = refs[N_GLA_IN:N_GLA_IN + N_RWKV_IN]
    ob_ref, oc_ref, ht_ref, h_ref, zlast_ref = refs[N_GLA_IN + N_RWKV_IN:]

    @pl.when(pl.program_id(0) == 0)
    def _():
        ht_ref[...] = jnp.zeros_like(ht_ref)
        h_ref[...] = jnp.zeros_like(h_ref)
        zlast_ref[...] = jnp.zeros_like(zlast_ref)

    gla = _gla_stages(*gla_in, ob_ref, ht_ref, n_batch)
    rwkv = _rwkv_stages(*rwkv_in, oc_ref, h_ref, zlast_ref, n_batch)

    def chunk(c, carry):
        live = [rwkv(c), gla(c)]
        while live:
            live = [g for g in live if next(g, True) is None]
        return carry

    lax.fori_loop(0, n_chunks // REC_CPI, chunk, 0)


def _recur(zb, zc, gp, rp, l, B, S, ts):
    ns = S // ts
    W, Q = RWKV_W, RWKV_QUAD
    blk = lambda n, j: pl.BlockSpec((B, ts, n), lambda s: (0, s, j))
    split = lambda c: pl.BlockSpec((None, 2, 128, c), lambda s: (l, 0, 0, 0))
    zb = zb.reshape(B, S, zb.shape[1])
    o_b, o_c = pl.pallas_call(
        functools.partial(_recur_body, n_chunks=ts // CHUNK, n_batch=B),
        grid=(ns,),
        in_specs=[
            blk(GLA_W, 0),
            blk(GLA_W, 1),
            blk(GLA_QK, 4),
            blk(GLA_QK, 5),
            blk(128, 12),
            split(GLA_QK), _vec_spec(GLA_QK, l), _vec_spec(GLA_W, l),
            pl.BlockSpec((GLA_LEVELS + 1, CHUNK, GLA_HEADS * CHUNK), lambda s: (0, 0, 0)),
            blk(RWKV_COLS_PAD, 0),
            _vec_spec(RWKV_COLS_PAD, l), _vec_spec(W, l), split(W),
            _vec_spec(W, l), split(W), _mat_spec(256, W, l),
            _vec_spec(W, l), _vec_spec(W, l), _vec_spec(W, l), _vec_spec(W, l), _vec_spec(W, l),
        ],
        out_specs=[blk(GLA_W, 0), blk(W, 0)],
        out_shape=[jax.ShapeDtypeStruct((B, S, GLA_W), BF16), jax.ShapeDtypeStruct((B, S, W), BF16)],
        scratch_shapes=[pltpu.VMEM((B, GLA_W, GLA_QK), F32),
                        pltpu.VMEM((B, RWKV_NQ, Q, Q), F32),
                        pltpu.VMEM((B, 8, RWKV_COLS_PAD), F32)],
        compiler_params=_cparams(1),
        name="recurrences",
    )(zb, zb, zb, zb, zb, gp["walpha"], gp["balpha"], gp["gn"], _gla_pair_masks(),
      zc.reshape(B, S, RWKV_COLS_PAD), rp["mu"], rp["w0"], rp["wdec"], rp["a0"], rp["wiclr"],
      rp["wgate"], rp["k_k"], rp["k_a"], rp["r_k"], rp["ln_w"], rp["ln_b"])
    return o_b.reshape(B * S, GLA_W), o_c.reshape(B * S, W)


def _merge_body(x_ref, oa_ref, ob_ref, oc_ref, gpre_ref, wd_ref, wa_ref, wb_ref, wc_ref, wo_ref,
                gpost_ref, o_ref):
    x = x_ref[...]
    D = D_MODEL
    h = _rms(x, gpre_ref[...]).astype(BF16)
    branches = ((oa_ref[...], wa_ref), (ob_ref[...], wb_ref), (oc_ref[...], wc_ref))
    y = None
    for c in range(0, D, MERGE_COLS):
        cols = slice(c, c + MERGE_COLS)
        merged = None
        for j, (o, w_r) in enumerate(branches):
            gate = _sigmoid(_dot(h, wd_ref[:, j * D + c:j * D + c + MERGE_COLS]))
            term = gate * _dot(o, w_r[:, cols])
            merged = term if merged is None else merged + term
        part = _dot(merged.astype(BF16), wo_ref[cols, :])
        y = part if y is None else y + part
    o_ref[...] = x + _rms(y, gpost_ref[...])


def _merge(x, oa, ob, oc, wd, wa, wb, wc, wo, ng, l, tm):
    T, D = x.shape
    tok = lambda n: pl.BlockSpec((tm, n), lambda i: (i, 0))
    res = lambda r, c: pl.BlockSpec((None, r, c), lambda i: (l, 0, 0), pipeline_mode=pl.Buffered(1))
    gain = lambda j: pl.BlockSpec((None, 1, D), lambda i: (l * 8 + j, 0, 0))
    return pl.pallas_call(
        _merge_body,
        grid=(T // tm,),
        in_specs=[tok(D), tok(MLA_W), tok(GLA_W), tok(RWKV_W), gain(2), res(D, N_BRANCH * D),
                  res(MLA_W, D), res(GLA_W, D), res(RWKV_W, D), res(D, D), gain(3)],
        out_specs=tok(D),
        out_shape=jax.ShapeDtypeStruct((T, D), F32),
        compiler_params=_cparams(1),
        name="merge",
    )(x, oa, ob, oc, ng, wd, wa, wb, wc, wo, ng)


def _mem_kv_body(mem_ref, g_ref, w_ref, o_ref):
    o_ref[...] = _dot(_rms(mem_ref[...], g_ref[...]).astype(BF16), w_ref[...]).astype(o_ref.dtype)


def _mem_kv(mem, mem_norm, wkv, l):
    B, M, D = mem.shape
    return pl.pallas_call(
        _mem_kv_body,
        grid=(B,),
        in_specs=[pl.BlockSpec((None, M, D), lambda b: (b, 0, 0)), _vec_spec(D, l),
                  _mat_spec(D, 2 * D, l)],
        out_specs=pl.BlockSpec((None, M, 2 * D), lambda b: (b, 0, 0)),
        out_shape=jax.ShapeDtypeStruct((B, M, 2 * D), BF16),
        compiler_params=_cparams(1),
        name="mem_kv",
    )(mem, mem_norm, wkv)


def _mem_attn_body(x_ref, gpre_ref, wq_ref, kv_ref, wo_ref, gpost_ref, o_ref):
    x = x_ref[...]
    D = D_MODEL
    h = _rms(x, gpre_ref[...]).astype(BF16)
    q = (_dot(h, wq_ref[...]) * (MEM_HD ** -0.5 * LOG2_E)).astype(BF16)
    kv = kv_ref[...]
    outs = []
    for hh in range(MEM_HEADS):
        sl = slice(hh * MEM_HD, (hh + 1) * MEM_HD)
        s = _dot_nt(q[:, sl], kv[:, sl])
        p = jnp.exp2(s - jnp.max(s, axis=-1, keepdims=True))
        o = _dot(p.astype(BF16), kv[:, D + hh * MEM_HD:D + (hh + 1) * MEM_HD])
        outs.append(o / jnp.sum(p, axis=-1, keepdims=True))
    o = jnp.concatenate(outs, axis=1).astype(BF16)
    o_ref[...] = x + _rms(_dot(o, wo_ref[...]), gpost_ref[...])


def _mem_attn(x, ng, wq, kv, wo, l, S, tm):
    T, D = x.shape
    M = kv.shape[1]
    per_b = S // tm
    return pl.pallas_call(
        _mem_attn_body,
        grid=(T // tm,),
        in_specs=[pl.BlockSpec((tm, D), lambda i: (i, 0)),
                  pl.BlockSpec((None, 1, D), lambda i: (l * 8 + 4, 0, 0)),
                  _mat_spec(D, D, l),
                  pl.BlockSpec((None, M, 2 * D), lambda i: (i // per_b, 0, 0)),
                  _mat_spec(D, D, l),
                  pl.BlockSpec((None, 1, D), lambda i: (l * 8 + 5, 0, 0))],
        out_specs=pl.BlockSpec((tm, D), lambda i: (i, 0)),
        out_shape=jax.ShapeDtypeStruct((T, D), F32),
        compiler_params=_cparams(1),
        name="mem_attn",
    )(x, ng, wq, kv, wo, ng)


def _prepare_params(w_in, mla_w_uq, mla_w_ukv, gla_w_alpha, gla_norm, rwkv_mu, rwkv_w_decay,
                    rwkv_w_iclr, rwkv_w_gate, w_branch):
    L, D = w_in.shape[0], w_in.shape[1]
    zc = lambda n: jnp.zeros((L, D, n), F32)
    o = 0
    cuts = {}
    for name, n in (("c_q", 256), ("c_kv", 128), ("k_rope", 32), ("gla_q", 256), ("gla_k", 256),
                    ("gla_v", 512), ("gla_g", 512), ("gla_a", 16), ("rwkv", 1824), ("gates", 3072)):
        cuts[name] = w_in[:, :, o:o + n]
        o += n
    kr = cuts["k_rope"]
    kr_b = jnp.concatenate([-kr[..., 16:], kr[..., :16]], axis=-1)
    wa = jnp.concatenate([cuts["c_q"], cuts["c_kv"], zc(64), kr, zc(32), zc(64), kr_b, zc(32)], -1)
    wb = jnp.concatenate([cuts["gla_v"], cuts["gla_g"], cuts["gla_q"], cuts["gla_k"],
                          cuts["gla_a"], zc(112)], -1)
    wc = jnp.concatenate([cuts["rwkv"], zc(RWKV_MISC - 288)], -1)
    wd = cuts["gates"]

    wuq = mla_w_uq.reshape(L, MLA_Q_RANK, MLA_HEADS, MLA_NOPE + MLA_ROPE)
    nope, rope = wuq[..., :MLA_NOPE], wuq[..., MLA_NOPE:]
    zq = lambda n: jnp.zeros((L, MLA_Q_RANK, MLA_HEADS, n), F32)
    wqa = jnp.concatenate([nope, rope, zq(32)], -1).reshape(L, MLA_Q_RANK, MLA_W)
    rope_b = jnp.concatenate([-rope[..., 16:], rope[..., :16]], -1)
    wqb = jnp.concatenate([zq(64), rope_b, zq(32)], -1).reshape(L, MLA_Q_RANK, MLA_W)
    wukv = mla_w_ukv.reshape(L, MLA_KV_RANK, MLA_HEADS, 128)
    zk = jnp.zeros((L, MLA_KV_RANK, MLA_HEADS, 64), F32)
    wk = jnp.concatenate([wukv[..., :64], zk], -1).reshape(L, MLA_KV_RANK, MLA_W)
    wv = jnp.concatenate([wukv[..., 64:], zk[..., :MLA_VSLOT - 64]], -1).reshape(
        L, MLA_KV_RANK, MLA_HEADS * MLA_VSLOT)

    bra = w_branch[:, :512].reshape(L, MLA_HEADS, 64, D)
    bra = jnp.concatenate([bra, jnp.zeros_like(bra)], axis=2).reshape(L, MLA_W, D)
    brb = w_branch[:, 512:512 + GLA_W]
    brc = w_branch[:, 512 + GLA_W:]

    walpha = jnp.concatenate(
        [gla_w_alpha, jnp.zeros((L, 128 - GLA_GATE_RANK, GLA_QK), F32)], axis=1)
    gn = jnp.tile(gla_norm, (1, GLA_HEADS))[:, None, :]
    mu = jnp.concatenate([rwkv_mu, jnp.zeros((L, RWKV_MISC - 288), F32)], -1)[:, None, :]
    zr = lambda n: jnp.zeros((L, n, RWKV_W), F32)
    def hi_lo(w):
        hi = w.astype(BF16)
        return jnp.stack([hi, (w - hi.astype(F32)).astype(BF16)], axis=1)

    wdec = hi_lo(jnp.concatenate([rwkv_w_decay, zr(64)], axis=1))
    wiclr = hi_lo(jnp.concatenate([zr(64), rwkv_w_iclr], axis=1))
    wgate = jnp.concatenate([rwkv_w_gate, zr(256 - RWKV_GATE_RANK)], axis=1)
    bf = lambda w: w.astype(BF16)
    return dict(wa=bf(wa), wb=bf(wb), wc=bf(wc), wd=bf(wd), wqa=bf(wqa), wqb=bf(wqb), wk=bf(wk),
                wv=bf(wv), bra=bf(bra), brb=bf(brb), brc=bf(brc), walpha=hi_lo(walpha), gn=gn, mu=mu,
                wdec=wdec, wiclr=wiclr, wgate=bf(wgate))


def kernel(x, mem, positions, norm_g, w_ffn_in, w_ffn_out, w_in, mla_q_norm, mla_w_uq, mla_kv_norm, mla_w_ukv, gla_w_alpha, gla_b_alpha, gla_norm, rwkv_mu, rwkv_w0, rwkv_w_decay, rwkv_a0, rwkv_w_iclr, rwkv_w_gate, rwkv_k_k, rwkv_k_a, rwkv_r_k, rwkv_ln_w, rwkv_ln_b, w_branch, w_out, mem_norm, mem_wq, mem_wkv, mem_wo):
    B, S, D = x.shape
    L = norm_g.shape[0]
    T = B * S
    tm = min(512, S)
    tq = min(1024, S)
    tk = tq // 2
    ts = min(512, S)
    tff = min(1024, S)
    nsp = 11

    pp = _prepare_params(w_in, mla_w_uq, mla_w_ukv, gla_w_alpha, gla_norm, rwkv_mu, rwkv_w_decay,
                         rwkv_w_iclr, rwkv_w_gate, w_branch)
    bf = lambda w: w.astype(BF16)
    ffn_in, ffn_out = bf(w_ffn_in), bf(w_ffn_out)
    wout, wq, wkv, wo = bf(w_out), bf(mem_wq), bf(mem_wkv), bf(mem_wo)
    ng = norm_g.reshape(L * 8, 1, D)
    row = lambda p: p[:, None, :]
    gl = dict(walpha=pp["walpha"], balpha=row(gla_b_alpha), gn=pp["gn"])
    rw = dict(mu=pp["mu"], w0=row(rwkv_w0), wdec=pp["wdec"], a0=row(rwkv_a0), wiclr=pp["wiclr"],
              wgate=pp["wgate"], k_k=row(rwkv_k_k), k_a=row(rwkv_k_a), r_k=row(rwkv_r_k),
              ln_w=row(rwkv_ln_w), ln_b=row(rwkv_ln_b))

    inv_freq = ROPE_THETA ** (-jnp.arange(0, MLA_ROPE, 2, dtype=F32) / MLA_ROPE)
    invf_lane = jnp.concatenate(
        [jnp.zeros((MLA_NOPE,), F32), inv_freq, inv_freq, jnp.zeros((32,), F32)])[None, :]
    c0, s0 = _rope_tables(positions.astype(F32).reshape(T, 1), invf_lane, tm)

    x = x.reshape(T, D)
    for l in range(L):
        x = _ffn(x, ng, ffn_in, ffn_out, l, 0, tff, nsp)
        zb, zc, q, k, vt = _mixer_proj(x, ng, l * 8 + 2, pp, c0, s0, row(mla_q_norm),
                                       row(mla_kv_norm), l, tk)
        o_a = _mla_attn(q, k, vt, B, S, tq, tk)
        o_b, o_c = _recur(zb, zc, gl, rw, l, B, S, ts)
        x = _merge(x, o_a, o_b, o_c, pp["wd"], pp["bra"], pp["brb"], pp["brc"], wout, ng, l, tff)
        kv = _mem_kv(mem, row(mem_norm), wkv, l)
        x = _mem_attn(x, ng, wq, kv, wo, l, S, tff)
        x = _ffn(x, ng, ffn_in, ffn_out, l, 1, tff, nsp)
    return x.reshape(B, S, D)
```

```python
import functools

import jax
import jax.numpy as jnp
import numpy as np
from jax import lax
from jax.experimental import pallas as pl
from jax.experimental.pallas import tpu as pltpu

F32 = jnp.float32
BF16 = jnp.bfloat16

D_MODEL = 1024
D_FF = 2816
NORM_EPS = 1e-6
LOG2_E = 1.4426950408889634
MLA_HEADS = 8
MLA_NOPE = 64
MLA_ROPE = 32
MLA_Q_RANK = 256
MLA_KV_RANK = 128
ROPE_THETA = 10000.0
MLA_SLOT = 128
MLA_W = MLA_HEADS * MLA_SLOT
MLA_HPS = 4
MLA_VSLOT = 80
MLA_ONE = 64
GLA_HEADS = 4
GLA_DK = 64
GLA_DV = 128
GLA_GATE_RANK = 16
GLA_GATE_NORM = 16.0
GLA_QK = GLA_HEADS * GLA_DK
GLA_W = GLA_HEADS * GLA_DV
GLA_LEVELS = 6
GLA_NEG = -1e30
RWKV_HEADS = 8
RWKV_N = 64
RWKV_DECAY_RANK = 64
RWKV_ICLR_RANK = 64
RWKV_GATE_RANK = 160
RWKV_GN_EPS = 64e-5
RWKV_W = RWKV_HEADS * RWKV_N
RWKV_MISC = 384
RWKV_COLS_PAD = 3 * RWKV_W + RWKV_MISC
RWKV_QUAD = 4 * RWKV_N
RWKV_NQ = RWKV_W // RWKV_QUAD
RWKV_SOLVE_BLOCK = 16
REC_CPI = 2
MEM_HEADS = 4
MEM_HD = D_MODEL // MEM_HEADS
N_BRANCH = 3
MERGE_COLS = 256
CHUNK = 64

VMEM_LIMIT_BYTES = 56 * 1024 * 1024


def _cparams(n_axes):
    return pltpu.CompilerParams(
        dimension_semantics=("arbitrary",) * n_axes,
        vmem_limit_bytes=VMEM_LIMIT_BYTES,
    )


def _dot(a, b):
    return jnp.dot(a, b, preferred_element_type=F32)


def _dot_nt(a, b):
    return lax.dot_general(a, b, (((1,), (1,)), ((), ())), preferred_element_type=F32)


def _dot_tn(a, b):
    return lax.dot_general(a, b, (((0,), (0,)), ((), ())), preferred_element_type=F32)


def _rms(x, g, eps=NORM_EPS):
    return x * lax.rsqrt(jnp.mean(x * x, axis=-1, keepdims=True) + eps) * g


def _sigmoid(x):
    return 1.0 / (1.0 + jnp.exp(-x))


def _softplus(x):
    return jnp.maximum(x, 0.0) + jnp.log(1.0 + jnp.exp(-jnp.abs(x)))


def _split2(x):
    hi = x.astype(BF16)
    return hi, (x - hi.astype(F32)).astype(BF16)


def _dot_split(x, w_hi, w_lo):
    hi, lo = _split2(x)
    m = x.shape[0]
    top = _dot(jnp.concatenate([hi, lo], axis=0), w_hi)
    return top[:m] + top[m:] + _dot(hi, w_lo)


def _dot_01(m01_bf16, x):
    p1 = x.astype(BF16)
    r1 = x - p1.astype(F32)
    p2 = r1.astype(BF16)
    p3 = (r1 - p2.astype(F32)).astype(BF16)
    n = x.shape[1]
    out = _dot(m01_bf16, jnp.concatenate([p1, p2, p3], axis=1))
    return out[:, :n] + out[:, n:2 * n] + out[:, 2 * n:]


def _iota(shape, dim):
    return lax.broadcasted_iota(jnp.int32, shape, dim)


def _segsum(x, e_bf16):
    hi = x.astype(BF16)
    lo = (x - hi.astype(F32)).astype(BF16)
    return _dot(hi, e_bf16) + _dot(lo, e_bf16)


def _vec_spec(n, l):
    return pl.BlockSpec((None, 1, n), lambda *_: (l, 0, 0))


def _mat_spec(r, c, l):
    return pl.BlockSpec((None, r, c), lambda *_: (l, 0, 0))


def _ffn_body(x_ref, gpre_ref, wi_ref, wo_ref, gpost_ref, o_ref, *, n_split):
    x = x_ref[...]
    h = _rms(x, gpre_ref[...]).astype(BF16)
    ff = wo_ref.shape[0]
    tf = ff // n_split
    y = None
    for j in range(n_split):
        g = _dot(h, wi_ref[:, j * tf:(j + 1) * tf])
        u = _dot(h, wi_ref[:, ff + j * tf:ff + (j + 1) * tf])
        act = (g * _sigmoid(g) * u).astype(BF16)
        part = _dot(act, wo_ref[j * tf:(j + 1) * tf, :])
        y = part if y is None else y + part
    o_ref[...] = x + 0.5 * _rms(y, gpost_ref[...])


def _ffn(x, ng, w_in, w_out, l, k, tm, n_split):
    T, D = x.shape
    ff = w_out.shape[2]
    g_pre, g_post = l * 8 + 6 * k, l * 8 + 6 * k + 1
    return pl.pallas_call(
        functools.partial(_ffn_body, n_split=n_split),
        grid=(T // tm,),
        in_specs=[
            pl.BlockSpec((tm, D), lambda i: (i, 0)),
            pl.BlockSpec((None, 1, D), lambda i: (g_pre, 0, 0)),
            pl.BlockSpec((None, None, D, 2 * ff), lambda i: (l, k, 0, 0),
                         pipeline_mode=pl.Buffered(1)),
            pl.BlockSpec((None, None, ff, D), lambda i: (l, k, 0, 0), pipeline_mode=pl.Buffered(1)),
            pl.BlockSpec((None, 1, D), lambda i: (g_post, 0, 0)),
        ],
        out_specs=pl.BlockSpec((tm, D), lambda i: (i, 0)),
        out_shape=jax.ShapeDtypeStruct((T, D), F32),
        compiler_params=_cparams(1),
        name="ffn",
    )(x, ng, w_in, w_out, ng)


def _mixer_proj_body(x_ref, g_ref, wa_ref, wb_ref, wc_ref, c0_ref, s0_ref, qn_ref, kvn_ref,
                     wqa_ref, wqb_ref, wk_ref, wv_ref, one_ref,
                     zb_ref, zc_ref, q_ref, k_ref, vt_ref):
    h = _rms(x_ref[...], g_ref[...]).astype(BF16)
    zb_ref[...] = _dot(h, wb_ref[...])
    zc_ref[...] = _dot(h, wc_ref[...])
    _mla_prep(_dot(h, wa_ref[...]), c0_ref, s0_ref, qn_ref, kvn_ref, wqa_ref, wqb_ref, wk_ref,
              wv_ref, one_ref, q_ref, k_ref, vt_ref)


def _mixer_proj(x, ng, g_idx, pp, c0, s0, qn, kvn, l, tk):
    T, D = x.shape
    vw = MLA_HEADS * MLA_VSLOT
    tok = lambda n: pl.BlockSpec((tk, n), lambda i: (i, 0))
    res = lambda r, c: pl.BlockSpec((None, r, c), lambda i: (l, 0, 0), pipeline_mode=pl.Buffered(1))
    nb, nc = pp["wb"].shape[2], pp["wc"].shape[2]
    one_lane = jnp.asarray((np.arange(vw) % MLA_VSLOT == MLA_ONE).astype(np.float32))[None, :]
    return pl.pallas_call(
        _mixer_proj_body,
        grid=(T // tk,),
        in_specs=[tok(D), pl.BlockSpec((None, 1, D), lambda i: (g_idx, 0, 0)),
                  res(D, pp["wa"].shape[2]), res(D, nb), res(D, nc),
                  tok(MLA_SLOT), tok(MLA_SLOT), _vec_spec(MLA_Q_RANK, l), _vec_spec(MLA_KV_RANK, l),
                  _mat_spec(MLA_Q_RANK, MLA_W, l), _mat_spec(MLA_Q_RANK, MLA_W, l),
                  _mat_spec(MLA_KV_RANK, MLA_W, l), _mat_spec(MLA_KV_RANK, vw, l),
                  pl.BlockSpec((1, vw), lambda i: (0, 0))],
        out_specs=[tok(nb), tok(nc), tok(MLA_W), tok(MLA_W),
                   pl.BlockSpec((None, vw, tk), lambda i: (i, 0, 0))],
        out_shape=[jax.ShapeDtypeStruct((T, nb), F32), jax.ShapeDtypeStruct((T, nc), F32),
                   jax.ShapeDtypeStruct((T, MLA_W), BF16), jax.ShapeDtypeStruct((T, MLA_W), BF16),
                   jax.ShapeDtypeStruct((T // tk, vw, tk), BF16)],
        compiler_params=_cparams(1),
        name="mixer_proj",
    )(x, ng, pp["wa"], pp["wb"], pp["wc"], c0, s0, qn, kvn, pp["wqa"], pp["wqb"], pp["wk"],
      pp["wv"], one_lane)


def _rope_table_body(pos_ref, invf_ref, c_ref, s_ref):
    ang = pos_ref[...] * invf_ref[...]
    lane = _iota(ang.shape, 1)
    rot = (lane >= MLA_NOPE) & (lane < MLA_NOPE + MLA_ROPE)
    c_ref[...] = jnp.where(lane < MLA_NOPE, 1.0, jnp.where(rot, jnp.cos(ang), 0.0))
    s_ref[...] = jnp.where(rot, jnp.sin(ang), 0.0)


def _rope_tables(pos_f32, invf_lane, tm):
    T = pos_f32.shape[0]
    return pl.pallas_call(
        _rope_table_body,
        grid=(T // tm,),
        in_specs=[pl.BlockSpec((tm, 1), lambda i: (i, 0)),
                  pl.BlockSpec((1, MLA_SLOT), lambda i: (0, 0))],
        out_specs=[pl.BlockSpec((tm, MLA_SLOT), lambda i: (i, 0))] * 2,
        out_shape=[jax.ShapeDtypeStruct((T, MLA_SLOT), F32)] * 2,
        compiler_params=_cparams(1),
        name="rope_tables",
    )(pos_f32, invf_lane)


def _mla_prep(za, c0_ref, s0_ref, qn_ref, kvn_ref, wqa_ref, wqb_ref, wk_ref, wv_ref,
              one_ref, q_ref, k_ref, vt_ref):
    cq = za[:, :MLA_Q_RANK]
    ckv = za[:, MLA_Q_RANK:MLA_Q_RANK + MLA_KV_RANK]
    kra = za[:, 384:512]
    krb = za[:, 512:640]
    cqn = _rms(cq, qn_ref[...]).astype(BF16)
    ckvn = _rms(ckv, kvn_ref[...]).astype(BF16)
    c0 = c0_ref[...]
    s0 = s0_ref[...]
    c8 = jnp.concatenate([c0] * MLA_HEADS, axis=1)
    s8 = jnp.concatenate([s0] * MLA_HEADS, axis=1)
    scale = (MLA_NOPE + MLA_ROPE) ** -0.5 * LOG2_E
    q = (_dot(cqn, wqa_ref[...]) * c8 + _dot(cqn, wqb_ref[...]) * s8) * scale
    q_ref[...] = q.astype(BF16)
    krot = kra * c0 + krb * s0
    k = _dot(ckvn, wk_ref[...]) + jnp.concatenate([krot] * MLA_HEADS, axis=1)
    k_ref[...] = k.astype(BF16)
    v = _dot(ckvn, wv_ref[...]) + one_ref[...]
    vt_ref[...] = v.T.astype(BF16)


def _mla_attn_body(q_ref, k_ref, vt_ref, o_ref, s_ref, m_ref, acc_ref, *, tq, tk):
    qi = pl.program_id(2)
    m_ref[...] = jnp.full_like(m_ref, -1e30)
    acc_ref[...] = jnp.zeros_like(acc_ref)
    heads = [slice(hh * MLA_SLOT, (hh + 1) * MLA_SLOT) for hh in range(MLA_HPS)]

    def produce(slot, ki):
        r0 = pl.multiple_of(ki * tk, tk)
        for hh, sl in enumerate(heads):
            s_ref[slot, hh] = _dot_nt(k_ref[pl.ds(r0, tk), sl], q_ref[:, sl])

    def consume(slot, ki, diag):
        for hh, sl in enumerate(heads):
            st = s_ref[slot, hh]
            if diag is not None:
                key = _iota(st.shape, 0) + diag * tk
                st = jnp.where(key <= _iota(st.shape, 1), st, -1e30)
            m_old = m_ref[hh]
            m_new = jnp.maximum(m_old, jnp.max(st, axis=0, keepdims=True))
            p = jnp.exp2(st - m_new).astype(BF16)
            vt = vt_ref[ki, hh * MLA_VSLOT:(hh + 1) * MLA_VSLOT, :]
            acc_ref[hh] = jnp.exp2(m_old - m_new) * acc_ref[hh] + _dot(vt, p)
            m_ref[hh] = m_new

    def body(j, carry):
        produce(1, 2 * j + 1)
        consume(0, 2 * j, None)
        produce(0, 2 * j + 2)
        consume(1, 2 * j + 1, None)
        return carry

    produce(0, 0)
    lax.fori_loop(0, qi, body, 0)
    r1 = pl.multiple_of((2 * qi + 1) * tk, tk)
    late = slice(tk, tq)
    for hh, sl in enumerate(heads):
        s_ref[1, hh, :, 0:tk] = _dot_nt(k_ref[pl.ds(r1, tk), sl], q_ref[late, sl])
    consume(0, 2 * qi, 0)
    for hh, sl in enumerate(heads):
        st = s_ref[1, hh, :, 0:tk]
        st = jnp.where(_iota(st.shape, 0) <= _iota(st.shape, 1), st, -1e30)
        m_old = m_ref[hh, :, late]
        m_new = jnp.maximum(m_old, jnp.max(st, axis=0, keepdims=True))
        p = jnp.exp2(st - m_new).astype(BF16)
        vt = vt_ref[2 * qi + 1, hh * MLA_VSLOT:(hh + 1) * MLA_VSLOT, :]
        acc_ref[hh, :, late] = jnp.exp2(m_old - m_new) * acc_ref[hh, :, late] + _dot(vt, p)
        m_ref[hh, :, late] = m_new
    for hh in range(MLA_HPS):
        acc = acc_ref[hh]
        o_t = acc / acc[MLA_ONE:MLA_ONE + 1, :]
        o_t = jnp.concatenate([o_t, jnp.zeros((MLA_SLOT - MLA_VSLOT, tq), F32)], axis=0)
        o_ref[:, hh * MLA_SLOT:(hh + 1) * MLA_SLOT] = o_t.T.astype(o_ref.dtype)


def _mla_attn(q, k, vt, B, S, tq, tk):
    assert tq == 2 * tk
    T = q.shape[0]
    nq, nk = S // tq, S // tk
    w = MLA_HPS * MLA_SLOT
    return pl.pallas_call(
        functools.partial(_mla_attn_body, tq=tq, tk=tk),
        grid=(B, MLA_HEADS // MLA_HPS, nq),
        in_specs=[
            pl.BlockSpec((tq, w), lambda b, h, i: (b * nq + i, h)),
            pl.BlockSpec((S, w), lambda b, h, i: (b, h)),
            pl.BlockSpec((nk, MLA_HPS * MLA_VSLOT, tk), lambda b, h, i: (b, h, 0)),
        ],
        out_specs=pl.BlockSpec((tq, w), lambda b, h, i: (b * nq + i, h)),
        out_shape=jax.ShapeDtypeStruct((T, MLA_W), BF16),
        scratch_shapes=[pltpu.VMEM((2, MLA_HPS, tk, tq), F32),
                        pltpu.VMEM((MLA_HPS, 1, tq), F32),
                        pltpu.VMEM((MLA_HPS, MLA_VSLOT, tq), F32)],
        compiler_params=_cparams(3),
        name="mla_attn",
    )(q, k, vt)


def _gla_pair_masks():
    i = np.arange(CHUNK)[:, None]
    j = np.arange(GLA_HEADS * CHUNK)[None, :] % CHUNK
    same = [(i >> (6 - lv)) == (j >> (6 - lv)) for lv in range(GLA_LEVELS)]
    return jnp.asarray(np.stack(same + [i == j]).astype(np.float32))


def _gla_stages(v_ref, g_ref, q_ref, k_ref, a_ref, walpha_ref, balpha_ref, gn_ref, pair_ref,
                o_ref, ht_ref, n_batch):
    C = CHUNK
    walpha_hi, walpha_lo = walpha_ref[0], walpha_ref[1]
    balpha = balpha_ref[...]
    gn = gn_ref[...]
    tri = (_iota((C, C), 1) <= _iota((C, C), 0)).astype(BF16)
    st_r, st_c = _iota((GLA_HEADS * C, GLA_QK), 0), _iota((GLA_HEADS * C, GLA_QK), 1)
    head_mask = ((st_r >> 6) == (st_c >> 6)).astype(BF16)
    sv_r, sv_c = _iota((GLA_HEADS * C, GLA_W), 0), _iota((GLA_HEADS * C, GLA_W), 1)
    value_mask = ((sv_r >> 6) == (sv_c >> 7)).astype(BF16)
    ht_r, ht_c = _iota((GLA_W, GLA_QK), 0), _iota((GLA_W, GLA_QK), 1)
    state_mask = ((ht_r >> 7) == (ht_c >> 6)).astype(F32)
    tok = _iota((C, GLA_QK), 0)
    halves = [C >> (lv + 1) for lv in range(GLA_LEVELS)]

    def stack(x):
        return jnp.concatenate([x.astype(BF16)] * GLA_HEADS, axis=0) * head_mask

    def level_operands(q, k, cb):
        last = {1: cb}
        for s in halves[:0:-1]:
            f = last[s]
            last[2 * s] = jnp.where((tok & s) != 0, f, pltpu.roll(f, C - s, 0))
        ops = []
        for hs in halves:
            f = last[hs]
            right = (tok & hs) != 0
            d = cb - jnp.where(right, pltpu.roll(f, hs, 0), f)
            ql = q * jnp.exp(jnp.where(right, d, GLA_NEG))
            kl = k * jnp.exp(jnp.where(right, GLA_NEG, -d))
            ops.append((ql.astype(BF16), stack(kl)))
        ops.append((q.astype(BF16), stack(k)))
        return ops

    def stages(c):
        cbs = [(b, pl.ds(pl.multiple_of((c * REC_CPI + sub) * C, C), C))
               for sub in range(REC_CPI) for b in range(n_batch)]
        xs = [_dot_split(a_ref[b, rows, :], walpha_hi, walpha_lo) + balpha for b, rows in cbs]
        yield
        cum = [_dot_01(tri, -_softplus(-x) * (1.0 / GLA_GATE_NORM)) for x in xs]
        yield
        st = []
        for (b, rows), cb in zip(cbs, cum):
            b_last = cb[C - 1:C, :]
            q = q_ref[b, rows, :] * GLA_DK ** -0.5
            k = k_ref[b, rows, :]
            vb = v_ref[b, rows, :].astype(BF16)
            st.append(dict(
                v=vb,
                bd_v=jnp.concatenate([vb] * GLA_HEADS, axis=0) * value_mask,
                qe=(q * jnp.exp(cb)).astype(BF16),
                ke=(k * jnp.exp(b_last - cb)).astype(BF16),
                gam=jnp.exp(b_last),
                ops=level_operands(q, k, cb)))
        scs = [None] * len(cbs)
        for lv in range(GLA_LEVELS + 1):
            for i, s in enumerate(st):
                ql, kl = s["ops"][lv]
                part = _dot_nt(ql, kl) * pair_ref[lv]
                scs[i] = part if lv == 0 else scs[i] + part
            yield
        intra = [_dot(sc.astype(BF16), s["bd_v"]) for sc, s in zip(scs, st)]
        yield
        for (b, rows), s, o_intra in zip(cbs, st, intra):
            ht = ht_ref[b]
            o = o_intra + _dot_nt(s["qe"], ht.astype(BF16))
            ht_ref[b] = ht * s["gam"] + state_mask * _dot_tn(s["v"], s["ke"])
            outs = []
            for h in range(GLA_HEADS):
                oh = o[:, h * GLA_DV:(h + 1) * GLA_DV]
                outs.append(oh * lax.rsqrt(jnp.mean(oh * oh, axis=-1, keepdims=True) + NORM_EPS))
            on = jnp.concatenate(outs, axis=1) * gn
            g = g_ref[b, rows, :]
            o_ref[b, rows, :] = (on * (g * _sigmoid(g))).astype(o_ref.dtype)
            yield

    return stages


def _rwkv_stages(z_ref, mu_ref, w0_ref, wdec_ref, a0_ref, wiclr_ref, wgate_ref, kk_ref, ka_ref,
                 rk_ref, lnw_ref, lnb_ref, o_ref, h_ref, zlast_ref, n_batch):
    C, W, Q = CHUNK, RWKV_W, RWKV_QUAD
    mu = mu_ref[...]
    w0, a0 = w0_ref[...], a0_ref[...]
    wdec_hi, wdec_lo = wdec_ref[0], wdec_ref[1]
    wiclr_hi, wiclr_lo = wiclr_ref[0], wiclr_ref[1]
    wgate = wgate_ref[...]
    k_k, k_a, r_k = kk_ref[...], ka_ref[...], rk_ref[...]
    ln_w, ln_b = lnw_ref[...], lnb_ref[...]

    tri = (_iota((C, C), 1) <= _iota((C, C), 0)).astype(BF16)
    sq_r, sq_c = _iota((Q, Q), 0), _iota((Q, Q), 1)
    head_blk = ((sq_r >> 6) == (sq_c >> 6)).astype(F32)
    eye_q = (sq_r == sq_c).astype(F32)
    e_seg = head_blk.astype(BF16)
    wd_t, wd_s = _iota((C, Q), 0), _iota((C, Q), 1) & (C - 1)
    strict = (wd_s < wd_t).astype(F32)
    incl = (wd_s <= wd_t).astype(F32)
    eye_wide = (wd_s == wd_t).astype(F32)
    blk_shift = RWKV_SOLVE_BLOCK.bit_length() - 1
    same_blk = ((wd_s >> blk_shift) == (wd_t >> blk_shift)).astype(F32)
    off_blk = 1.0 - same_blk
    row0 = _iota((C, RWKV_COLS_PAD), 0) == 0

    def bd(x):
        return jnp.concatenate([x.astype(BF16)] * 4, axis=0) * e_seg

    def bd2(x):
        return jnp.concatenate([bd(x[:, :Q]), bd(x[:, Q:])], axis=1)

    def dot_bd_split(x, y):
        xh, xl = _split2(x)
        yh, yl = _split2(y)
        m = x.shape[0]
        top = _dot(jnp.concatenate([xh, xl], axis=0), bd(yh))
        return top[:m] + top[m:] + _dot(xh, bd(yl))

    def quad(x, i):
        return x[:, i * Q:(i + 1) * Q]

    def segsum(x):
        return jnp.concatenate([_segsum(quad(x, i), e_seg) for i in range(RWKV_NQ)], axis=1)

    def chunk_one(b, rows):
        z = z_ref[b, rows, :]
        zp = jnp.where(row0, zlast_ref[b, 0:1, :], pltpu.roll(z, 1, 0))
        zlast_ref[b, 0:1, :] = z[C - 1:C, :]
        z = z + mu * (zp - z)
        r = z[:, 0:W]
        k = z[:, W:2 * W]
        v = z[:, 2 * W:3 * W]
        m0 = z[:, 3 * W:3 * W + 128]
        m12 = z[:, 3 * W + 128:3 * W + 384]
        w_log = -_softplus(-(w0 + _dot_split(jnp.tanh(m0), wdec_hi, wdec_lo))) - 0.5
        lw = -jnp.exp(w_log)
        a = _sigmoid(a0 + _dot_split(m0, wiclr_hi, wiclr_lo))
        g = _dot(_sigmoid(m12).astype(BF16), wgate)
        kk = k * k_k
        kkn = kk * lax.rsqrt(jnp.maximum(segsum(kk * kk), 1e-24))
        k2 = k * (1.0 + (a - 1.0) * k_a)
        beta = kkn * a
        cs = _dot_01(tri, lw)
        c_last = cs[C - 1:C, :]
        dec_in = jnp.exp(-cs)
        dec_out = jnp.exp(c_last - cs)
        kt = kkn * jnp.exp(cs - lw)
        rt = r * jnp.exp(cs)
        bh = beta * dec_in
        kh = k2 * dec_in
        kbar = k2 * dec_out
        bbar = beta * dec_out
        gam = jnp.exp(c_last)
        chains = [dict(b=b, i=i, kt=quad(kt, i), rt=quad(rt, i), v=quad(v, i), bh=quad(bh, i),
                       kh=quad(kh, i), kbar=quad(kbar, i), bbar=quad(bbar, i), gam=quad(gam, i))
                  for i in range(RWKV_NQ)]
        return dict(r=r, k2=k2, v=v, g=g), chains

    def epilogue(b, rows, tok, y):
        mean = segsum(y) * (1.0 / RWKV_N)
        d = y - mean
        var = segsum(d * d) * (1.0 / RWKV_N)
        yn = d * lax.rsqrt(var + RWKV_GN_EPS) * ln_w + ln_b
        bonus = segsum(tok["r"] * tok["k2"] * r_k) * tok["v"]
        o_ref[b, rows, :] = ((yn + bonus) * tok["g"]).astype(o_ref.dtype)

    def stages(c):
        groups, chains = [], []
        for sub in range(REC_CPI):
            rows = pl.ds(pl.multiple_of((c * REC_CPI + sub) * C, C), C)
            for b in range(n_batch):
                tok, ch = chunk_one(b, rows)
                groups.append((b, rows, tok, ch))
                chains += ch
                yield
        for ch in chains:
            lhs = jnp.concatenate([ch["kt"], ch["rt"]], axis=0).astype(BF16)
            rhs = jnp.concatenate([bd(ch["bh"]), bd(ch["kh"])], axis=0)
            sc = _dot_nt(lhs, rhs)
            ch["a_kk"] = sc[:C, Q:] * strict
            ch["a_rb"] = sc[C:, :Q] * incl
            ch["a_rk"] = sc[C:, Q:] * incl
            a_kb = sc[:C, :Q] * strict
            ch["a_off"] = (a_kb * off_blk).astype(BF16)
            ch["n"] = -(a_kb * same_blk)
            ch["t"] = eye_wide + ch["n"]
        yield
        for ch in chains:
            ch["p"] = dot_bd_split(ch["n"], ch["n"])
            ch["bd_v"] = bd(ch["v"])
            ch["akk_v"] = _dot(ch["a_kk"].astype(BF16), ch["bd_v"])
        yield
        for _ in range(2):
            for ch in chains:
                tp = dot_bd_split(jnp.concatenate([ch["t"], ch["p"]], axis=0), ch["p"])
                ch["t"] = ch["t"] + tp[:C]
                ch["p"] = tp[C:]
            yield
        for ch in chains:
            ch["t"] = (ch["t"] + dot_bd_split(ch["t"], ch["p"])).astype(BF16)
        yield
        for ch in chains:
            ch["x"] = jnp.concatenate([ch["akk_v"], ch["kt"]], axis=1)
            ch["u"] = _dot(ch["t"], bd2(ch["x"]))
        yield
        for _ in range(C // RWKV_SOLVE_BLOCK - 1):
            for ch in chains:
                ch["r"] = ch["x"] - _dot(ch["a_off"], bd2(ch["u"]))
            yield
            for ch in chains:
                ch["u"] = _dot(ch["t"], bd2(ch["r"]))
            yield
        for ch in chains:
            ch["w1"], ch["w2"] = ch["u"][:, :Q], ch["u"][:, Q:]
        for ch in chains:
            w1, w2, bbar = ch["w1"], ch["w2"], ch["bbar"]
            ch["y0"] = _dot(jnp.concatenate([ch["a_rk"], ch["a_rb"]], axis=1).astype(BF16),
                            jnp.concatenate([ch["bd_v"], bd(-w1)], axis=0))
            ch["rp"] = (ch["rt"] - _dot(ch["a_rb"].astype(BF16), bd(w2))).astype(BF16)
            ch["p_bd"] = (eye_q * ch["gam"] - head_blk * _dot_tn(
                bbar.astype(BF16), w2.astype(BF16))).astype(BF16)
            ch["q_bd"] = head_blk * _dot_tn(
                jnp.concatenate([ch["kbar"], -bbar], axis=0).astype(BF16),
                jnp.concatenate([ch["v"], w1], axis=0).astype(BF16))
        yield
        for b, rows, tok, chs in groups:
            ys = []
            for ch in chs:
                hb = h_ref[b, ch["i"]].astype(BF16)
                ys.append(_dot(ch["rp"], hb) + ch["y0"])
                h_ref[b, ch["i"]] = _dot(ch["p_bd"], hb) + ch["q_bd"]
            epilogue(b, rows, tok, jnp.concatenate(ys, axis=1))
            yield

    return stages


N_GLA_IN, N_RWKV_IN = 9, 12


def _recur_body(*refs, n_chunks, n_batch):
    gla_in = refs[:N_GLA_IN]
    rwkv_in = refs[N_GLA_IN:N_GLA_IN + N_RWKV_IN]
    ob_ref, oc_ref, ht_ref, h_ref, zlast_ref = refs[N_GLA_IN + N_RWKV_IN:]

    @pl.when(pl.program_id(0) == 0)
    def _():
        ht_ref[...] = jnp.zeros_like(ht_ref)
        h_ref[...] = jnp.zeros_like(h_ref)
        zlast_ref[...] = jnp.zeros_like(zlast_ref)

    gla = _gla_stages(*gla_in, ob_ref, ht_ref, n_batch)
    rwkv = _rwkv_stages(*rwkv_in, oc_ref, h_ref, zlast_ref, n_batch)

    def chunk(c, carry):
        live = [rwkv(c), gla(c)]
        while live:
            live = [g for g in live if next(g, True) is None]
        return carry

    lax.fori_loop(0, n_chunks // REC_CPI, chunk, 0)


def _recur(zb, zc, gp, rp, l, B, S, ts):
    ns = S // ts
    W, Q = RWKV_W, RWKV_QUAD
    blk = lambda n, j: pl.BlockSpec((B, ts, n), lambda s: (0, s, j))
    split = lambda c: pl.BlockSpec((None, 2, 128, c), lambda s: (l, 0, 0, 0))
    zb = zb.reshape(B, S, zb.shape[1])
    o_b, o_c = pl.pallas_call(
        functools.partial(_recur_body, n_chunks=ts // CHUNK, n_batch=B),
        grid=(ns,),
        in_specs=[
            blk(GLA_W, 0),
            blk(GLA_W, 1),
            blk(GLA_QK, 4),
            blk(GLA_QK, 5),
            blk(128, 12),
            split(GLA_QK), _vec_spec(GLA_QK, l), _vec_spec(GLA_W, l),
            pl.BlockSpec((GLA_LEVELS + 1, CHUNK, GLA_HEADS * CHUNK), lambda s: (0, 0, 0)),
            blk(RWKV_COLS_PAD, 0),
            _vec_spec(RWKV_COLS_PAD, l), _vec_spec(W, l), split(W),
            _vec_spec(W, l), split(W), _mat_spec(256, W, l),
            _vec_spec(W, l), _vec_spec(W, l), _vec_spec(W, l), _vec_spec(W, l), _vec_spec(W, l),
        ],
        out_specs=[blk(GLA_W, 0), blk(W, 0)],
        out_shape=[jax.ShapeDtypeStruct((B, S, GLA_W), BF16), jax.ShapeDtypeStruct((B, S, W), BF16)],
        scratch_shapes=[pltpu.VMEM((B, GLA_W, GLA_QK), F32),
                        pltpu.VMEM((B, RWKV_NQ, Q, Q), F32),
                        pltpu.VMEM((B, 8, RWKV_COLS_PAD), F32)],
        compiler_params=_cparams(1),
        name="recurrences",
    )(zb, zb, zb, zb, zb, gp["walpha"], gp["balpha"], gp["gn"], _gla_pair_masks(),
      zc.reshape(B, S, RWKV_COLS_PAD), rp["mu"], rp["w0"], rp["wdec"], rp["a0"], rp["wiclr"],
      rp["wgate"], rp["k_k"], rp["k_a"], rp["r_k"], rp["ln_w"], rp["ln_b"])
    return o_b.reshape(B * S, GLA_W), o_c.reshape(B * S, W)


def _merge_body(x_ref, oa_ref, ob_ref, oc_ref, gpre_ref, wd_ref, wa_ref, wb_ref, wc_ref, wo_ref,
                gpost_ref, o_ref):
    x = x_ref[...]
    D = D_MODEL
    h = _rms(x, gpre_ref[...]).astype(BF16)
    branches = ((oa_ref[...], wa_ref), (ob_ref[...], wb_ref), (oc_ref[...], wc_ref))
    y = None
    for c in range(0, D, MERGE_COLS):
        cols = slice(c, c + MERGE_COLS)
        merged = None
        for j, (o, w_r) in enumerate(branches):
            gate = _sigmoid(_dot(h, wd_ref[:, j * D + c:j * D + c + MERGE_COLS]))
            term = gate * _dot(o, w_r[:, cols])
            merged = term if merged is None else merged + term
        part = _dot(merged.astype(BF16), wo_ref[cols, :])
        y = part if y is None else y + part
    o_ref[...] = x + _rms(y, gpost_ref[...])


def _merge(x, oa, ob, oc, wd, wa, wb, wc, wo, ng, l, tm):
    T, D = x.shape
    tok = lambda n: pl.BlockSpec((tm, n), lambda i: (i, 0))
    res = lambda r, c: pl.BlockSpec((None, r, c), lambda i: (l, 0, 0), pipeline_mode=pl.Buffered(1))
    gain = lambda j: pl.BlockSpec((None, 1, D), lambda i: (l * 8 + j, 0, 0))
    return pl.pallas_call(
        _merge_body,
        grid=(T // tm,),
        in_specs=[tok(D), tok(MLA_W), tok(GLA_W), tok(RWKV_W), gain(2), res(D, N_BRANCH * D),
                  res(MLA_W, D), res(GLA_W, D), res(RWKV_W, D), res(D, D), gain(3)],
        out_specs=tok(D),
        out_shape=jax.ShapeDtypeStruct((T, D), F32),
        compiler_params=_cparams(1),
        name="merge",
    )(x, oa, ob, oc, ng, wd, wa, wb, wc, wo, ng)


def _mem_kv_body(mem_ref, g_ref, w_ref, o_ref):
    o_ref[...] = _dot(_rms(mem_ref[...], g_ref[...]).astype(BF16), w_ref[...]).astype(o_ref.dtype)


def _mem_kv(mem, mem_norm, wkv, l):
    B, M, D = mem.shape
    return pl.pallas_call(
        _mem_kv_body,
        grid=(B,),
        in_specs=[pl.BlockSpec((None, M, D), lambda b: (b, 0, 0)), _vec_spec(D, l),
                  _mat_spec(D, 2 * D, l)],
        out_specs=pl.BlockSpec((None, M, 2 * D), lambda b: (b, 0, 0)),
        out_shape=jax.ShapeDtypeStruct((B, M, 2 * D), BF16),
        compiler_params=_cparams(1),
        name="mem_kv",
    )(mem, mem_norm, wkv)


def _mem_attn_body(x_ref, gpre_ref, wq_ref, kv_ref, wo_ref, gpost_ref, o_ref):
    x = x_ref[...]
    D = D_MODEL
    h = _rms(x, gpre_ref[...]).astype(BF16)
    q = (_dot(h, wq_ref[...]) * (MEM_HD ** -0.5 * LOG2_E)).astype(BF16)
    kv = kv_ref[...]
    outs = []
    for hh in range(MEM_HEADS):
        sl = slice(hh * MEM_HD, (hh + 1) * MEM_HD)
        s = _dot_nt(q[:, sl], kv[:, sl])
        p = jnp.exp2(s - jnp.max(s, axis=-1, keepdims=True))
        o = _dot(p.astype(BF16), kv[:, D + hh * MEM_HD:D + (hh + 1) * MEM_HD])
        outs.append(o / jnp.sum(p, axis=-1, keepdims=True))
    o = jnp.concatenate(outs, axis=1).astype(BF16)
    o_ref[...] = x + _rms(_dot(o, wo_ref[...]), gpost_ref[...])


def _mem_attn(x, ng, wq, kv, wo, l, S, tm):
    T, D = x.shape
    M = kv.shape[1]
    per_b = S // tm
    return pl.pallas_call(
        _mem_attn_body,
        grid=(T // tm,),
        in_specs=[pl.BlockSpec((tm, D), lambda i: (i, 0)),
                  pl.BlockSpec((None, 1, D), lambda i: (l * 8 + 4, 0, 0)),
                  _mat_spec(D, D, l),
                  pl.BlockSpec((None, M, 2 * D), lambda i: (i // per_b, 0, 0)),
                  _mat_spec(D, D, l),
                  pl.BlockSpec((None, 1, D), lambda i: (l * 8 + 5, 0, 0))],
        out_specs=pl.BlockSpec((tm, D), lambda i: (i, 0)),
        out_shape=jax.ShapeDtypeStruct((T, D), F32),
        compiler_params=_cparams(1),
        name="mem_attn",
    )(x, ng, wq, kv, wo, ng)


def _prepare_params(w_in, mla_w_uq, mla_w_ukv, gla_w_alpha, gla_norm, rwkv_mu, rwkv_w_decay,
                    rwkv_w_iclr, rwkv_w_gate, w_branch):
    L, D = w_in.shape[0], w_in.shape[1]
    zc = lambda n: jnp.zeros((L, D, n), F32)
    o = 0
    cuts = {}
    for name, n in (("c_q", 256), ("c_kv", 128), ("k_rope", 32), ("gla_q", 256), ("gla_k", 256),
                    ("gla_v", 512), ("gla_g", 512), ("gla_a", 16), ("rwkv", 1824), ("gates", 3072)):
        cuts[name] = w_in[:, :, o:o + n]
        o += n
    kr = cuts["k_rope"]
    kr_b = jnp.concatenate([-kr[..., 16:], kr[..., :16]], axis=-1)
    wa = jnp.concatenate([cuts["c_q"], cuts["c_kv"], zc(64), kr, zc(32), zc(64), kr_b, zc(32)], -1)
    wb = jnp.concatenate([cuts["gla_v"], cuts["gla_g"], cuts["gla_q"], cuts["gla_k"],
                          cuts["gla_a"], zc(112)], -1)
    wc = jnp.concatenate([cuts["rwkv"], zc(RWKV_MISC - 288)], -1)
    wd = cuts["gates"]

    wuq = mla_w_uq.reshape(L, MLA_Q_RANK, MLA_HEADS, MLA_NOPE + MLA_ROPE)
    nope, rope = wuq[..., :MLA_NOPE], wuq[..., MLA_NOPE:]
    zq = lambda n: jnp.zeros((L, MLA_Q_RANK, MLA_HEADS, n), F32)
    wqa = jnp.concatenate([nope, rope, zq(32)], -1).reshape(L, MLA_Q_RANK, MLA_W)
    rope_b = jnp.concatenate([-rope[..., 16:], rope[..., :16]], -1)
    wqb = jnp.concatenate([zq(64), rope_b, zq(32)], -1).reshape(L, MLA_Q_RANK, MLA_W)
    wukv = mla_w_ukv.reshape(L, MLA_KV_RANK, MLA_HEADS, 128)
    zk = jnp.zeros((L, MLA_KV_RANK, MLA_HEADS, 64), F32)
    wk = jnp.concatenate([wukv[..., :64], zk], -1).reshape(L, MLA_KV_RANK, MLA_W)
    wv = jnp.concatenate([wukv[..., 64:], zk[..., :MLA_VSLOT - 64]], -1).reshape(
        L, MLA_KV_RANK, MLA_HEADS * MLA_VSLOT)

    bra = w_branch[:, :512].reshape(L, MLA_HEADS, 64, D)
    bra = jnp.concatenate([bra, jnp.zeros_like(bra)], axis=2).reshape(L, MLA_W, D)
    brb = w_branch[:, 512:512 + GLA_W]
    brc = w_branch[:, 512 + GLA_W:]

    walpha = jnp.concatenate(
        [gla_w_alpha, jnp.zeros((L, 128 - GLA_GATE_RANK, GLA_QK), F32)], axis=1)
    gn = jnp.tile(gla_norm, (1, GLA_HEADS))[:, None, :]
    mu = jnp.concatenate([rwkv_mu, jnp.zeros((L, RWKV_MISC - 288), F32)], -1)[:, None, :]
    zr = lambda n: jnp.zeros((L, n, RWKV_W), F32)
    def hi_lo(w):
        hi = w.astype(BF16)
        return jnp.stack([hi, (w - hi.astype(F32)).astype(BF16)], axis=1)

    wdec = hi_lo(jnp.concatenate([rwkv_w_decay, zr(64)], axis=1))
    wiclr = hi_lo(jnp.concatenate([zr(64), rwkv_w_iclr], axis=1))
    wgate = jnp.concatenate([rwkv_w_gate, zr(256 - RWKV_GATE_RANK)], axis=1)
    bf = lambda w: w.astype(BF16)
    return dict(wa=bf(wa), wb=bf(wb), wc=bf(wc), wd=bf(wd), wqa=bf(wqa), wqb=bf(wqb), wk=bf(wk),
                wv=bf(wv), bra=bf(bra), brb=bf(brb), brc=bf(brc), walpha=hi_lo(walpha), gn=gn, mu=mu,
                wdec=wdec, wiclr=wiclr, wgate=bf(wgate))


def kernel(x, mem, positions, norm_g, w_ffn_in, w_ffn_out, w_in, mla_q_norm, mla_w_uq, mla_kv_norm, mla_w_ukv, gla_w_alpha, gla_b_alpha, gla_norm, rwkv_mu, rwkv_w0, rwkv_w_decay, rwkv_a0, rwkv_w_iclr, rwkv_w_gate, rwkv_k_k, rwkv_k_a, rwkv_r_k, rwkv_ln_w, rwkv_ln_b, w_branch, w_out, mem_norm, mem_wq, mem_wkv, mem_wo):
    B, S, D = x.shape
    L = norm_g.shape[0]
    T = B * S
    tm = min(512, S)
    tq = min(1024, S)
    tk = tq // 2
    ts = min(512, S)
    tff = min(1024, S)
    nsp = 11

    pp = _prepare_params(w_in, mla_w_uq, mla_w_ukv, gla_w_alpha, gla_norm, rwkv_mu, rwkv_w_decay,
                         rwkv_w_iclr, rwkv_w_gate, w_branch)
    bf = lambda w: w.astype(BF16)
    ffn_in, ffn_out = bf(w_ffn_in), bf(w_ffn_out)
    wout, wq, wkv, wo = bf(w_out), bf(mem_wq), bf(mem_wkv), bf(mem_wo)
    ng = norm_g.reshape(L * 8, 1, D)
    row = lambda p: p[:, None, :]
    gl = dict(walpha=pp["walpha"], balpha=row(gla_b_alpha), gn=pp["gn"])
    rw = dict(mu=pp["mu"], w0=row(rwkv_w0), wdec=pp["wdec"], a0=row(rwkv_a0), wiclr=pp["wiclr"],
              wgate=pp["wgate"], k_k=row(rwkv_k_k), k_a=row(rwkv_k_a), r_k=row(rwkv_r_k),
              ln_w=row(rwkv_ln_w), ln_b=row(rwkv_ln_b))

    inv_freq = ROPE_THETA ** (-jnp.arange(0, MLA_ROPE, 2, dtype=F32) / MLA_ROPE)
    invf_lane = jnp.concatenate(
        [jnp.zeros((MLA_NOPE,), F32), inv_freq, inv_freq, jnp.zeros((32,), F32)])[None, :]
    c0, s0 = _rope_tables(positions.astype(F32).reshape(T, 1), invf_lane, tm)

    x = x.reshape(T, D)
    for l in range(L):
        x = _ffn(x, ng, ffn_in, ffn_out, l, 0, tff, nsp)
        zb, zc, q, k, vt = _mixer_proj(x, ng, l * 8 + 2, pp, c0, s0, row(mla_q_norm),
                                       row(mla_kv_norm), l, tk)
        o_a = _mla_attn(q, k, vt, B, S, tq, tk)
        o_b, o_c = _recur(zb, zc, gl, rw, l, B, S, ts)
        x = _merge(x, o_a, o_b, o_c, pp["wd"], pp["bra"], pp["brb"], pp["brc"], wout, ng, l, tff)
        kv = _mem_kv(mem, row(mem_norm), wkv, l)
        x = _mem_attn(x, ng, wq, kv, wo, l, S, tff)
        x = _ffn(x, ng, ffn_in, ffn_out, l, 1, tff, nsp)
    return x.reshape(B, S, D)
```

```python
import functools

import jax
import jax.numpy as jnp
import numpy as np
from jax import lax
from jax.experimental import pallas as pl
from jax.experimental.pallas import tpu as pltpu

F32 = jnp.float32
BF16 = jnp.bfloat16

D_MODEL = 1024
D_FF = 2816
NORM_EPS = 1e-6
LOG2_E = 1.4426950408889634
MLA_HEADS = 8
MLA_NOPE = 64
MLA_ROPE = 32
MLA_Q_RANK = 256
MLA_KV_RANK = 128
ROPE_THETA = 10000.0
MLA_SLOT = 128
MLA_W = MLA_HEADS * MLA_SLOT
MLA_HPS = 4
MLA_VSLOT = 80
MLA_ONE = 64
GLA_HEADS = 4
GLA_DK = 64
GLA_DV = 128
GLA_GATE_RANK = 16
GLA_GATE_NORM = 16.0
GLA_QK = GLA_HEADS * GLA_DK
GLA_W = GLA_HEADS * GLA_DV
GLA_LEVELS = 6
GLA_NEG = -1e30
RWKV_HEADS = 8
RWKV_N = 64
RWKV_DECAY_RANK = 64
RWKV_ICLR_RANK = 64
RWKV_GATE_RANK = 160
RWKV_GN_EPS = 64e-5
RWKV_W = RWKV_HEADS * RWKV_N
RWKV_MISC = 384
RWKV_COLS_PAD = 3 * RWKV_W + RWKV_MISC
RWKV_QUAD = 4 * RWKV_N
RWKV_NQ = RWKV_W // RWKV_QUAD
RWKV_SOLVE_BLOCK = 16
REC_CPI = 2
MEM_HEADS = 4
MEM_HD = D_MODEL // MEM_HEADS
N_BRANCH = 3
MERGE_COLS = 256
CHUNK = 64

VMEM_LIMIT_BYTES = 56 * 1024 * 1024


def _cparams(n_axes):
    return pltpu.CompilerParams(
        dimension_semantics=("arbitrary",) * n_axes,
        vmem_limit_bytes=VMEM_LIMIT_BYTES,
    )


def _dot(a, b):
    return jnp.dot(a, b, preferred_element_type=F32)


def _dot_nt(a, b):
    return lax.dot_general(a, b, (((1,), (1,)), ((), ())), preferred_element_type=F32)


def _dot_tn(a, b):
    return lax.dot_general(a, b, (((0,), (0,)), ((), ())), preferred_element_type=F32)


def _rms(x, g, eps=NORM_EPS):
    return x * lax.rsqrt(jnp.mean(x * x, axis=-1, keepdims=True) + eps) * g


def _sigmoid(x):
    return 1.0 / (1.0 + jnp.exp(-x))


def _softplus(x):
    return jnp.maximum(x, 0.0) + jnp.log(1.0 + jnp.exp(-jnp.abs(x)))


def _split2(x):
    hi = x.astype(BF16)
    return hi, (x - hi.astype(F32)).astype(BF16)


def _dot_split(x, w_hi, w_lo):
    hi, lo = _split2(x)
    m = x.shape[0]
    top = _dot(jnp.concatenate([hi, lo], axis=0), w_hi)
    return top[:m] + top[m:] + _dot(hi, w_lo)


def _dot_01(m01_bf16, x):
    p1 = x.astype(BF16)
    r1 = x - p1.astype(F32)
    p2 = r1.astype(BF16)
    p3 = (r1 - p2.astype(F32)).astype(BF16)
    n = x.shape[1]
    out = _dot(m01_bf16, jnp.concatenate([p1, p2, p3], axis=1))
    return out[:, :n] + out[:, n:2 * n] + out[:, 2 * n:]


def _iota(shape, dim):
    return lax.broadcasted_iota(jnp.int32, shape, dim)


def _segsum(x, e_bf16):
    hi = x.astype(BF16)
    lo = (x - hi.astype(F32)).astype(BF16)
    return _dot(hi, e_bf16) + _dot(lo, e_bf16)


def _vec_spec(n, l):
    return pl.BlockSpec((None, 1, n), lambda *_: (l, 0, 0))


def _mat_spec(r, c, l):
    return pl.BlockSpec((None, r, c), lambda *_: (l, 0, 0))


def _ffn_body(x_ref, gpre_ref, wi_ref, wo_ref, gpost_ref, o_ref, *, n_split):
    x = x_ref[...]
    h = _rms(x, gpre_ref[...]).astype(BF16)
    ff = wo_ref.shape[0]
    tf = ff // n_split
    y = None
    for j in range(n_split):
        g = _dot(h, wi_ref[:, j * tf:(j + 1) * tf])
        u = _dot(h, wi_ref[:, ff + j * tf:ff + (j + 1) * tf])
        act = (g * _sigmoid(g) * u).astype(BF16)
        part = _dot(act, wo_ref[j * tf:(j + 1) * tf, :])
        y = part if y is None else y + part
    o_ref[...] = x + 0.5 * _rms(y, gpost_ref[...])


def _ffn(x, ng, w_in, w_out, l, k, tm, n_split):
    T, D = x.shape
    ff = w_out.shape[2]
    g_pre, g_post = l * 8 + 6 * k, l * 8 + 6 * k + 1
    return pl.pallas_call(
        functools.partial(_ffn_body, n_split=n_split),
        grid=(T // tm,),
        in_specs=[
            pl.BlockSpec((tm, D), lambda i: (i, 0)),
            pl.BlockSpec((None, 1, D), lambda i: (g_pre, 0, 0)),
            pl.BlockSpec((None, None, D, 2 * ff), lambda i: (l, k, 0, 0),
                         pipeline_mode=pl.Buffered(1)),
            pl.BlockSpec((None, None, ff, D), lambda i: (l, k, 0, 0), pipeline_mode=pl.Buffered(1)),
            pl.BlockSpec((None, 1, D), lambda i: (g_post, 0, 0)),
        ],
        out_specs=pl.BlockSpec((tm, D), lambda i: (i, 0)),
        out_shape=jax.ShapeDtypeStruct((T, D), F32),
        compiler_params=_cparams(1),
        name="ffn",
    )(x, ng, w_in, w_out, ng)


def _mixer_proj_body(x_ref, g_ref, wa_ref, wb_ref, wc_ref, c0_ref, s0_ref, qn_ref, kvn_ref,
                     wqa_ref, wqb_ref, wk_ref, wv_ref, one_ref,
                     zb_ref, zc_ref, q_ref, k_ref, vt_ref):
    h = _rms(x_ref[...], g_ref[...]).astype(BF16)
    zb_ref[...] = _dot(h, wb_ref[...])
    zc_ref[...] = _dot(h, wc_ref[...])
    _mla_prep(_dot(h, wa_ref[...]), c0_ref, s0_ref, qn_ref, kvn_ref, wqa_ref, wqb_ref, wk_ref,
              wv_ref, one_ref, q_ref, k_ref, vt_ref)


def _mixer_proj(x, ng, g_idx, pp, c0, s0, qn, kvn, l, tk):
    T, D = x.shape
    vw = MLA_HEADS * MLA_VSLOT
    tok = lambda n: pl.BlockSpec((tk, n), lambda i: (i, 0))
    res = lambda r, c: pl.BlockSpec((None, r, c), lambda i: (l, 0, 0), pipeline_mode=pl.Buffered(1))
    nb, nc = pp["wb"].shape[2], pp["wc"].shape[2]
    one_lane = jnp.asarray((np.arange(vw) % MLA_VSLOT == MLA_ONE).astype(np.float32))[None, :]
    return pl.pallas_call(
        _mixer_proj_body,
        grid=(T // tk,),
        in_specs=[tok(D), pl.BlockSpec((None, 1, D), lambda i: (g_idx, 0, 0)),
                  res(D, pp["wa"].shape[2]), res(D, nb), res(D, nc),
                  tok(MLA_SLOT), tok(MLA_SLOT), _vec_spec(MLA_Q_RANK, l), _vec_spec(MLA_KV_RANK, l),
                  _mat_spec(MLA_Q_RANK, MLA_W, l), _mat_spec(MLA_Q_RANK, MLA_W, l),
                  _mat_spec(MLA_KV_RANK, MLA_W, l), _mat_spec(MLA_KV_RANK, vw, l),
                  pl.BlockSpec((1, vw), lambda i: (0, 0))],
        out_specs=[tok(nb), tok(nc), tok(MLA_W), tok(MLA_W),
                   pl.BlockSpec((None, vw, tk), lambda i: (i, 0, 0))],
        out_shape=[jax.ShapeDtypeStruct((T, nb), F32), jax.ShapeDtypeStruct((T, nc), F32),
                   jax.ShapeDtypeStruct((T, MLA_W), BF16), jax.ShapeDtypeStruct((T, MLA_W), BF16),
                   jax.ShapeDtypeStruct((T // tk, vw, tk), BF16)],
        compiler_params=_cparams(1),
        name="mixer_proj",
    )(x, ng, pp["wa"], pp["wb"], pp["wc"], c0, s0, qn, kvn, pp["wqa"], pp["wqb"], pp["wk"],
      pp["wv"], one_lane)


def _rope_table_body(pos_ref, invf_ref, c_ref, s_ref):
    ang = pos_ref[...] * invf_ref[...]
    lane = _iota(ang.shape, 1)
    rot = (lane >= MLA_NOPE) & (lane < MLA_NOPE + MLA_ROPE)
    c_ref[...] = jnp.where(lane < MLA_NOPE, 1.0, jnp.where(rot, jnp.cos(ang), 0.0))
    s_ref[...] = jnp.where(rot, jnp.sin(ang), 0.0)


def _rope_tables(pos_f32, invf_lane, tm):
    T = pos_f32.shape[0]
    return pl.pallas_call(
        _rope_table_body,
        grid=(T // tm,),
        in_specs=[pl.BlockSpec((tm, 1), lambda i: (i, 0)),
                  pl.BlockSpec((1, MLA_SLOT), lambda i: (0, 0))],
        out_specs=[pl.BlockSpec((tm, MLA_SLOT), lambda i: (i, 0))] * 2,
        out_shape=[jax.ShapeDtypeStruct((T, MLA_SLOT), F32)] * 2,
        compiler_params=_cparams(1),
        name="rope_tables",
    )(pos_f32, invf_lane)


def _mla_prep(za, c0_ref, s0_ref, qn_ref, kvn_ref, wqa_ref, wqb_ref, wk_ref, wv_ref,
              one_ref, q_ref, k_ref, vt_ref):
    cq = za[:, :MLA_Q_RANK]
    ckv = za[:, MLA_Q_RANK:MLA_Q_RANK + MLA_KV_RANK]
    kra = za[:, 384:512]
    krb = za[:, 512:640]
    cqn = _rms(cq, qn_ref[...]).astype(BF16)
    ckvn = _rms(ckv, kvn_ref[...]).astype(BF16)
    c0 = c0_ref[...]
    s0 = s0_ref[...]
    c8 = jnp.concatenate([c0] * MLA_HEADS, axis=1)
    s8 = jnp.concatenate([s0] * MLA_HEADS, axis=1)
    scale = (MLA_NOPE + MLA_ROPE) ** -0.5 * LOG2_E
    q = (_dot(cqn, wqa_ref[...]) * c8 + _dot(cqn, wqb_ref[...]) * s8) * scale
    q_ref[...] = q.astype(BF16)
    krot = kra * c0 + krb * s0
    k = _dot(ckvn, wk_ref[...]) + jnp.concatenate([krot] * MLA_HEADS, axis=1)
    k_ref[...] = k.astype(BF16)
    v = _dot(ckvn, wv_ref[...]) + one_ref[...]
    vt_ref[...] = v.T.astype(BF16)


def _mla_attn_body(q_ref, k_ref, vt_ref, o_ref, s_ref, m_ref, acc_ref, *, tq, tk):
    qi = pl.program_id(2)
    m_ref[...] = jnp.full_like(m_ref, -1e30)
    acc_ref[...] = jnp.zeros_like(acc_ref)
    heads = [slice(hh * MLA_SLOT, (hh + 1) * MLA_SLOT) for hh in range(MLA_HPS)]

    def produce(slot, ki):
        r0 = pl.multiple_of(ki * tk, tk)
        for hh, sl in enumerate(heads):
            s_ref[slot, hh] = _dot_nt(k_ref[pl.ds(r0, tk), sl], q_ref[:, sl])

    def consume(slot, ki, diag):
        for hh, sl in enumerate(heads):
            st = s_ref[slot, hh]
            if diag is not None:
                key = _iota(st.shape, 0) + diag * tk
                st = jnp.where(key <= _iota(st.shape, 1), st, -1e30)
            m_old = m_ref[hh]
            m_new = jnp.maximum(m_old, jnp.max(st, axis=0, keepdims=True))
            p = jnp.exp2(st - m_new).astype(BF16)
            vt = vt_ref[ki, hh * MLA_VSLOT:(hh + 1) * MLA_VSLOT, :]
            acc_ref[hh] = jnp.exp2(m_old - m_new) * acc_ref[hh] + _dot(vt, p)
            m_ref[hh] = m_new

    def body(j, carry):
        produce(1, 2 * j + 1)
        consume(0, 2 * j, None)
        produce(0, 2 * j + 2)
        consume(1, 2 * j + 1, None)
        return carry

    produce(0, 0)
    lax.fori_loop(0, qi, body, 0)
    r1 = pl.multiple_of((2 * qi + 1) * tk, tk)
    late = slice(tk, tq)
    for hh, sl in enumerate(heads):
        s_ref[1, hh, :, 0:tk] = _dot_nt(k_ref[pl.ds(r1, tk), sl], q_ref[late, sl])
    consume(0, 2 * qi, 0)
    for hh, sl in enumerate(heads):
        st = s_ref[1, hh, :, 0:tk]
        st = jnp.where(_iota(st.shape, 0) <= _iota(st.shape, 1), st, -1e30)
        m_old = m_ref[hh, :, late]
        m_new = jnp.maximum(m_old, jnp.max(st, axis=0, keepdims=True))
        p = jnp.exp2(st - m_new).astype(BF16)
        vt = vt_ref[2 * qi + 1, hh * MLA_VSLOT:(hh + 1) * MLA_VSLOT, :]
        acc_ref[hh, :, late] = jnp.exp2(m_old - m_new) * acc_ref[hh, :, late] + _dot(vt, p)
        m_ref[hh, :, late] = m_new
    for hh in range(MLA_HPS):
        acc = acc_ref[hh]
        o_t = acc / acc[MLA_ONE:MLA_ONE + 1, :]
        o_t = jnp.concatenate([o_t, jnp.zeros((MLA_SLOT - MLA_VSLOT, tq), F32)], axis=0)
        o_ref[:, hh * MLA_SLOT:(hh + 1) * MLA_SLOT] = o_t.T.astype(o_ref.dtype)


def _mla_attn(q, k, vt, B, S, tq, tk):
    assert tq == 2 * tk
    T = q.shape[0]
    nq, nk = S // tq, S // tk
    w = MLA_HPS * MLA_SLOT
    return pl.pallas_call(
        functools.partial(_mla_attn_body, tq=tq, tk=tk),
        grid=(B, MLA_HEADS // MLA_HPS, nq),
        in_specs=[
            pl.BlockSpec((tq, w), lambda b, h, i: (b * nq + i, h)),
            pl.BlockSpec((S, w), lambda b, h, i: (b, h)),
            pl.BlockSpec((nk, MLA_HPS * MLA_VSLOT, tk), lambda b, h, i: (b, h, 0)),
        ],
        out_specs=pl.BlockSpec((tq, w), lambda b, h, i: (b * nq + i, h)),
        out_shape=jax.ShapeDtypeStruct((T, MLA_W), BF16),
        scratch_shapes=[pltpu.VMEM((2, MLA_HPS, tk, tq), F32),
                        pltpu.VMEM((MLA_HPS, 1, tq), F32),
                        pltpu.VMEM((MLA_HPS, MLA_VSLOT, tq), F32)],
        compiler_params=_cparams(3),
        name="mla_attn",
    )(q, k, vt)


def _gla_pair_masks():
    i = np.arange(CHUNK)[:, None]
    j = np.arange(GLA_HEADS * CHUNK)[None, :] % CHUNK
    same = [(i >> (6 - lv)) == (j >> (6 - lv)) for lv in range(GLA_LEVELS)]
    return jnp.asarray(np.stack(same + [i == j]).astype(np.float32))


def _gla_stages(v_ref, g_ref, q_ref, k_ref, a_ref, walpha_ref, balpha_ref, gn_ref, pair_ref,
                o_ref, ht_ref, n_batch):
    C = CHUNK
    walpha_hi, walpha_lo = walpha_ref[0], walpha_ref[1]
    balpha = balpha_ref[...]
    gn = gn_ref[...]
    tri = (_iota((C, C), 1) <= _iota((C, C), 0)).astype(BF16)
    st_r, st_c = _iota((GLA_HEADS * C, GLA_QK), 0), _iota((GLA_HEADS * C, GLA_QK), 1)
    head_mask = ((st_r >> 6) == (st_c >> 6)).astype(BF16)
    sv_r, sv_c = _iota((GLA_HEADS * C, GLA_W), 0), _iota((GLA_HEADS * C, GLA_W), 1)
    value_mask = ((sv_r >> 6) == (sv_c >> 7)).astype(BF16)
    ht_r, ht_c = _iota((GLA_W, GLA_QK), 0), _iota((GLA_W, GLA_QK), 1)
    state_mask = ((ht_r >> 7) == (ht_c >> 6)).astype(F32)
    tok = _iota((C, GLA_QK), 0)
    halves = [C >> (lv + 1) for lv in range(GLA_LEVELS)]

    def stack(x):
        return jnp.concatenate([x.astype(BF16)] * GLA_HEADS, axis=0) * head_mask

    def level_operands(q, k, cb):
        last = {1: cb}
        for s in halves[:0:-1]:
            f = last[s]
            last[2 * s] = jnp.where((tok & s) != 0, f, pltpu.roll(f, C - s, 0))
        ops = []
        for hs in halves:
            f = last[hs]
            right = (tok & hs) != 0
            d = cb - jnp.where(right, pltpu.roll(f, hs, 0), f)
            ql = q * jnp.exp(jnp.where(right, d, GLA_NEG))
            kl = k * jnp.exp(jnp.where(right, GLA_NEG, -d))
            ops.append((ql.astype(BF16), stack(kl)))
        ops.append((q.astype(BF16), stack(k)))
        return ops

    def stages(c):
        cbs = [(b, pl.ds(pl.multiple_of((c * REC_CPI + sub) * C, C), C))
               for sub in range(REC_CPI) for b in range(n_batch)]
        xs = [_dot_split(a_ref[b, rows, :], walpha_hi, walpha_lo) + balpha for b, rows in cbs]
        yield
        cum = [_dot_01(tri, -_softplus(-x) * (1.0 / GLA_GATE_NORM)) for x in xs]
        yield
        st = []
        for (b, rows), cb in zip(cbs, cum):
            b_last = cb[C - 1:C, :]
            q = q_ref[b, rows, :] * GLA_DK ** -0.5
            k = k_ref[b, rows, :]
            vb = v_ref[b, rows, :].astype(BF16)
            st.append(dict(
                v=vb,
                bd_v=jnp.concatenate([vb] * GLA_HEADS, axis=0) * value_mask,
                qe=(q * jnp.exp(cb)).astype(BF16),
                ke=(k * jnp.exp(b_last - cb)).astype(BF16),
                gam=jnp.exp(b_last),
                ops=level_operands(q, k, cb)))
        scs = [None] * len(cbs)
        for lv in range(GLA_LEVELS + 1):
            for i, s in enumerate(st):
                ql, kl = s["ops"][lv]
                part = _dot_nt(ql, kl) * pair_ref[lv]
                scs[i] = part if lv == 0 else scs[i] + part
            yield
        intra = [_dot(sc.astype(BF16), s["bd_v"]) for sc, s in zip(scs, st)]
        yield
        for (b, rows), s, o_intra in zip(cbs, st, intra):
            ht = ht_ref[b]
            o = o_intra + _dot_nt(s["qe"], ht.astype(BF16))
            ht_ref[b] = ht * s["gam"] + state_mask * _dot_tn(s["v"], s["ke"])
            outs = []
            for h in range(GLA_HEADS):
                oh = o[:, h * GLA_DV:(h + 1) * GLA_DV]
                outs.append(oh * lax.rsqrt(jnp.mean(oh * oh, axis=-1, keepdims=True) + NORM_EPS))
            on = jnp.concatenate(outs, axis=1) * gn
            g = g_ref[b, rows, :]
            o_ref[b, rows, :] = (on * (g * _sigmoid(g))).astype(o_ref.dtype)
            yield

    return stages


def _rwkv_stages(z_ref, mu_ref, w0_ref, wdec_ref, a0_ref, wiclr_ref, wgate_ref, kk_ref, ka_ref,
                 rk_ref, lnw_ref, lnb_ref, o_ref, h_ref, zlast_ref, n_batch):
    C, W, Q = CHUNK, RWKV_W, RWKV_QUAD
    mu = mu_ref[...]
    w0, a0 = w0_ref[...], a0_ref[...]
    wdec_hi, wdec_lo = wdec_ref[0], wdec_ref[1]
    wiclr_hi, wiclr_lo = wiclr_ref[0], wiclr_ref[1]
    wgate = wgate_ref[...]
    k_k, k_a, r_k = kk_ref[...], ka_ref[...], rk_ref[...]
    ln_w, ln_b = lnw_ref[...], lnb_ref[...]

    tri = (_iota((C, C), 1) <= _iota((C, C), 0)).astype(BF16)
    sq_r, sq_c = _iota((Q, Q), 0), _iota((Q, Q), 1)
    head_blk = ((sq_r >> 6) == (sq_c >> 6)).astype(F32)
    eye_q = (sq_r == sq_c).astype(F32)
    e_seg = head_blk.astype(BF16)
    wd_t, wd_s = _iota((C, Q), 0), _iota((C, Q), 1) & (C - 1)
    strict = (wd_s < wd_t).astype(F32)
    incl = (wd_s <= wd_t).astype(F32)
    eye_wide = (wd_s == wd_t).astype(F32)
    blk_shift = RWKV_SOLVE_BLOCK.bit_length() - 1
    same_blk = ((wd_s >> blk_shift) == (wd_t >> blk_shift)).astype(F32)
    off_blk = 1.0 - same_blk
    row0 = _iota((C, RWKV_COLS_PAD), 0) == 0

    def bd(x):
        return jnp.concatenate([x.astype(BF16)] * 4, axis=0) * e_seg

    def bd2(x):
        return jnp.concatenate([bd(x[:, :Q]), bd(x[:, Q:])], axis=1)

    def dot_bd_split(x, y):
        xh, xl = _split2(x)
        yh, yl = _split2(y)
        m = x.shape[0]
        top = _dot(jnp.concatenate([xh, xl], axis=0), bd(yh))
        return top[:m] + top[m:] + _dot(xh, bd(yl))

    def quad(x, i):
        return x[:, i * Q:(i + 1) * Q]

    def segsum(x):
        return jnp.concatenate([_segsum(quad(x, i), e_seg) for i in range(RWKV_NQ)], axis=1)

    def chunk_one(b, rows):
        z = z_ref[b, rows, :]
        zp = jnp.where(row0, zlast_ref[b, 0:1, :], pltpu.roll(z, 1, 0))
        zlast_ref[b, 0:1, :] = z[C - 1:C, :]
        z = z + mu * (zp - z)
        r = z[:, 0:W]
        k = z[:, W:2 * W]
        v = z[:, 2 * W:3 * W]
        m0 = z[:, 3 * W:3 * W + 128]
        m12 = z[:, 3 * W + 128:3 * W + 384]
        w_log = -_softplus(-(w0 + _dot_split(jnp.tanh(m0), wdec_hi, wdec_lo))) - 0.5
        lw = -jnp.exp(w_log)
        a = _sigmoid(a0 + _dot_split(m0, wiclr_hi, wiclr_lo))
        g = _dot(_sigmoid(m12).astype(BF16), wgate)
        kk = k * k_k
        kkn = kk * lax.rsqrt(jnp.maximum(segsum(kk * kk), 1e-24))
        k2 = k * (1.0 + (a - 1.0) * k_a)
        beta = kkn * a
        cs = _dot_01(tri, lw)
        c_last = cs[C - 1:C, :]
        dec_in = jnp.exp(-cs)
        dec_out = jnp.exp(c_last - cs)
        kt = kkn * jnp.exp(cs - lw)
        rt = r * jnp.exp(cs)
        bh = beta * dec_in
        kh = k2 * dec_in
        kbar = k2 * dec_out
        bbar = beta * dec_out
        gam = jnp.exp(c_last)
        chains = [dict(b=b, i=i, kt=quad(kt, i), rt=quad(rt, i), v=quad(v, i), bh=quad(bh, i),
                       kh=quad(kh, i), kbar=quad(kbar, i), bbar=quad(bbar, i), gam=quad(gam, i))
                  for i in range(RWKV_NQ)]
        return dict(r=r, k2=k2, v=v, g=g), chains

    def epilogue(b, rows, tok, y):
        mean = segsum(y) * (1.0 / RWKV_N)
        d = y - mean
        var = segsum(d * d) * (1.0 / RWKV_N)
        yn = d * lax.rsqrt(var + RWKV_GN_EPS) * ln_w + ln_b
        bonus = segsum(tok["r"] * tok["k2"] * r_k) * tok["v"]
        o_ref[b, rows, :] = ((yn + bonus) * tok["g"]).astype(o_ref.dtype)

    def stages(c):
        groups, chains = [], []
        for sub in range(REC_CPI):
            rows = pl.ds(pl.multiple_of((c * REC_CPI + sub) * C, C), C)
            for b in range(n_batch):
                tok, ch = chunk_one(b, rows)
                groups.append((b, rows, tok, ch))
                chains += ch
                yield
        for ch in chains:
            lhs = jnp.concatenate([ch["kt"], ch["rt"]], axis=0).astype(BF16)
            rhs = jnp.concatenate([bd(ch["bh"]), bd(ch["kh"])], axis=0)
            sc = _dot_nt(lhs, rhs)
            ch["a_kk"] = sc[:C, Q:] * strict
            ch["a_rb"] = sc[C:, :Q] * incl
            ch["a_rk"] = sc[C:, Q:] * incl
            a_kb = sc[:C, :Q] * strict
            ch["a_off"] = (a_kb * off_blk).astype(BF16)
            ch["n"] = -(a_kb * same_blk)
            ch["t"] = eye_wide + ch["n"]
        yield
        for ch in chains:
            ch["p"] = dot_bd_split(ch["n"], ch["n"])
            ch["bd_v"] = bd(ch["v"])
            ch["akk_v"] = _dot(ch["a_kk"].astype(BF16), ch["bd_v"])
        yield
        for _ in range(2):
            for ch in chains:
                tp = dot_bd_split(jnp.concatenate([ch["t"], ch["p"]], axis=0), ch["p"])
                ch["t"] = ch["t"] + tp[:C]
                ch["p"] = tp[C:]
            yield
        for ch in chains:
            ch["t"] = (ch["t"] + dot_bd_split(ch["t"], ch["p"])).astype(BF16)
        yield
        for ch in chains:
            x = jnp.concatenate([ch["akk_v"], ch["kt"]], axis=1)
            ch["u0"] = _dot(ch["t"], bd2(x))
            ch["m"] = _dot(ch["t"], bd(ch["a_off"])).astype(BF16)
            ch["u"] = ch["u0"]
        yield
        for _ in range(C // RWKV_SOLVE_BLOCK - 1):
            for ch in chains:
                ch["u"] = ch["u0"] - _dot(ch["m"], bd2(ch["u"]))
            yield
        for ch in chains:
            ch["w1"], ch["w2"] = ch["u"][:, :Q], ch["u"][:, Q:]
        for ch in chains:
            w1, w2, bbar = ch["w1"], ch["w2"], ch["bbar"]
            ch["y0"] = _dot(jnp.concatenate([ch["a_rk"], ch["a_rb"]], axis=1).astype(BF16),
                            jnp.concatenate([ch["bd_v"], bd(-w1)], axis=0))
            ch["rp"] = (ch["rt"] - _dot(ch["a_rb"].astype(BF16), bd(w2))).astype(BF16)
            ch["p_bd"] = (eye_q * ch["gam"] - head_blk * _dot_tn(
                bbar.astype(BF16), w2.astype(BF16))).astype(BF16)
            ch["q_bd"] = head_blk * _dot_tn(
                jnp.concatenate([ch["kbar"], -bbar], axis=0).astype(BF16),
                jnp.concatenate([ch["v"], w1], axis=0).astype(BF16))
        yield
        for b, rows, tok, chs in groups:
            ys = []
            for ch in chs:
                hb = h_ref[b, ch["i"]].astype(BF16)
                ys.append(_dot(ch["rp"], hb) + ch["y0"])
                h_ref[b, ch["i"]] = _dot(ch["p_bd"], hb) + ch["q_bd"]
            epilogue(b, rows, tok, jnp.concatenate(ys, axis=1))
            yield

    return stages


N_GLA_IN, N_RWKV_IN = 9, 12


def _recur_body(*refs, n_chunks, n_batch):
    gla_in = refs[:N_GLA_IN]
    rwkv_in = refs[N_GLA_IN:N_GLA_IN + N_RWKV_IN]
    ob_ref, oc_ref, ht_ref, h_ref, zlast_ref = refs[N_GLA_IN + N_RWKV_IN:]

    @pl.when(pl.program_id(0) == 0)
    def _():
        ht_ref[...] = jnp.zeros_like(ht_ref)
        h_ref[...] = jnp.zeros_like(h_ref)
        zlast_ref[...] = jnp.zeros_like(zlast_ref)

    gla = _gla_stages(*gla_in, ob_ref, ht_ref, n_batch)
    rwkv = _rwkv_stages(*rwkv_in, oc_ref, h_ref, zlast_ref, n_batch)

    def chunk(c, carry):
        live = [rwkv(c), gla(c)]
        while live:
            live = [g for g in live if next(g, True) is None]
        return carry

    lax.fori_loop(0, n_chunks // REC_CPI, chunk, 0)


def _recur(zb, zc, gp, rp, l, B, S, ts):
    ns = S // ts
    W, Q = RWKV_W, RWKV_QUAD
    blk = lambda n, j: pl.BlockSpec((B, ts, n), lambda s: (0, s, j))
    split = lambda c: pl.BlockSpec((None, 2, 128, c), lambda s: (l, 0, 0, 0))
    zb = zb.reshape(B, S, zb.shape[1])
    o_b, o_c = pl.pallas_call(
        functools.partial(_recur_body, n_chunks=ts // CHUNK, n_batch=B),
        grid=(ns,),
        in_specs=[
            blk(GLA_W, 0),
            blk(GLA_W, 1),
            blk(GLA_QK, 4),
            blk(GLA_QK, 5),
            blk(128, 12),
            split(GLA_QK), _vec_spec(GLA_QK, l), _vec_spec(GLA_W, l),
            pl.BlockSpec((GLA_LEVELS + 1, CHUNK, GLA_HEADS * CHUNK), lambda s: (0, 0, 0)),
            blk(RWKV_COLS_PAD, 0),
            _vec_spec(RWKV_COLS_PAD, l), _vec_spec(W, l), split(W),
            _vec_spec(W, l), split(W), _mat_spec(256, W, l),
            _vec_spec(W, l), _vec_spec(W, l), _vec_spec(W, l), _vec_spec(W, l), _vec_spec(W, l),
        ],
        out_specs=[blk(GLA_W, 0), blk(W, 0)],
        out_shape=[jax.ShapeDtypeStruct((B, S, GLA_W), BF16), jax.ShapeDtypeStruct((B, S, W), BF16)],
        scratch_shapes=[pltpu.VMEM((B, GLA_W, GLA_QK), F32),
                        pltpu.VMEM((B, RWKV_NQ, Q, Q), F32),
                        pltpu.VMEM((B, 8, RWKV_COLS_PAD), F32)],
        compiler_params=_cparams(1),
        name="recurrences",
    )(zb, zb, zb, zb, zb, gp["walpha"], gp["balpha"], gp["gn"], _gla_pair_masks(),
      zc.reshape(B, S, RWKV_COLS_PAD), rp["mu"], rp["w0"], rp["wdec"], rp["a0"], rp["wiclr"],
      rp["wgate"], rp["k_k"], rp["k_a"], rp["r_k"], rp["ln_w"], rp["ln_b"])
    return o_b.reshape(B * S, GLA_W), o_c.reshape(B * S, W)


def _merge_body(x_ref, oa_ref, ob_ref, oc_ref, gpre_ref, wd_ref, wa_ref, wb_ref, wc_ref, wo_ref,
                gpost_ref, o_ref):
    x = x_ref[...]
    D = D_MODEL
    h = _rms(x, gpre_ref[...]).astype(BF16)
    branches = ((oa_ref[...], wa_ref), (ob_ref[...], wb_ref), (oc_ref[...], wc_ref))
    y = None
    for c in range(0, D, MERGE_COLS):
        cols = slice(c, c + MERGE_COLS)
        merged = None
        for j, (o, w_r) in enumerate(branches):
            gate = _sigmoid(_dot(h, wd_ref[:, j * D + c:j * D + c + MERGE_COLS]))
            term = gate * _dot(o, w_r[:, cols])
            merged = term if merged is None else merged + term
        part = _dot(merged.astype(BF16), wo_ref[cols, :])
        y = part if y is None else y + part
    o_ref[...] = x + _rms(y, gpost_ref[...])


def _merge(x, oa, ob, oc, wd, wa, wb, wc, wo, ng, l, tm):
    T, D = x.shape
    tok = lambda n: pl.BlockSpec((tm, n), lambda i: (i, 0))
    res = lambda r, c: pl.BlockSpec((None, r, c), lambda i: (l, 0, 0), pipeline_mode=pl.Buffered(1))
    gain = lambda j: pl.BlockSpec((None, 1, D), lambda i: (l * 8 + j, 0, 0))
    return pl.pallas_call(
        _merge_body,
        grid=(T // tm,),
        in_specs=[tok(D), tok(MLA_W), tok(GLA_W), tok(RWKV_W), gain(2), res(D, N_BRANCH * D),
                  res(MLA_W, D), res(GLA_W, D), res(RWKV_W, D), res(D, D), gain(3)],
        out_specs=tok(D),
        out_shape=jax.ShapeDtypeStruct((T, D), F32),
        compiler_params=_cparams(1),
        name="merge",
    )(x, oa, ob, oc, ng, wd, wa, wb, wc, wo, ng)


def _mem_kv_body(mem_ref, g_ref, w_ref, o_ref):
    o_ref[...] = _dot(_rms(mem_ref[...], g_ref[...]).astype(BF16), w_ref[...]).astype(o_ref.dtype)


def _mem_kv(mem, mem_norm, wkv, l):
    B, M, D = mem.shape
    return pl.pallas_call(
        _mem_kv_body,
        grid=(B,),
        in_specs=[pl.BlockSpec((None, M, D), lambda b: (b, 0, 0)), _vec_spec(D, l),
                  _mat_spec(D, 2 * D, l)],
        out_specs=pl.BlockSpec((None, M, 2 * D), lambda b: (b, 0, 0)),
        out_shape=jax.ShapeDtypeStruct((B, M, 2 * D), BF16),
        compiler_params=_cparams(1),
        name="mem_kv",
    )(mem, mem_norm, wkv)


def _mem_attn_body(x_ref, gpre_ref, wq_ref, kv_ref, wo_ref, gpost_ref, o_ref):
    x = x_ref[...]
    D = D_MODEL
    h = _rms(x, gpre_ref[...]).astype(BF16)
    q = (_dot(h, wq_ref[...]) * (MEM_HD ** -0.5 * LOG2_E)).astype(BF16)
    kv = kv_ref[...]
    outs = []
    for hh in range(MEM_HEADS):
        sl = slice(hh * MEM_HD, (hh + 1) * MEM_HD)
        s = _dot_nt(q[:, sl], kv[:, sl])
        p = jnp.exp2(s - jnp.max(s, axis=-1, keepdims=True))
        o = _dot(p.astype(BF16), kv[:, D + hh * MEM_HD:D + (hh + 1) * MEM_HD])
        outs.append(o / jnp.sum(p, axis=-1, keepdims=True))
    o = jnp.concatenate(outs, axis=1).astype(BF16)
    o_ref[...] = x + _rms(_dot(o, wo_ref[...]), gpost_ref[...])


def _mem_attn(x, ng, wq, kv, wo, l, S, tm):
    T, D = x.shape
    M = kv.shape[1]
    per_b = S // tm
    return pl.pallas_call(
        _mem_attn_body,
        grid=(T // tm,),
        in_specs=[pl.BlockSpec((tm, D), lambda i: (i, 0)),
                  pl.BlockSpec((None, 1, D), lambda i: (l * 8 + 4, 0, 0)),
                  _mat_spec(D, D, l),
                  pl.BlockSpec((None, M, 2 * D), lambda i: (i // per_b, 0, 0)),
                  _mat_spec(D, D, l),
                  pl.BlockSpec((None, 1, D), lambda i: (l * 8 + 5, 0, 0))],
        out_specs=pl.BlockSpec((tm, D), lambda i: (i, 0)),
        out_shape=jax.ShapeDtypeStruct((T, D), F32),
        compiler_params=_cparams(1),
        name="mem_attn",
    )(x, ng, wq, kv, wo, ng)


def _prepare_params(w_in, mla_w_uq, mla_w_ukv, gla_w_alpha, gla_norm, rwkv_mu, rwkv_w_decay,
                    rwkv_w_iclr, rwkv_w_gate, w_branch):
    L, D = w_in.shape[0], w_in.shape[1]
    zc = lambda n: jnp.zeros((L, D, n), F32)
    o = 0
    cuts = {}
    for name, n in (("c_q", 256), ("c_kv", 128), ("k_rope", 32), ("gla_q", 256), ("gla_k", 256),
                    ("gla_v", 512), ("gla_g", 512), ("gla_a", 16), ("rwkv", 1824), ("gates", 3072)):
        cuts[name] = w_in[:, :, o:o + n]
        o += n
    kr = cuts["k_rope"]
    kr_b = jnp.concatenate([-kr[..., 16:], kr[..., :16]], axis=-1)
    wa = jnp.concatenate([cuts["c_q"], cuts["c_kv"], zc(64), kr, zc(32), zc(64), kr_b, zc(32)], -1)
    wb = jnp.concatenate([cuts["gla_v"], cuts["gla_g"], cuts["gla_q"], cuts["gla_k"],
                          cuts["gla_a"], zc(112)], -1)
    wc = jnp.concatenate([cuts["rwkv"], zc(RWKV_MISC - 288)], -1)
    wd = cuts["gates"]

    wuq = mla_w_uq.reshape(L, MLA_Q_RANK, MLA_HEADS, MLA_NOPE + MLA_ROPE)
    nope, rope = wuq[..., :MLA_NOPE], wuq[..., MLA_NOPE:]
    zq = lambda n: jnp.zeros((L, MLA_Q_RANK, MLA_HEADS, n), F32)
    wqa = jnp.concatenate([nope, rope, zq(32)], -1).reshape(L, MLA_Q_RANK, MLA_W)
    rope_b = jnp.concatenate([-rope[..., 16:], rope[..., :16]], -1)
    wqb = jnp.concatenate([zq(64), rope_b, zq(32)], -1).reshape(L, MLA_Q_RANK, MLA_W)
    wukv = mla_w_ukv.reshape(L, MLA_KV_RANK, MLA_HEADS, 128)
    zk = jnp.zeros((L, MLA_KV_RANK, MLA_HEADS, 64), F32)
    wk = jnp.concatenate([wukv[..., :64], zk], -1).reshape(L, MLA_KV_RANK, MLA_W)
    wv = jnp.concatenate([wukv[..., 64:], zk[..., :MLA_VSLOT - 64]], -1).reshape(
        L, MLA_KV_RANK, MLA_HEADS * MLA_VSLOT)

    bra = w_branch[:, :512].reshape(L, MLA_HEADS, 64, D)
    bra = jnp.concatenate([bra, jnp.zeros_like(bra)], axis=2).reshape(L, MLA_W, D)
    brb = w_branch[:, 512:512 + GLA_W]
    brc = w_branch[:, 512 + GLA_W:]

    walpha = jnp.concatenate(
        [gla_w_alpha, jnp.zeros((L, 128 - GLA_GATE_RANK, GLA_QK), F32)], axis=1)
    gn = jnp.tile(gla_norm, (1, GLA_HEADS))[:, None, :]
    mu = jnp.concatenate([rwkv_mu, jnp.zeros((L, RWKV_MISC - 288), F32)], -1)[:, None, :]
    zr = lambda n: jnp.zeros((L, n, RWKV_W), F32)
    def hi_lo(w):
        hi = w.astype(BF16)
        return jnp.stack([hi, (w - hi.astype(F32)).astype(BF16)], axis=1)

    wdec = hi_lo(jnp.concatenate([rwkv_w_decay, zr(64)], axis=1))
    wiclr = hi_lo(jnp.concatenate([zr(64), rwkv_w_iclr], axis=1))
    wgate = jnp.concatenate([rwkv_w_gate, zr(256 - RWKV_GATE_RANK)], axis=1)
    bf = lambda w: w.astype(BF16)
    return dict(wa=bf(wa), wb=bf(wb), wc=bf(wc), wd=bf(wd), wqa=bf(wqa), wqb=bf(wqb), wk=bf(wk),
                wv=bf(wv), bra=bf(bra), brb=bf(brb), brc=bf(brc), walpha=hi_lo(walpha), gn=gn, mu=mu,
                wdec=wdec, wiclr=wiclr, wgate=bf(wgate))


def kernel(x, mem, positions, norm_g, w_ffn_in, w_ffn_out, w_in, mla_q_norm, mla_w_uq, mla_kv_norm, mla_w_ukv, gla_w_alpha, gla_b_alpha, gla_norm, rwkv_mu, rwkv_w0, rwkv_w_decay, rwkv_a0, rwkv_w_iclr, rwkv_w_gate, rwkv_k_k, rwkv_k_a, rwkv_r_k, rwkv_ln_w, rwkv_ln_b, w_branch, w_out, mem_norm, mem_wq, mem_wkv, mem_wo):
    B, S, D = x.shape
    L = norm_g.shape[0]
    T = B * S
    tm = min(512, S)
    tq = min(1024, S)
    tk = tq // 2
    ts = min(512, S)
    tff = min(1024, S)
    nsp = 11

    pp = _prepare_params(w_in, mla_w_uq, mla_w_ukv, gla_w_alpha, gla_norm, rwkv_mu, rwkv_w_decay,
                         rwkv_w_iclr, rwkv_w_gate, w_branch)
    bf = lambda w: w.astype(BF16)
    ffn_in, ffn_out = bf(w_ffn_in), bf(w_ffn_out)
    wout, wq, wkv, wo = bf(w_out), bf(mem_wq), bf(mem_wkv), bf(mem_wo)
    ng = norm_g.reshape(L * 8, 1, D)
    row = lambda p: p[:, None, :]
    gl = dict(walpha=pp["walpha"], balpha=row(gla_b_alpha), gn=pp["gn"])
    rw = dict(mu=pp["mu"], w0=row(rwkv_w0), wdec=pp["wdec"], a0=row(rwkv_a0), wiclr=pp["wiclr"],
              wgate=pp["wgate"], k_k=row(rwkv_k_k), k_a=row(rwkv_k_a), r_k=row(rwkv_r_k),
              ln_w=row(rwkv_ln_w), ln_b=row(rwkv_ln_b))

    inv_freq = ROPE_THETA ** (-jnp.arange(0, MLA_ROPE, 2, dtype=F32) / MLA_ROPE)
    invf_lane = jnp.concatenate(
        [jnp.zeros((MLA_NOPE,), F32), inv_freq, inv_freq, jnp.zeros((32,), F32)])[None, :]
    c0, s0 = _rope_tables(positions.astype(F32).reshape(T, 1), invf_lane, tm)

    x = x.reshape(T, D)
    for l in range(L):
        x = _ffn(x, ng, ffn_in, ffn_out, l, 0, tff, nsp)
        zb, zc, q, k, vt = _mixer_proj(x, ng, l * 8 + 2, pp, c0, s0, row(mla_q_norm),
                                       row(mla_kv_norm), l, tk)
        o_a = _mla_attn(q, k, vt, B, S, tq, tk)
        o_b, o_c = _recur(zb, zc, gl, rw, l, B, S, ts)
        x = _merge(x, o_a, o_b, o_c, pp["wd"], pp["bra"], pp["brb"], pp["brc"], wout, ng, l, tff)
        kv = _mem_kv(mem, row(mem_norm), wkv, l)
        x = _mem_attn(x, ng, wq, kv, wo, l, S, tff)
        x = _ffn(x, ng, ffn_in, ffn_out, l, 1, tff, nsp)
    return x.reshape(B, S, D)
```
